```python
import math
import jax, jax.numpy as jnp
from jax import lax
import numpy as np

D_MODEL = 1024
BATCH = 8
SEQ = 2048
DEPTH = 1
DEC_BATCH = 32
DEC_SEQ = 8
PAST_LEN = 8192
PAGE_SIZE = 128

GLA_HEADS = 4
GLA_DK = D_MODEL // 2 // GLA_HEADS
GLA_DV = D_MODEL // GLA_HEADS
GLA_LOWRANK = 16
GLA_GATE_NORM = 16.0
GLA_CHUNK = 16
MOBA_HEADS = 8
MOBA_HD = 64
MOBA_BLOCK = 256
MOBA_TOPK = 3
MOBA_QCHUNK = 16
ROPE_THETA = 10000.0
N_EXPERTS = 32
TOP_K = 4
D_FF = D_MODEL
SWIGLU_LIMIT = 7.0
SWIGLU_ALPHA = 1.702
MOE_BLOCK = 128
EPS = 1e-6

GLA_KW = GLA_HEADS * GLA_DK
GLA_VW = GLA_HEADS * GLA_DV
MOBA_W = MOBA_HEADS * MOBA_HD
IN_SPLITS = (GLA_KW, GLA_KW, GLA_VW, GLA_VW, GLA_LOWRANK, MOBA_W, MOBA_W, MOBA_W, D_MODEL, D_MODEL)
IN_WIDTH = 2 * GLA_KW + 2 * GLA_VW + GLA_LOWRANK + 3 * MOBA_W + 2 * D_MODEL

kernel_name = 'hybrid_gla_moba_moe_decode_step'


def split_cols(t, sizes):
    out, start = [], 0
    for s in sizes:
        out.append(t[..., start:start + s])
        start += s
    return out


def rms_norm(x, g):
    x32 = x.astype(jnp.float32)
    y = x32 * lax.rsqrt(jnp.mean(x32 * x32, axis=-1, keepdims=True) + EPS)
    return y.astype(x.dtype) * g


def rotary(x, pos):
    half = x.shape[-1] // 2
    inv = 1.0 / (ROPE_THETA ** (jnp.arange(half, dtype=jnp.float32) / half))
    ang = pos.astype(jnp.float32)[:, None] * inv[None, :]
    cos = jnp.cos(ang)[None, :, None, :]
    sin = jnp.sin(ang)[None, :, None, :]
    x32 = x.astype(jnp.float32)
    x1, x2 = x32[..., :half], x32[..., half:]
    return jnp.concatenate([x1 * cos - x2 * sin, x2 * cos + x1 * sin], axis=-1).astype(x.dtype)


def gla_recurrence(q, k, v, log_a, s0):
    B, L, H, _ = q.shape
    DV = v.shape[-1]
    n = -(-L // GLA_CHUNK)
    pad = n * GLA_CHUNK - L

    def blocks(t):
        t = jnp.pad(t.astype(jnp.float32), ((0, 0), (0, pad), (0, 0), (0, 0)))
        return t.reshape(B, n, GLA_CHUNK, H, t.shape[-1]).transpose(1, 0, 3, 2, 4)

    causal = jnp.tril(jnp.ones((GLA_CHUNK, GLA_CHUNK), dtype=bool))

    def step(S, inp):
        qc, kc, vc, gc = inp
        b = jnp.cumsum(gc, axis=2)
        o_inter = jnp.einsum('bhtk,bhkv->bhtv', qc * jnp.exp(b), S)
        rel = jnp.where(causal[:, :, None], b[:, :, :, None, :] - b[:, :, None, :, :], -jnp.inf)
        att = jnp.einsum('bhtk,bhsk,bhtsk->bhts', qc, kc, jnp.exp(rel))
        o = o_inter + jnp.einsum('bhts,bhsv->bhtv', att, vc)
        b_last = b[:, :, -1:, :]
        S = jnp.exp(b_last[:, :, 0, :])[..., None] * S + jnp.einsum('bhsk,bhsv->bhkv', kc * jnp.exp(b_last - b), vc)
        return S, o

    S, o = lax.scan(step, s0.astype(jnp.float32), (blocks(q), blocks(k), blocks(v), blocks(log_a)))
    o = o.transpose(1, 0, 3, 2, 4).reshape(B, n * GLA_CHUNK, H, DV)[:, :L]
    return o.astype(v.dtype), S.astype(s0.dtype)


def moba_attention(q, k, v, pos0):
    B, L, H, hd = q.shape
    T = k.shape[1]
    nbk = -(-T // MOBA_BLOCK)
    pad = nbk * MOBA_BLOCK - T
    kb = jnp.pad(k, ((0, 0), (0, pad), (0, 0), (0, 0))).reshape(B, nbk, MOBA_BLOCK, H, hd)
    vb = jnp.pad(v, ((0, 0), (0, pad), (0, 0), (0, 0))).reshape(B, nbk, MOBA_BLOCK, H, hd)
    k_mean = jnp.mean(kb.astype(jnp.float32), axis=2)
    n_sel = min(MOBA_TOPK, nbk)
    qc = math.gcd(L, MOBA_QCHUNK)
    n_chunks = L // qc
    pos = pos0 + jnp.arange(L, dtype=jnp.int32)
    q_chunks = q.reshape(B, n_chunks, qc, H, hd).transpose(1, 0, 2, 3, 4)
    pos_chunks = pos.reshape(n_chunks, qc)
    blk_ids = jnp.arange(nbk, dtype=jnp.int32)
    offs = jnp.arange(MOBA_BLOCK, dtype=jnp.int32)
    b_idx = jnp.arange(B)[:, None, None, None]
    h_idx = jnp.arange(H)[None, :, None, None]
    scale = MOBA_HD ** -0.5

    def attend(args):
        qq, pp = args
        own = pp // MOBA_BLOCK
        gate = jnp.einsum('bqhd,bjhd->bhqj', qq.astype(jnp.float32), k_mean)
        gate = jnp.where(blk_ids[None, None, None, :] < own[None, None, :, None], gate, -jnp.inf)
        _, top = lax.top_k(gate, n_sel)
        own_b = jnp.broadcast_to(own[None, None, :, None], (B, H, qc, 1))
        sel = jnp.concatenate([top, own_b], axis=-1)
        keep = jnp.concatenate([top < own_b, jnp.ones_like(own_b, dtype=bool)], axis=-1)
        kg = kb[b_idx, sel, :, h_idx, :]
        vg = vb[b_idx, sel, :, h_idx, :]
        kpos = sel[..., None] * MOBA_BLOCK + offs
        mask = keep[..., None] & (kpos <= pp[None, None, :, None, None])
        logits = jnp.einsum('bqhd,bhqsnd->bhqsn', qq, kg).astype(jnp.float32) * scale
        logits = jnp.where(mask, logits, -jnp.inf).reshape(B, H, qc, -1)
        p = jax.nn.softmax(logits, axis=-1).reshape(mask.shape).astype(v.dtype)
        return jnp.einsum('bhqsn,bhqsnd->bqhd', p, vg)

    out = lax.map(attend, (q_chunks, pos_chunks))
    return out.transpose(1, 0, 2, 3, 4).reshape(B, L, H, hd)


def moe_ffn(h, w_router, b_router, w_gate_up, b_gate_up, w_down, b_down):
    T, D = h.shape
    logits = (h @ w_router + b_router).astype(jnp.float32)
    top_val, top_idx = lax.top_k(logits, TOP_K)
    gates = jax.nn.softmax(top_val, axis=-1)
    n_assign = T * TOP_K
    flat_e = top_idx.reshape(-1).astype(jnp.int32)
    order = jnp.argsort(flat_e).astype(jnp.int32)
    sorted_e = flat_e[order]
    sorted_tok = order // TOP_K
    counts = jnp.bincount(flat_e, length=N_EXPERTS).astype(jnp.int32)
    n_blocks_e = (counts + MOE_BLOCK - 1) // MOE_BLOCK
    blk_end = jnp.cumsum(n_blocks_e)
    blk_start = blk_end - n_blocks_e
    grp_start = jnp.cumsum(counts) - counts
    dest = blk_start[sorted_e] * MOE_BLOCK + (jnp.arange(n_assign, dtype=jnp.int32) - grp_start[sorted_e])
    n_blocks = -(-n_assign // MOE_BLOCK) + N_EXPERTS
    n_rows = n_blocks * MOE_BLOCK
    row_tok = jnp.full((n_rows,), T, dtype=jnp.int32).at[dest].set(sorted_tok)
    block_expert = jnp.minimum(jnp.searchsorted(blk_end, jnp.arange(n_blocks, dtype=jnp.int32), side='right'), N_EXPERTS - 1)
    h_pad = jnp.concatenate([h, jnp.zeros((1, D), h.dtype)], axis=0)
    xb = h_pad[row_tok].reshape(n_blocks, MOE_BLOCK, D)

    def expert_block(args):
        xe, e = args
        gu = xe @ w_gate_up[e] + b_gate_up[e]
        glu, lin = gu[:, :D_FF], gu[:, D_FF:]
        glu = jnp.minimum(glu, SWIGLU_LIMIT)
        lin = jnp.clip(lin, -SWIGLU_LIMIT, SWIGLU_LIMIT)
        act = glu * jax.nn.sigmoid(SWIGLU_ALPHA * glu) * (lin + 1.0)
        return act @ w_down[e] + b_down[e]

    yb = lax.map(expert_block, (xb, block_expert)).reshape(n_rows, D)
    y_sorted = yb[dest]
    g_sorted = gates.reshape(-1)[order].astype(y_sorted.dtype)
    return jax.ops.segment_sum(y_sorted * g_sorted[:, None], sorted_tok, num_segments=T)


def layer_forward(x, c, gla_s0, past_k, past_v, w_ada, b_ada, g_norm1, w_in, w_gla_a2, b_gla_a2,
                  g_gla_head, w_proj_gla, w_proj_moba, w_out, g_norm2, w_router, b_router,
                  w_gate_up, b_gate_up, w_down, b_down):
    B, L, D = x.shape
    pos0 = past_k.shape[1]
    mod = jax.nn.silu(c) @ w_ada + b_ada
    sh1, sc1, gt1, sh2, sc2, gt2 = jnp.split(mod[:, None, :], 6, axis=-1)
    h = rms_norm(x, g_norm1) * (1.0 + sc1) + sh1
    proj = h @ w_in
    gq, gk, gv, gr, ga, mq, mk, mv, ba, bb = split_cols(proj, IN_SPLITS)
    q_g = gq.reshape(B, L, GLA_HEADS, GLA_DK) * (GLA_DK ** -0.5)
    k_g = gk.reshape(B, L, GLA_HEADS, GLA_DK)
    v_g = gv.reshape(B, L, GLA_HEADS, GLA_DV)
    log_a = jax.nn.log_sigmoid((ga @ w_gla_a2 + b_gla_a2).astype(jnp.float32)) / GLA_GATE_NORM
    log_a = log_a.reshape(B, L, GLA_HEADS, GLA_DK)
    o_g, s_new = gla_recurrence(q_g, k_g, v_g, log_a, gla_s0)
    o_g = rms_norm(o_g, g_gla_head).reshape(B, L, GLA_VW) * jax.nn.silu(gr)
    y_a = o_g @ w_proj_gla
    pos = pos0 + jnp.arange(L, dtype=jnp.int32)
    q_m = rotary(mq.reshape(B, L, MOBA_HEADS, MOBA_HD), pos)
    k_m = rotary(mk.reshape(B, L, MOBA_HEADS, MOBA_HD), pos)
    v_m = mv.reshape(B, L, MOBA_HEADS, MOBA_HD)
    o_m = moba_attention(q_m, jnp.concatenate([past_k, k_m], axis=1), jnp.concatenate([past_v, v_m], axis=1), pos0)
    y_b = o_m.reshape(B, L, MOBA_W) @ w_proj_moba
    mix = jax.nn.sigmoid(ba) * y_a + jax.nn.sigmoid(bb) * y_b
    x = x + gt1 * (mix @ w_out)
    h2 = rms_norm(x, g_norm2) * (1.0 + sc2) + sh2
    y_ff = moe_ffn(h2.reshape(B * L, D), w_router, b_router, w_gate_up, b_gate_up, w_down, b_down)
    x = x + gt2 * y_ff.reshape(B, L, D)
    return x, k_m, v_m, s_new


def setup_inputs(seed: int = 0) -> dict:
    key = jax.random.key(seed)
    ks = jax.random.split(key, 32)
    n_pages = PAST_LEN // PAGE_SIZE
    n_used = DEC_BATCH * n_pages
    n_pool = n_used + (n_used + 3) // 4

    def nrm(k, shape, scale):
        return jax.random.normal(k, shape, jnp.float32) * scale

    D = D_MODEL
    return {
        'x_prompt': nrm(ks[0], (BATCH, SEQ, D), 1.0),
        'x_sample': nrm(ks[1], (DEC_BATCH, DEC_SEQ, D), 1.0),
        'cache_k': nrm(ks[2], (DEPTH, n_pool, PAGE_SIZE, MOBA_HEADS, MOBA_HD), 1.0),
        'cache_v': nrm(ks[3], (DEPTH, n_pool, PAGE_SIZE, MOBA_HEADS, MOBA_HD), 1.0),
        'state_gla': nrm(ks[4], (DEPTH, DEC_BATCH, GLA_HEADS, GLA_DK, GLA_DV), 1.0),
        'page_table': jax.random.permutation(ks[5], n_pool)[:n_used].reshape(DEC_BATCH, n_pages).astype(jnp.int32),
        'c_prompt': nrm(ks[6], (BATCH, D), 1.0),
        'c_sample': nrm(ks[7], (DEC_BATCH, D), 1.0),
        'w_ada': nrm(ks[8], (DEPTH, D, 6 * D), D ** -0.5),
        'b_ada': nrm(ks[9], (DEPTH, 6 * D), 0.01),
        'g_norm1': 1.0 + nrm(ks[10], (DEPTH, D), 0.02),
        'w_in': nrm(ks[11], (DEPTH, D, IN_WIDTH), D ** -0.5),
        'w_gla_a2': nrm(ks[12], (DEPTH, GLA_LOWRANK, GLA_KW), GLA_LOWRANK ** -0.5),
        'b_gla_a2': nrm(ks[13], (DEPTH, GLA_KW), 0.1),
        'g_gla_head': 1.0 + nrm(ks[14], (DEPTH, GLA_DV), 0.02),
        'w_proj_gla': nrm(ks[15], (DEPTH, GLA_VW, D), GLA_VW ** -0.5),
        'w_proj_moba': nrm(ks[16], (DEPTH, MOBA_W, D), MOBA_W ** -0.5),
        'w_out': nrm(ks[17], (DEPTH, D, D), D ** -0.5),
        'g_norm2': 1.0 + nrm(ks[18], (DEPTH, D), 0.02),
        'w_router': nrm(ks[19], (DEPTH, D, N_EXPERTS), D ** -0.5),
        'b_router': nrm(ks[20], (DEPTH, N_EXPERTS), 0.01),
        'w_gate_up': nrm(ks[21], (DEPTH, N_EXPERTS, D, 2 * D_FF), D ** -0.5),
        'b_gate_up': nrm(ks[22], (DEPTH, N_EXPERTS, 2 * D_FF), 0.01),
        'w_down': nrm(ks[23], (DEPTH, N_EXPERTS, D_FF, D), D_FF ** -0.5),
        'b_down': nrm(ks[24], (DEPTH, N_EXPERTS, D), 0.01),
        'g_final': 1.0 + nrm(ks[25], (D,), 0.02),
    }


def reference(x_prompt, x_sample, cache_k, cache_v, state_gla, page_table, c_prompt, c_sample,
              w_ada, b_ada, g_norm1, w_in, w_gla_a2, b_gla_a2, g_gla_head, w_proj_gla, w_proj_moba,
              w_out, g_norm2, w_router, b_router, w_gate_up, b_gate_up, w_down, b_down, g_final):
    nb_p = x_prompt.shape[0]
    nb_s = x_sample.shape[0]
    n_pages = page_table.shape[1]
    yp, ys = x_prompt, x_sample
    kp_l, vp_l, sp_l, ks_l, vs_l, ss_l = [], [], [], [], [], []
    for l in range(DEPTH):
        wl = (w_ada[l], b_ada[l], g_norm1[l], w_in[l], w_gla_a2[l], b_gla_a2[l], g_gla_head[l],
              w_proj_gla[l], w_proj_moba[l], w_out[l], g_norm2[l], w_router[l], b_router[l],
              w_gate_up[l], b_gate_up[l], w_down[l], b_down[l])
        empty = jnp.zeros((nb_p, 0, MOBA_HEADS, MOBA_HD), x_prompt.dtype)
        s0 = jnp.zeros((nb_p, GLA_HEADS, GLA_DK, GLA_DV), state_gla.dtype)
        yp, kp, vp, sp = layer_forward(yp, c_prompt, s0, empty, empty, *wl)
        past_k = cache_k[l][page_table].reshape(nb_s, n_pages * PAGE_SIZE, MOBA_HEADS, MOBA_HD)
        past_v = cache_v[l][page_table].reshape(nb_s, n_pages * PAGE_SIZE, MOBA_HEADS, MOBA_HD)
        ys, kn, vn, sn = layer_forward(ys, c_sample, state_gla[l], past_k, past_v, *wl)
        kp_l.append(kp); vp_l.append(vp); sp_l.append(sp)
        ks_l.append(kn); vs_l.append(vn); ss_l.append(sn)
    y_prompt = rms_norm(yp, g_final)
    y_sample = rms_norm(ys, g_final)
    return (y_prompt, y_sample, jnp.stack(kp_l), jnp.stack(vp_l), jnp.stack(sp_l),
            jnp.stack(ks_l), jnp.stack(vs_l), jnp.stack(ss_l))
```

```python
import functools

import jax
import jax.numpy as jnp
from jax import lax
from jax.experimental import pallas as pl
from jax.experimental.pallas import tpu as pltpu

F32 = jnp.float32
BF16 = jnp.bfloat16
I32 = jnp.int32

D_MODEL = 1024
GLA_HEADS = 4
GLA_DK = 128
GLA_DV = 256
GLA_LOWRANK = 16
GLA_GATE_NORM = 16.0
GLA_SUB = 16
MOBA_HEADS = 8
MOBA_HD = 64
MOBA_BLOCK = 256
MOBA_TOPK = 3
MOBA_W = MOBA_HEADS * MOBA_HD
ROPE_THETA = 10000.0
N_EXPERTS = 32
TOP_K = 4
D_FF = D_MODEL
SWIGLU_LIMIT = 7.0
SWIGLU_ALPHA = 1.702
EPS = 1e-6
LANES = 128
PROJ_TILE = 512

T_GQ, T_GK, T_GV, T_GR, T_GA, T_MQ, T_MK, T_MV, T_BA, T_BB = 0, 1, 2, 4, 6, 7, 8, 9, 10, 12
N_PROJ_TILES = 14
MOE_ROWS = 256
VMEM_LIMIT = 56 * 1024 * 1024

NEG_INF = float("-inf")


def _sigmoid(x):
    return 1.0 / (1.0 + jnp.exp(-x))


def _split3(x):
    hi = x.astype(BF16)
    r = x - hi.astype(F32)
    mid = r.astype(BF16)
    lo = (r - mid.astype(F32)).astype(BF16)
    return hi, mid, lo


def _dot(a, b):
    return jnp.dot(a, b, preferred_element_type=F32)


def _dot_nt(a, b):
    return lax.dot_general(a, b, (((1,), (1,)), ((), ())), preferred_element_type=F32)


def _dot_nt_f32(a, b):
    ah = a.astype(BF16)
    al = (a - ah.astype(F32)).astype(BF16)
    bh = b.astype(BF16)
    bl = (b - bh.astype(F32)).astype(BF16)
    return _dot_nt(ah, bh) + _dot_nt(ah, bl) + _dot_nt(al, bh)


def _div(x, n):
    assert n & (n - 1) == 0
    return lax.shift_right_logical(x, n.bit_length() - 1)


def _mod(x, n):
    assert n & (n - 1) == 0
    return x & (n - 1)


def _cparams(sem):
    return pltpu.CompilerParams(dimension_semantics=sem, vmem_limit_bytes=VMEM_LIMIT)


def _ada_body(c_ref, w_ref, b_ref, o_ref):
    c = c_ref[...]
    s = c * _sigmoid(c)
    o_ref[...] = _dot(s.astype(BF16), w_ref[...].astype(BF16)) + b_ref[...]


def _ada(c, w, b):
    n = w.shape[1]
    tn = n // 4
    return pl.pallas_call(
        _ada_body,
        grid=(4,),
        in_specs=[pl.BlockSpec(c.shape, lambda j: (0, 0)),
                  pl.BlockSpec((w.shape[0], tn), lambda j: (0, j)),
                  pl.BlockSpec((1, tn), lambda j: (0, j))],
        out_specs=pl.BlockSpec((c.shape[0], tn), lambda j: (0, j)),
        out_shape=jax.ShapeDtypeStruct((c.shape[0], n), F32),
        compiler_params=_cparams(("arbitrary",)),
        name="ada",
    )(c, w, b.reshape(1, n))


def _rotary(x, cos, sin_signed):
    lane = lax.broadcasted_iota(I32, x.shape, 1)
    half = MOBA_HD // 2
    partner = jnp.where((lane & (MOBA_HD - 1)) < half, lane + half, lane - half)
    w = x.shape[1]
    r1 = pltpu.roll(x, half, 1)
    i1 = pltpu.roll(lane, half, 1)
    r2 = pltpu.roll(x, w - half, 1)
    swapped = jnp.where(i1 == partner, r1, r2)
    return x * cos + swapped * sin_signed


def _log_sigmoid(x):
    return jnp.minimum(x, 0.0) - jnp.log(1.0 + jnp.exp(-jnp.abs(x)))


def _inproj_body(x_ref, sc_ref, sh_ref, g_ref, w_ref, w2_ref, b2_ref, cos_ref, sin_ref, o_ref, h_scr):
    j = pl.program_id(1)

    @pl.when(j == 0)
    def _():
        x = x_ref[...]
        ms = jnp.mean(x * x, axis=-1, keepdims=True)
        y = x * lax.rsqrt(ms + EPS) * g_ref[...]
        h = y * (1.0 + sc_ref[0]) + sh_ref[0]
        h_scr[...] = h.astype(BF16)

    acc = _dot(h_scr[...], w_ref[...])
    is_rot = jnp.logical_or(j == T_MQ, j == T_MK)
    is_la = j == T_GA

    @pl.when(is_rot)
    def _():
        o_ref[0] = _rotary(acc, cos_ref[...], sin_ref[...])

    @pl.when(is_la)
    def _():
        z = _dot(acc.astype(BF16), w2_ref[...]) + b2_ref[...]
        o_ref[0] = _log_sigmoid(z) * (1.0 / GLA_GATE_NORM)

    @pl.when(jnp.logical_not(jnp.logical_or(is_rot, is_la)))
    def _():
        o_ref[0] = acc


def _inproj(x2, sc, sh, g1, w_in_p, w2_p, b2, cos, sin, tm, rows_per_mod):
    r, d = x2.shape
    nt = r // tm
    tiles_per_mod = max(rows_per_mod // tm, 1)
    tiles_per_tab = cos.shape[0] // tm
    mod_block = (1,) + sc.shape[1:]
    return pl.pallas_call(
        _inproj_body,
        grid=(nt, N_PROJ_TILES),
        in_specs=[pl.BlockSpec((tm, d), lambda i, j: (i, 0)),
                  pl.BlockSpec(mod_block, lambda i, j: (i // tiles_per_mod, 0, 0)),
                  pl.BlockSpec(mod_block, lambda i, j: (i // tiles_per_mod, 0, 0)),
                  pl.BlockSpec((1, d), lambda i, j: (0, 0)),
                  pl.BlockSpec((d, PROJ_TILE), lambda i, j: (0, j)),
                  pl.BlockSpec((PROJ_TILE, PROJ_TILE), lambda i, j: (0, 0)),
                  pl.BlockSpec((1, PROJ_TILE), lambda i, j: (0, 0)),
                  pl.BlockSpec((tm, PROJ_TILE), lambda i, j: (i % tiles_per_tab, 0)),
                  pl.BlockSpec((tm, PROJ_TILE), lambda i, j: (i % tiles_per_tab, 0))],
        out_specs=pl.BlockSpec((1, tm, PROJ_TILE), lambda i, j: (j, i, 0)),
        out_shape=jax.ShapeDtypeStruct((N_PROJ_TILES, r, PROJ_TILE), F32),
        scratch_shapes=[pltpu.VMEM((tm, d), BF16)],
        compiler_params=_cparams(("arbitrary", "arbitrary")),
        name="inproj",
    )(x2, sc, sh, g1, w_in_p, w2_p, b2, cos, sin)


def _gla_body(q_ref, k_ref, v_ref, la_ref, gr_ref, s0_ref, gh_ref, og_ref, sout_ref, st_scr, *, C, SB):
    c = pl.program_id(1)
    nc = pl.num_programs(1)

    @pl.when(c == 0)
    def _():
        for h in range(GLA_HEADS):
            st_scr[h] = s0_ref[0, h].T

    la = la_ref[0]
    row_c = lax.broadcasted_iota(I32, (C, C), 0)
    col_c = lax.broadcasted_iota(I32, (C, C), 1)
    tri = jnp.where(row_c >= col_c, 1.0, 0.0).astype(BF16)
    hi, mid, lo = _split3(la)
    b_all = _dot(tri, hi) + _dot(tri, mid) + _dot(tri, lo)
    q_all = q_ref[0] * (GLA_DK ** -0.5)
    k_all = k_ref[0]
    gh = gh_ref[...]
    row_k = lax.broadcasted_iota(I32, (C, GLA_DK), 0)
    row_sb = lax.broadcasted_iota(I32, (SB, 1), 0)
    lane_sb = lax.broadcasted_iota(I32, (SB, C), 1)

    for h in range(GLA_HEADS):
        ks = slice(h * GLA_DK, (h + 1) * GLA_DK)
        vs = slice((h % 2) * GLA_DV, (h % 2 + 1) * GLA_DV)
        b = b_all[:, ks]
        q = q_all[:, ks]
        k = k_all[:, ks]
        v = v_ref[h // 2][:, vs]
        st = st_scr[h]
        b_last = b[C - 1:C, :]
        o = _dot_nt((q * jnp.exp(b)).astype(BF16), st.astype(BF16))
        att_rows = []
        for i in range(C // SB):
            r0 = i * SB
            bi = b[r0:r0 + SB]
            qi = q[r0:r0 + SB]
            ki = k[r0:r0 + SB]
            if i > 0:
                bref = b[r0 - 1:r0, :]
                qt = qi * jnp.exp(bi - bref)
                kt = jnp.where(row_k < r0, k * jnp.exp(jnp.minimum(bref - b, 0.0)), 0.0)
                att_i = _dot_nt(qt.astype(BF16), kt.astype(BF16))
            else:
                att_i = jnp.zeros((SB, C), F32)
            for s in range(SB):
                dec = jnp.exp(jnp.minimum(bi - bi[s:s + 1, :], 0.0))
                col = jnp.sum(qi * ki[s:s + 1, :] * dec, axis=1, keepdims=True)
                col = jnp.where(row_sb >= s, col, 0.0)
                att_i = att_i + jnp.where(lane_sb == r0 + s, col, 0.0)
            att_rows.append(att_i)
        att = att_rows[0] if len(att_rows) == 1 else jnp.concatenate(att_rows, axis=0)
        vb = v.astype(BF16)
        o = o + _dot(att.astype(BF16), vb)
        kd = k * jnp.exp(b_last - b)
        st_scr[h] = st * jnp.exp(b_last) + _dot(v.T.astype(BF16), kd.astype(BF16))
        on = o * lax.rsqrt(jnp.mean(o * o, axis=-1, keepdims=True) + EPS) * gh
        gr = gr_ref[h // 2][:, vs]
        og_ref[:, h * GLA_DV:(h + 1) * GLA_DV] = (on * (gr * _sigmoid(gr))).astype(og_ref.dtype)

    @pl.when(c == nc - 1)
    def _():
        for h in range(GLA_HEADS):
            sout_ref[0, h] = st_scr[h].T


def _gla(proj3, s0, g_head, n_seq, seq_len, chunk):
    r = proj3.shape[1]
    nc = seq_len // chunk
    sb = min(GLA_SUB, chunk)
    out_dtype = BF16 if chunk % 16 == 0 else F32
    row = lambda b, c: b * nc + c
    body = functools.partial(_gla_body, C=chunk, SB=sb)
    return pl.pallas_call(
        body,
        grid=(n_seq, nc),
        in_specs=[pl.BlockSpec((1, chunk, PROJ_TILE), lambda b, c: (T_GQ, row(b, c), 0)),
                  pl.BlockSpec((1, chunk, PROJ_TILE), lambda b, c: (T_GK, row(b, c), 0)),
                  pl.BlockSpec((2, chunk, PROJ_TILE), lambda b, c: (T_GV // 2, row(b, c), 0)),
                  pl.BlockSpec((1, chunk, PROJ_TILE), lambda b, c: (T_GA, row(b, c), 0)),
                  pl.BlockSpec((2, chunk, PROJ_TILE), lambda b, c: (T_GR // 2, row(b, c), 0)),
                  pl.BlockSpec((1, GLA_HEADS, GLA_DK, GLA_DV), lambda b, c: (b, 0, 0, 0)),
                  pl.BlockSpec((1, GLA_DV), lambda b, c: (0, 0))],
        out_specs=[pl.BlockSpec((chunk, GLA_HEADS * GLA_DV), lambda b, c: (row(b, c), 0)),
                   pl.BlockSpec((1, GLA_HEADS, GLA_DK, GLA_DV), lambda b, c: (b, 0, 0, 0))],
        out_shape=[jax.ShapeDtypeStruct((r, GLA_HEADS * GLA_DV), out_dtype),
                   jax.ShapeDtypeStruct((n_seq, GLA_HEADS, GLA_DK, GLA_DV), F32)],
        scratch_shapes=[pltpu.VMEM((GLA_HEADS, GLA_DV, GLA_DK), F32)],
        compiler_params=_cparams(("arbitrary", "arbitrary")),
        name="gla",
    )(proj3, proj3, proj3, proj3, proj3, s0, g_head)


def _select_topk(g, n_valid_rows, n_rows, n_sel):
    row = lax.broadcasted_iota(I32, g.shape, 0)
    valid = row < n_valid_rows
    gm = jnp.where(valid, g, NEG_INF)
    rank = jnp.zeros(g.shape, F32)
    for jp in range(n_rows):
        gj = gm[jp:jp + 1, :]
        ahead = jnp.logical_or(gj > gm, jnp.logical_and(gj == gm, jp < row))
        rank = rank + jnp.where(ahead, 1.0, 0.0)
    return jnp.where(jnp.logical_and(valid, rank < n_sel), 1.0, 0.0)


def _moba_p_body(q_ref, k_ref, v_ref, o_ref, k_scr, vt_scr, km_scr, sel_scr, *, nb, nbp):
    i = pl.program_id(1)
    blk = MOBA_BLOCK
    scale = MOBA_HD ** -0.5

    @pl.when(i == 0)
    def _():
        km_scr[...] = jnp.zeros(km_scr.shape, F32)
        for j in range(nb):
            kj = k_ref[0, j * blk:(j + 1) * blk, :]
            k_scr[j] = kj.astype(BF16)
            km_scr[j:j + 1, :] = jnp.mean(kj, axis=0, keepdims=True)
            vt_scr[j] = v_ref[0, j * blk:(j + 1) * blk, :].T.astype(BF16)

    q = q_ref[0]
    km = km_scr[...]
    km_rep = jnp.concatenate([km] * MOBA_HEADS, axis=0)
    wrow = lax.broadcasted_iota(I32, km_rep.shape, 0)
    wlane = lax.broadcasted_iota(I32, km_rep.shape, 1)
    wt = jnp.where(_div(wlane, MOBA_HD) == _div(wrow, nbp), km_rep, 0.0)
    gates_t = _dot_nt_f32(wt, q)
    for h in range(MOBA_HEADS):
        sel_scr[h] = _select_topk(gates_t[h * nbp:(h + 1) * nbp, :], i, nb, MOBA_TOPK)

    key_row = lax.broadcasted_iota(I32, (blk, blk), 0)
    q_lane = lax.broadcasted_iota(I32, (blk, blk), 1)
    pair_lane = lax.broadcasted_iota(I32, (blk, LANES), 1)
    outs = []
    for h in range(MOBA_HEADS):
        p0 = (h // 2) * LANES
        qp = q[:, p0:p0 + LANES]
        qm = jnp.where(_div(pair_lane, MOBA_HD) == h % 2, qp, 0.0).astype(BF16)

        def scores(j):
            kj = k_scr[j, :, p0:p0 + LANES]
            return _dot_nt(kj, qm) * scale

        def values(j):
            return vt_scr[j, h * MOBA_HD:(h + 1) * MOBA_HD, :]

        s = jnp.where(key_row <= q_lane, scores(i), NEG_INF)
        m = jnp.max(s, axis=0, keepdims=True)
        p = jnp.exp(s - m)
        l = jnp.sum(p, axis=0, keepdims=True)
        acc = _dot(values(i), p.astype(BF16))

        def step(j, carry):
            m, l, acc = carry
            keep = sel_scr[h, pl.ds(j, 1), :]
            s = jnp.where(keep > 0.5, scores(j), NEG_INF)
            m_new = jnp.maximum(m, jnp.max(s, axis=0, keepdims=True))
            alpha = jnp.exp(m - m_new)
            p = jnp.exp(s - m_new)
            l = l * alpha + jnp.sum(p, axis=0, keepdims=True)
            acc = acc * alpha + _dot(values(j), p.astype(BF16))
            return m_new, l, acc

        m, l, acc = lax.fori_loop(0, i, step, (m, l, acc))
        outs.append(acc / l)
    out_t = jnp.concatenate(outs, axis=0)
    o_ref[...] = out_t.T.astype(o_ref.dtype)


def _moba_prompt(proj3, n_seq, seq_len):
    r = proj3.shape[1]
    nb = seq_len // MOBA_BLOCK
    nbp = -(-nb // 8) * 8
    body = functools.partial(_moba_p_body, nb=nb, nbp=nbp)
    return pl.pallas_call(
        body,
        grid=(n_seq, nb),
        in_specs=[pl.BlockSpec((1, MOBA_BLOCK, MOBA_W), lambda b, i: (T_MQ, b * nb + i, 0)),
                  pl.BlockSpec((1, seq_len, MOBA_W), lambda b, i: (T_MK, b, 0)),
                  pl.BlockSpec((1, seq_len, MOBA_W), lambda b, i: (T_MV, b, 0))],
        out_specs=pl.BlockSpec((MOBA_BLOCK, MOBA_W), lambda b, i: (b * nb + i, 0)),
        out_shape=jax.ShapeDtypeStruct((r, MOBA_W), BF16),
        scratch_shapes=[pltpu.VMEM((nb, MOBA_BLOCK, MOBA_W), BF16),
                        pltpu.VMEM((nb, MOBA_W, MOBA_BLOCK), BF16),
                        pltpu.VMEM((nbp, MOBA_W), F32),
                        pltpu.VMEM((MOBA_HEADS, nbp, MOBA_BLOCK), F32)],
        compiler_params=_cparams(("arbitrary", "arbitrary")),
        name="moba_prompt",
    )(proj3, proj3, proj3)


def _moba_s_body(pt_ref, ka_ref, kb_ref, va_ref, vb_ref, qn_ref, kn_ref, vn_ref, o_ref,
                 q2_scr, gate_scr, m_scr, l_scr, acc_scr, *, nbp, L):
    j = pl.program_id(1)
    blk = MOBA_BLOCK
    scale = MOBA_HD ** -0.5
    n_col = LANES

    @pl.when(j == 0)
    def _():
        qn = qn_ref[...]
        q2 = jnp.concatenate([qn] * (n_col // L), axis=0)
        row = lax.broadcasted_iota(I32, q2.shape, 0)
        lane = lax.broadcasted_iota(I32, q2.shape, 1)
        q2_scr[...] = jnp.where(_div(lane, MOBA_HD) == _div(row, L), q2, 0.0)
        gate_scr[...] = jnp.full(gate_scr.shape, NEG_INF, F32)
        m_scr[...] = jnp.zeros(m_scr.shape, F32)
        l_scr[...] = jnp.zeros(l_scr.shape, F32)

    def block_partial(kblk, vblk, causal):
        q2 = q2_scr[...]
        s = _dot_nt(kblk.astype(BF16), q2.astype(BF16)) * scale
        if causal:
            key_row = lax.broadcasted_iota(I32, s.shape, 0)
            col = lax.broadcasted_iota(I32, s.shape, 1)
            s = jnp.where(key_row <= _mod(col, L), s, NEG_INF)
        m = jnp.max(s, axis=0, keepdims=True)
        p = jnp.exp(s - m)
        l = jnp.sum(p, axis=0, keepdims=True)
        acc = _dot(vblk.T.astype(BF16), p.astype(BF16))
        return m, l, acc

    @pl.when(j < nbp)
    def _():
        kblk = jnp.concatenate([ka_ref[0], kb_ref[0]], axis=0)
        vblk = jnp.concatenate([va_ref[0], vb_ref[0]], axis=0)
        km = jnp.mean(kblk, axis=0, keepdims=True)
        km8 = jnp.concatenate([km] * 8, axis=0)
        gate = _dot_nt_f32(km8, q2_scr[...])
        gate_scr[pl.ds(j, 1), :] = gate[0:1, :]
        m, l, acc = block_partial(kblk, vblk, False)
        m_scr[pl.ds(j, 1), :] = m
        l_scr[pl.ds(j, 1), :] = l
        acc_scr[j] = acc

    @pl.when(j == nbp)
    def _():
        pad = jnp.zeros((blk - L, MOBA_W), F32)
        m_own, l_own, acc_own = block_partial(jnp.concatenate([kn_ref[...], pad], axis=0),
                                              jnp.concatenate([vn_ref[...], pad], axis=0), True)
        n_sel = min(MOBA_TOPK, nbp + 1)
        sel = _select_topk(gate_scr[...], nbp, nbp, n_sel)
        m_all = m_scr[...]
        m_top = jnp.maximum(m_own, jnp.max(jnp.where(sel > 0.5, m_all, NEG_INF), axis=0, keepdims=True))
        w = jnp.where(sel > 0.5, jnp.exp(m_all - m_top), 0.0)
        w_own = jnp.exp(m_own - m_top)
        den = jnp.sum(w * l_scr[...], axis=0, keepdims=True) + w_own * l_own
        num = w_own * acc_own
        for jj in range(nbp):
            num = num + w[jj:jj + 1, :] * acc_scr[jj]
        out_t = num / den
        row = lax.broadcasted_iota(I32, out_t.shape, 0)
        col = lax.broadcasted_iota(I32, out_t.shape, 1)
        out_m = jnp.where(_div(row, MOBA_HD) == _div(col, L), out_t, 0.0)
        prow = lax.broadcasted_iota(I32, (L, n_col), 0)
        pcol = lax.broadcasted_iota(I32, (L, n_col), 1)
        pick = jnp.where(jnp.logical_and(_mod(pcol, L) == prow, pcol < MOBA_HEADS * L), 1.0, 0.0).astype(BF16)
        hi, mid, lo = _split3(out_m)
        o_ref[...] = _dot_nt(pick, hi) + _dot_nt(pick, mid) + _dot_nt(pick, lo)


def _moba_sample(proj3, cache_k, cache_v, page_table, n_seq, L):
    n_pool, page = cache_k.shape[0], cache_k.shape[1]
    n_pages = page_table.shape[1]
    ppb = MOBA_BLOCK // page
    nbp = n_pages // ppb
    nbp_pad = -(-nbp // 8) * 8
    assert ppb == 2 and n_pages % ppb == 0 and LANES % L == 0 and MOBA_HEADS * L <= LANES
    ck = cache_k.reshape(n_pool, page, MOBA_W)
    cv = cache_v.reshape(n_pool, page, MOBA_W)
    pt = page_table.reshape(-1).astype(I32)

    def page_map(off):
        return lambda b, j, pt_ref: (pt_ref[b * n_pages + ppb * jnp.minimum(j, nbp - 1) + off], 0, 0)

    new_map = lambda t: (lambda b, j, pt_ref: (t, b, 0))
    body = functools.partial(_moba_s_body, nbp=nbp, L=L)
    grid_spec = pltpu.PrefetchScalarGridSpec(
        num_scalar_prefetch=1,
        grid=(n_seq, nbp + 1),
        in_specs=[pl.BlockSpec((1, page, MOBA_W), page_map(0)),
                  pl.BlockSpec((1, page, MOBA_W), page_map(1)),
                  pl.BlockSpec((1, page, MOBA_W), page_map(0)),
                  pl.BlockSpec((1, page, MOBA_W), page_map(1)),
                  pl.BlockSpec((None, L, MOBA_W), new_map(T_MQ)),
                  pl.BlockSpec((None, L, MOBA_W), new_map(T_MK)),
                  pl.BlockSpec((None, L, MOBA_W), new_map(T_MV))],
        out_specs=pl.BlockSpec((L, MOBA_W), lambda b, j, pt_ref: (b, 0)),
        scratch_shapes=[pltpu.VMEM((LANES, MOBA_W), F32),
                        pltpu.VMEM((nbp_pad, LANES), F32),
                        pltpu.VMEM((nbp_pad, LANES), F32),
                        pltpu.VMEM((nbp_pad, LANES), F32),
                        pltpu.VMEM((nbp, MOBA_W, LANES), F32)])
    return pl.pallas_call(
        body,
        grid_spec=grid_spec,
        out_shape=jax.ShapeDtypeStruct((n_seq * L, MOBA_W), F32),
        compiler_params=_cparams(("arbitrary", "arbitrary")),
        name="moba_sample",
    )(pt, ck, ck, cv, cv, proj3, proj3, proj3)


def _post_body(og_ref, om_ref, ba_ref, bb_ref, x_ref, gt_ref, sc_ref, sh_ref, g2_ref, wpg_ref, wpm_ref,
               wo_ref, wr_ref, br_ref, x1_ref, h2_ref, lg_ref):
    ya = _dot(og_ref[...].astype(BF16), wpg_ref[...])
    yb = _dot(om_ref[...].astype(BF16), wpm_ref[...])
    ba = jnp.concatenate([ba_ref[0], ba_ref[1]], axis=1)
    bb = jnp.concatenate([bb_ref[0], bb_ref[1]], axis=1)
    mix = _sigmoid(ba) * ya + _sigmoid(bb) * yb
    x1 = x_ref[...] + gt_ref[0] * _dot(mix.astype(BF16), wo_ref[...])
    x1_ref[...] = x1
    ms = jnp.mean(x1 * x1, axis=-1, keepdims=True)
    h2 = x1 * lax.rsqrt(ms + EPS) * g2_ref[...] * (1.0 + sc_ref[0]) + sh_ref[0]
    h2_ref[...] = h2
    hh = h2.astype(BF16)
    hl = (h2 - hh.astype(F32)).astype(BF16)
    wr = wr_ref[...]
    wh = wr.astype(BF16)
    wl = (wr - wh.astype(F32)).astype(BF16)
    lg_ref[...] = _dot(hh, wh) + _dot(hh, wl) + _dot(hl, wh) + br_ref[...]


def _post(og, om, proj3, x2, gt, sc, sh, g2, wpg, wpm, wo, wr, br, tm, rows_per_mod):
    r, d = x2.shape
    nt = r // tm
    tiles_per_mod = max(rows_per_mod // tm, 1)
    mod_block = (1,) + gt.shape[1:]
    mod_map = lambda i: (i // tiles_per_mod, 0, 0)
    full = lambda a: pl.BlockSpec(a.shape, lambda i: (0,) * a.ndim)
    return pl.pallas_call(
        _post_body,
        grid=(nt,),
        in_specs=[pl.BlockSpec((tm, og.shape[1]), lambda i: (i, 0)),
                  pl.BlockSpec((tm, om.shape[1]), lambda i: (i, 0)),
                  pl.BlockSpec((2, tm, PROJ_TILE), lambda i: (T_BA // 2, i, 0)),
                  pl.BlockSpec((2, tm, PROJ_TILE), lambda i: (T_BB // 2, i, 0)),
                  pl.BlockSpec((tm, d), lambda i: (i, 0)),
                  pl.BlockSpec(mod_block, mod_map),
                  pl.BlockSpec(mod_block, mod_map),
                  pl.BlockSpec(mod_block, mod_map),
                  full(g2), full(wpg), full(wpm), full(wo), full(wr), full(br)],
        out_specs=[pl.BlockSpec((tm, d), lambda i: (i, 0)),
                   pl.BlockSpec((tm, d), lambda i: (i, 0)),
                   pl.BlockSpec((tm, LANES), lambda i: (i, 0))],
        out_shape=[jax.ShapeDtypeStruct((r, d), F32),
                   jax.ShapeDtypeStruct((r, d), F32),
                   jax.ShapeDtypeStruct((r, LANES), F32)],
        compiler_params=_cparams(("arbitrary",)),
        name="post",
    )(og, om, proj3, proj3, x2, gt, sc, sh, g2, wpg, wpm, wo, wr, br)


def _route_body(lg_ref, eidx_ref, rank_ref, gate_ref, cnt_ref, run_scr):
    i = pl.program_id(0)
    tm = lg_ref.shape[0]

    @pl.when(i == 0)
    def _():
        run_scr[...] = jnp.zeros(run_scr.shape, F32)

    l = lg_ref[...]
    lane = lax.broadcasted_iota(I32, l.shape, 1)
    vals, hots = [], []
    for _ in range(TOP_K):
        m = jnp.max(l, axis=1, keepdims=True)
        idx = jnp.min(jnp.where(l == m, lane, LANES), axis=1, keepdims=True)
        hot = lane == idx
        vals.append(m)
        hots.append(hot)
        l = jnp.where(hot, NEG_INF, l)
    es = [jnp.exp(v - vals[0]) for v in vals]
    den = es[0] + es[1] + es[2] + es[3]
    chosen = jnp.zeros(l.shape, F32)
    for hot in hots:
        chosen = chosen + jnp.where(hot, 1.0, 0.0)
    row = lax.broadcasted_iota(I32, (tm, tm), 0)
    col = lax.broadcasted_iota(I32, (tm, tm), 1)
    before = jnp.where(row > col, 1.0, 0.0).astype(BF16)
    pos = _dot(before, chosen.astype(BF16)) + run_scr[...]
    eidx = jnp.zeros(l.shape, I32)
    rank = jnp.zeros(l.shape, I32)
    gate = jnp.zeros(l.shape, F32)
    for k in range(TOP_K):
        e_k = jnp.min(jnp.where(hots[k], lane, LANES), axis=1, keepdims=True)
        r_k = jnp.sum(jnp.where(hots[k], pos, 0.0), axis=1, keepdims=True).astype(I32)
        eidx = jnp.where(lane == k, e_k, eidx)
        rank = jnp.where(lane == k, r_k, rank)
        gate = jnp.where(lane == k, es[k] / den, gate)
    eidx_ref[...] = eidx
    rank_ref[...] = rank
    gate_ref[...] = gate
    run_scr[...] = run_scr[...] + jnp.sum(chosen, axis=0, keepdims=True)
    cnt_ref[...] = jnp.broadcast_to(run_scr[...], cnt_ref.shape)


def _route(logits, tm):
    t = logits.shape[0]
    blk = pl.BlockSpec((tm, LANES), lambda i: (i, 0))
    return pl.pallas_call(
        _route_body,
        grid=(t // tm,),
        in_specs=[blk],
        out_specs=[blk, blk, blk, pl.BlockSpec((8, LANES), lambda i: (0, 0))],
        out_shape=[jax.ShapeDtypeStruct((t, LANES), I32),
                   jax.ShapeDtypeStruct((t, LANES), I32),
                   jax.ShapeDtypeStruct((t, LANES), F32),
                   jax.ShapeDtypeStruct((8, LANES), F32)],
        scratch_shapes=[pltpu.VMEM((1, LANES), F32)],
        compiler_params=_cparams(("arbitrary",)),
        name="route",
    )(logits)


def _dispatch_body(dest_ref, h_ref, zero_ref, xb_ref, sem, *, tm):
    del zero_ref
    i = pl.program_id(0)

    def copy(t, k):
        dst = dest_ref[t * TOP_K + k]
        return pltpu.make_async_copy(h_ref.at[pl.ds(t, 1), :], xb_ref.at[pl.ds(dst, 1), :], sem)

    def start(r, carry):
        for k in range(TOP_K):
            copy(i * tm + r, k).start()
        return carry

    def wait(r, carry):
        for k in range(TOP_K):
            copy(i * tm + r, k).wait()
        return carry

    lax.fori_loop(0, tm, start, 0)
    lax.fori_loop(0, tm, wait, 0)


def _dispatch(dest_flat, h_all, n_rows, tm):
    t, d = h_all.shape
    zeros = jnp.zeros((n_rows, d), h_all.dtype)
    grid_spec = pltpu.PrefetchScalarGridSpec(
        num_scalar_prefetch=1,
        grid=(t // tm,),
        in_specs=[pl.BlockSpec(memory_space=pl.ANY), pl.BlockSpec(memory_space=pl.ANY)],
        out_specs=pl.BlockSpec(memory_space=pl.ANY),
        scratch_shapes=[pltpu.SemaphoreType.DMA(())])
    return pl.pallas_call(
        functools.partial(_dispatch_body, tm=tm),
        grid_spec=grid_spec,
        out_shape=jax.ShapeDtypeStruct((n_rows, d), h_all.dtype),
        input_output_aliases={2: 0},
        compiler_params=_cparams(("arbitrary",)),
        name="dispatch",
    )(dest_flat, h_all, zeros)


def _moe_body(be_ref, nu_ref, x_ref, wgu_ref, bgu_ref, wd_ref, bd_ref, y_ref, wgu_scr, wd_scr):
    i = pl.program_id(0)
    prev = be_ref[jnp.maximum(i - 1, 0)]
    fresh = jnp.logical_or(i == 0, be_ref[i] != prev)

    @pl.when(jnp.logical_and(fresh, i < nu_ref[0]))
    def _():
        wgu_scr[...] = wgu_ref[0].astype(BF16)
        wd_scr[...] = wd_ref[0].astype(BF16)

    @pl.when(i < nu_ref[0])
    def _():
        x = x_ref[...].astype(BF16)
        gu = _dot(x, wgu_scr[...]) + bgu_ref[0]
        glu = jnp.minimum(gu[:, :D_FF], SWIGLU_LIMIT)
        lin = jnp.clip(gu[:, D_FF:], -SWIGLU_LIMIT, SWIGLU_LIMIT)
        act = glu * _sigmoid(SWIGLU_ALPHA * glu) * (lin + 1.0)
        y_ref[...] = _dot(act.astype(BF16), wd_scr[...]) + bd_ref[0]

    @pl.when(i >= nu_ref[0])
    def _():
        y_ref[...] = jnp.zeros(y_ref.shape, F32)


def _moe(block_expert, n_used, xb, wgu, bgu, wd, bd):
    n_rows, d = xb.shape
    nblk = n_rows // MOE_ROWS
    ne = wgu.shape[0]
    emap = lambda i, be, nu: (be[i], 0, 0)
    grid_spec = pltpu.PrefetchScalarGridSpec(
        num_scalar_prefetch=2,
        grid=(nblk,),
        in_specs=[pl.BlockSpec((MOE_ROWS, d), lambda i, be, nu: (i, 0)),
                  pl.BlockSpec((1, d, 2 * D_FF), emap),
                  pl.BlockSpec((1, 1, 2 * D_FF), emap),
                  pl.BlockSpec((1, D_FF, d), emap),
                  pl.BlockSpec((1, 1, d), emap)],
        out_specs=pl.BlockSpec((MOE_ROWS, d), lambda i, be, nu: (i, 0)),
        scratch_shapes=[pltpu.VMEM((d, 2 * D_FF), BF16), pltpu.VMEM((D_FF, d), BF16)])
    return pl.pallas_call(
        _moe_body,
        grid_spec=grid_spec,
        out_shape=jax.ShapeDtypeStruct((n_rows, d), F32),
        compiler_params=_cparams(("arbitrary",)),
        name="experts",
    )(block_expert, n_used, xb, wgu, bgu.reshape(ne, 1, 2 * D_FF), wd, bd.reshape(ne, 1, d))


def _combine_body(dest_ref, yb_ref, gate_ref, x1_ref, gt_ref, gf_ref, y_ref, buf, sem, *, tm, tok0):
    i = pl.program_id(0)

    def copy(r, k):
        src = dest_ref[(tok0 + i * tm + r) * TOP_K + k]
        return pltpu.make_async_copy(yb_ref.at[pl.ds(src, 1), :], buf.at[k, pl.ds(r, 1), :], sem)

    def start(r, carry):
        for k in range(TOP_K):
            copy(r, k).start()
        return carry

    def wait(r, carry):
        for k in range(TOP_K):
            copy(r, k).wait()
        return carry

    lax.fori_loop(0, tm, start, 0)
    lax.fori_loop(0, tm, wait, 0)
    g = gate_ref[...]
    y = g[:, 0:1] * buf[0]
    for k in range(1, TOP_K):
        y = y + g[:, k:k + 1] * buf[k]
    x2 = x1_ref[...] + gt_ref[0] * y
    ms = jnp.mean(x2 * x2, axis=-1, keepdims=True)
    y_ref[...] = x2 * lax.rsqrt(ms + EPS) * gf_ref[...]


def _combine(dest_flat, yb, gates, x1, gt, g_final, tm, rows_per_mod, tok0):
    r, d = x1.shape
    nt = r // tm
    tiles_per_mod = max(rows_per_mod // tm, 1)
    tile0 = tok0 // tm
    mod_block = (1,) + gt.shape[1:]
    grid_spec = pltpu.PrefetchScalarGridSpec(
        num_scalar_prefetch=1,
        grid=(nt,),
        in_specs=[pl.BlockSpec(memory_space=pl.ANY),
                  pl.BlockSpec((tm, LANES), lambda i, dr: (tile0 + i, 0)),
                  pl.BlockSpec((tm, d), lambda i, dr: (i, 0)),
                  pl.BlockSpec(mod_block, lambda i, dr: (i // tiles_per_mod, 0, 0)),
                  pl.BlockSpec((1, d), lambda i, dr: (0, 0))],
        out_specs=pl.BlockSpec((tm, d), lambda i, dr: (i, 0)),
        scratch_shapes=[pltpu.VMEM((TOP_K, tm, d), F32), pltpu.SemaphoreType.DMA(())])
    return pl.pallas_call(
        functools.partial(_combine_body, tm=tm, tok0=tok0),
        grid_spec=grid_spec,
        out_shape=jax.ShapeDtypeStruct((r, d), F32),
        compiler_params=_cparams(("arbitrary",)),
        name="combine",
    )(dest_flat, yb, gates, x1, gt, g_final)


def _rope_tables(pos):
    half = MOBA_HD // 2
    inv = 1.0 / (ROPE_THETA ** (jnp.arange(half, dtype=F32) / half))
    ang = pos.astype(F32)[:, None] * inv[None, :]
    cos = jnp.cos(ang)
    sin = jnp.sin(ang)
    cos_h = jnp.concatenate([cos, cos], axis=1)
    sin_h = jnp.concatenate([-sin, sin], axis=1)
    return jnp.tile(cos_h, (1, MOBA_HEADS)), jnp.tile(sin_h, (1, MOBA_HEADS))


def _row_tile(n, cap):
    t = min(n, cap)
    while n % t:
        t //= 2
    return t


def kernel(x_prompt, x_sample, cache_k, cache_v, state_gla, page_table, c_prompt, c_sample, w_ada, b_ada, g_norm1, w_in, w_gla_a2, b_gla_a2, g_gla_head, w_proj_gla, w_proj_moba, w_out, g_norm2, w_router, b_router, w_gate_up, b_gate_up, w_down, b_down, g_final):
    nb_p, seq, d = x_prompt.shape
    nb_s, dec = x_sample.shape[:2]
    past_len = page_table.shape[1] * cache_k.shape[2]
    assert w_ada.shape[0] == 1 and d == D_MODEL
    assert seq % MOBA_BLOCK == 0 and past_len % MOBA_BLOCK == 0 and dec <= MOBA_BLOCK and dec % 8 == 0
    r_p, r_s = nb_p * seq, nb_s * dec
    t_all = r_p + r_s

    c_all = jnp.concatenate([c_prompt, c_sample], axis=0)
    mod = _ada(c_all, w_ada[0], b_ada[0]).reshape(nb_p + nb_s, 6, d)
    mod_p = [mod[:nb_p, k].reshape(nb_p, 1, d) for k in range(6)]
    mod_s = [jnp.repeat(mod[nb_p:, k], dec, axis=0).reshape(1, r_s, d) for k in range(6)]

    w = w_in[0]
    gkw = GLA_HEADS * GLA_DK
    gvw = GLA_HEADS * GLA_DV
    c_ga = 2 * gkw + 2 * gvw
    pad = jnp.zeros((d, PROJ_TILE - GLA_LOWRANK), w.dtype)
    w_in_p = jnp.concatenate([w[:, :c_ga + GLA_LOWRANK], pad, w[:, c_ga + GLA_LOWRANK:]], axis=1).astype(BF16)
    assert w_in_p.shape[1] == N_PROJ_TILES * PROJ_TILE
    w2_p = jnp.zeros((PROJ_TILE, gkw), BF16).at[:GLA_LOWRANK].set(w_gla_a2[0].astype(BF16))
    b2 = b_gla_a2[0].reshape(1, gkw)
    g1 = g_norm1[0].reshape(1, d)
    g2 = g_norm2[0].reshape(1, d)
    wpg = w_proj_gla[0].astype(BF16)
    wpm = w_proj_moba[0].astype(BF16)
    wo = w_out[0].astype(BF16)
    wr = jnp.zeros((d, LANES), F32).at[:, :N_EXPERTS].set(w_router[0])
    br = jnp.full((1, LANES), NEG_INF, F32).at[0, :N_EXPERTS].set(b_router[0])
    g_head = g_gla_head[0].reshape(1, GLA_DV)

    cos_p, sin_p = _rope_tables(jnp.arange(seq, dtype=I32))
    cos_s, sin_s = _rope_tables(past_len + jnp.arange(dec, dtype=I32))
    cos_s, sin_s = jnp.tile(cos_s, (nb_s, 1)), jnp.tile(sin_s, (nb_s, 1))

    xp2 = x_prompt.reshape(r_p, d)
    xs2 = x_sample.reshape(r_s, d)
    tm_p = _row_tile(seq, 1024)

    proj_p = _inproj(xp2, mod_p[1], mod_p[0], g1, w_in_p, w2_p, b2, cos_p, sin_p, tm_p, seq)
    s0_p = jnp.zeros((nb_p, GLA_HEADS, GLA_DK, GLA_DV), state_gla.dtype)
    og_p, st_p = _gla(proj_p, s0_p, g_head, nb_p, seq, _row_tile(seq, 64))
    om_p = _moba_prompt(proj_p, nb_p, seq)
    tm_post = _row_tile(seq, 512)
    x1_p, h2_p, lg_p = _post(og_p, om_p, proj_p, xp2, mod_p[2], mod_p[4], mod_p[3], g2, wpg, wpm, wo, wr, br,
                             tm_post, seq)

    proj_s = _inproj(xs2, mod_s[1], mod_s[0], g1, w_in_p, w2_p, b2, cos_s, sin_s, r_s, r_s)
    og_s, st_s = _gla(proj_s, state_gla[0], g_head, nb_s, dec, dec)
    om_s = _moba_sample(proj_s, cache_k[0], cache_v[0], page_table, nb_s, dec)
    x1_s, h2_s, lg_s = _post(og_s, om_s, proj_s, xs2, mod_s[2], mod_s[4], mod_s[3], g2, wpg, wpm, wo, wr, br,
                             r_s, r_s)

    h_all = jnp.concatenate([h2_p, h2_s], axis=0)
    lg_all = jnp.concatenate([lg_p, lg_s], axis=0)
    tm_r = _row_tile(t_all, 256)
    eidx, rank, gates, cnt = _route(lg_all, tm_r)
    counts = cnt[0, :N_EXPERTS].astype(I32)
    nblk_e = (counts + MOE_ROWS - 1) // MOE_ROWS
    blk_end = jnp.cumsum(nblk_e)
    blk_start = blk_end - nblk_e
    dest = blk_start[eidx[:, :TOP_K]] * MOE_ROWS + rank[:, :TOP_K]
    dest_flat = dest.reshape(-1).astype(I32)
    n_blocks = -(-(t_all * TOP_K) // MOE_ROWS) + N_EXPERTS
    block_expert = jnp.minimum(jnp.searchsorted(blk_end, jnp.arange(n_blocks, dtype=I32), side="right"),
                               N_EXPERTS - 1).astype(I32)
    n_used = blk_end[-1:].astype(I32)
    xb = _dispatch(dest_flat, h_all, n_blocks * MOE_ROWS, tm_r)
    yb = _moe(block_expert, n_used, xb, w_gate_up[0], b_gate_up[0], w_down[0], b_down[0])
    gf = g_final.reshape(1, d)
    tm_c = _row_tile(seq, 256)
    y_p = _combine(dest_flat, yb, gates, x1_p, mod_p[5], gf, tm_c, seq, 0)
    y_s = _combine(dest_flat, yb, gates, x1_s, mod_s[5], gf, r_s, r_s, r_p)

    k_p = proj_p[T_MK].reshape(1, nb_p, seq, MOBA_HEADS, MOBA_HD)
    v_p = proj_p[T_MV].reshape(1, nb_p, seq, MOBA_HEADS, MOBA_HD)
    k_s = proj_s[T_MK].reshape(1, nb_s, dec, MOBA_HEADS, MOBA_HD)
    v_s = proj_s[T_MV].reshape(1, nb_s, dec, MOBA_HEADS, MOBA_HD)
    return (y_p.reshape(nb_p, seq, d), y_s.reshape(nb_s, dec, d), k_p, v_p, st_p[None],
            k_s, v_s, st_s[None])
```

```python
import functools

import jax
import jax.numpy as jnp
from jax import lax
from jax.experimental import pallas as pl
from jax.experimental.pallas import tpu as pltpu

F32 = jnp.float32
BF16 = jnp.bfloat16
I32 = jnp.int32

D_MODEL = 1024
GLA_HEADS = 4
GLA_DK = 128
GLA_DV = 256
GLA_LOWRANK = 16
GLA_GATE_NORM = 16.0
GLA_SUB = 16
MOBA_HEADS = 8
MOBA_HD = 64
MOBA_BLOCK = 256
MOBA_TOPK = 3
MOBA_W = MOBA_HEADS * MOBA_HD
ROPE_THETA = 10000.0
N_EXPERTS = 32
TOP_K = 4
D_FF = D_MODEL
SWIGLU_LIMIT = 7.0
SWIGLU_ALPHA = 1.702
EPS = 1e-6
LANES = 128
PROJ_TILE = 512

T_GQ, T_GK, T_GV, T_GR, T_GA, T_MQ, T_MK, T_MV, T_BA, T_BB = 0, 1, 2, 4, 6, 7, 8, 9, 10, 12
N_PROJ_TILES = 14
MOE_ROWS = 256
MOBA_S_BLOCKS_PER_STEP = 4
VMEM_LIMIT = 56 * 1024 * 1024

NEG_INF = float("-inf")


def _sigmoid(x):
    return 1.0 / (1.0 + jnp.exp(-x))


def _split3(x):
    hi = x.astype(BF16)
    r = x - hi.astype(F32)
    mid = r.astype(BF16)
    lo = (r - mid.astype(F32)).astype(BF16)
    return hi, mid, lo


def _dot(a, b):
    return jnp.dot(a, b, preferred_element_type=F32)


def _dot_nt(a, b):
    return lax.dot_general(a, b, (((1,), (1,)), ((), ())), preferred_element_type=F32)


def _dot_nt_f32(a, b):
    ah = a.astype(BF16)
    al = (a - ah.astype(F32)).astype(BF16)
    bh = b.astype(BF16)
    bl = (b - bh.astype(F32)).astype(BF16)
    return _dot_nt(ah, bh) + _dot_nt(ah, bl) + _dot_nt(al, bh)


def _div(x, n):
    assert n & (n - 1) == 0
    return lax.shift_right_logical(x, n.bit_length() - 1)


def _mod(x, n):
    assert n & (n - 1) == 0
    return x & (n - 1)


def _cparams(sem):
    return pltpu.CompilerParams(dimension_semantics=sem, vmem_limit_bytes=VMEM_LIMIT)


def _ada_body(c_ref, w_ref, b_ref, o_ref):
    c = c_ref[...]
    s = c * _sigmoid(c)
    o_ref[...] = _dot(s.astype(BF16), w_ref[...].astype(BF16)) + b_ref[...]


def _ada(c, w, b):
    n = w.shape[1]
    tn = n // 4
    return pl.pallas_call(
        _ada_body,
        grid=(4,),
        in_specs=[pl.BlockSpec(c.shape, lambda j: (0, 0)),
                  pl.BlockSpec((w.shape[0], tn), lambda j: (0, j)),
                  pl.BlockSpec((1, tn), lambda j: (0, j))],
        out_specs=pl.BlockSpec((c.shape[0], tn), lambda j: (0, j)),
        out_shape=jax.ShapeDtypeStruct((c.shape[0], n), F32),
        compiler_params=_cparams(("arbitrary",)),
        name="ada",
    )(c, w, b.reshape(1, n))


def _rotary(x, cos, sin_signed):
    lane = lax.broadcasted_iota(I32, x.shape, 1)
    half = MOBA_HD // 2
    partner = jnp.where((lane & (MOBA_HD - 1)) < half, lane + half, lane - half)
    w = x.shape[1]
    r1 = pltpu.roll(x, half, 1)
    i1 = pltpu.roll(lane, half, 1)
    r2 = pltpu.roll(x, w - half, 1)
    swapped = jnp.where(i1 == partner, r1, r2)
    return x * cos + swapped * sin_signed


def _log_sigmoid(x):
    return jnp.minimum(x, 0.0) - jnp.log(1.0 + jnp.exp(-jnp.abs(x)))


def _store_heads(ref, val):
    tm = val.shape[0]
    for h in range(MOBA_HEADS):
        ref[pl.ds(h, tm, stride=MOBA_HEADS), :] = val[:, h * MOBA_HD:(h + 1) * MOBA_HD]


def _inproj_body(x_ref, sc_ref, sh_ref, g_ref, w_ref, w2_ref, b2_ref, cos_ref, sin_ref, o_ref, k5_ref, v5_ref,
                 h_scr):
    j = pl.program_id(1)

    @pl.when(j == 0)
    def _():
        x = x_ref[...]
        ms = jnp.mean(x * x, axis=-1, keepdims=True)
        y = x * lax.rsqrt(ms + EPS) * g_ref[...]
        h = y * (1.0 + sc_ref[0]) + sh_ref[0]
        h_scr[...] = h.astype(BF16)

    acc = _dot(h_scr[...], w_ref[...])
    is_la = j == T_GA
    special = functools.reduce(jnp.logical_or, [j == T_MQ, j == T_MK, j == T_MV, is_la])

    @pl.when(j == T_MQ)
    def _():
        o_ref[0] = _rotary(acc, cos_ref[...], sin_ref[...])

    @pl.when(j == T_MK)
    def _():
        rot = _rotary(acc, cos_ref[...], sin_ref[...])
        o_ref[0] = rot
        _store_heads(k5_ref, rot)

    @pl.when(j == T_MV)
    def _():
        o_ref[0] = acc
        _store_heads(v5_ref, acc)

    @pl.when(is_la)
    def _():
        z = _dot(acc.astype(BF16), w2_ref[...]) + b2_ref[...]
        o_ref[0] = _log_sigmoid(z) * (1.0 / GLA_GATE_NORM)

    @pl.when(jnp.logical_not(special))
    def _():
        o_ref[0] = acc


def _inproj(x2, sc, sh, g1, w_in_p, w2_p, b2, cos, sin, tm, rows_per_mod):
    r, d = x2.shape
    nt = r // tm
    tiles_per_mod = max(rows_per_mod // tm, 1)
    tiles_per_tab = cos.shape[0] // tm
    mod_block = (1,) + sc.shape[1:]
    return pl.pallas_call(
        _inproj_body,
        grid=(nt, N_PROJ_TILES),
        in_specs=[pl.BlockSpec((tm, d), lambda i, j: (i, 0)),
                  pl.BlockSpec(mod_block, lambda i, j: (i // tiles_per_mod, 0, 0)),
                  pl.BlockSpec(mod_block, lambda i, j: (i // tiles_per_mod, 0, 0)),
                  pl.BlockSpec((1, d), lambda i, j: (0, 0)),
                  pl.BlockSpec((d, PROJ_TILE), lambda i, j: (0, j)),
                  pl.BlockSpec((PROJ_TILE, PROJ_TILE), lambda i, j: (0, 0)),
                  pl.BlockSpec((1, PROJ_TILE), lambda i, j: (0, 0)),
                  pl.BlockSpec((tm, PROJ_TILE), lambda i, j: (i % tiles_per_tab, 0)),
                  pl.BlockSpec((tm, PROJ_TILE), lambda i, j: (i % tiles_per_tab, 0))],
        out_specs=[pl.BlockSpec((1, tm, PROJ_TILE), lambda i, j: (j, i, 0)),
                   pl.BlockSpec((tm * MOBA_HEADS, MOBA_HD), lambda i, j: (i, 0)),
                   pl.BlockSpec((tm * MOBA_HEADS, MOBA_HD), lambda i, j: (i, 0))],
        out_shape=[jax.ShapeDtypeStruct((N_PROJ_TILES, r, PROJ_TILE), F32),
                   jax.ShapeDtypeStruct((r * MOBA_HEADS, MOBA_HD), F32),
                   jax.ShapeDtypeStruct((r * MOBA_HEADS, MOBA_HD), F32)],
        scratch_shapes=[pltpu.VMEM((tm, d), BF16)],
        compiler_params=_cparams(("arbitrary", "arbitrary")),
        name="inproj",
    )(x2, sc, sh, g1, w_in_p, w2_p, b2, cos, sin)


def _gla_body(q_ref, k_ref, v_ref, la_ref, gr_ref, s0_ref, gh_ref, og_ref, sout_ref, st_scr, *, C, SB):
    c = pl.program_id(1)
    nc = pl.num_programs(1)

    @pl.when(c == 0)
    def _():
        for h in range(GLA_HEADS):
            st_scr[h] = s0_ref[0, h].T

    la = la_ref[0]
    row_c = lax.broadcasted_iota(I32, (C, C), 0)
    col_c = lax.broadcasted_iota(I32, (C, C), 1)
    tri = jnp.where(row_c >= col_c, 1.0, 0.0).astype(BF16)
    hi, mid, lo = _split3(la)
    b_all = _dot(tri, hi) + _dot(tri, mid) + _dot(tri, lo)
    q_all = q_ref[0] * (GLA_DK ** -0.5)
    k_all = k_ref[0]
    gh = gh_ref[...]
    row_k = lax.broadcasted_iota(I32, (C, GLA_DK), 0)
    row_sb = lax.broadcasted_iota(I32, (SB, 1), 0)
    lane_sb = lax.broadcasted_iota(I32, (SB, C), 1)

    for h in range(GLA_HEADS):
        ks = slice(h * GLA_DK, (h + 1) * GLA_DK)
        vs = slice((h % 2) * GLA_DV, (h % 2 + 1) * GLA_DV)
        b = b_all[:, ks]
        q = q_all[:, ks]
        k = k_all[:, ks]
        v = v_ref[h // 2][:, vs]
        st = st_scr[h]
        b_last = b[C - 1:C, :]
        o = _dot_nt((q * jnp.exp(b)).astype(BF16), st.astype(BF16))
        att_rows = []
        for i in range(C // SB):
            r0 = i * SB
            bi = b[r0:r0 + SB]
            qi = q[r0:r0 + SB]
            ki = k[r0:r0 + SB]
            if i > 0:
                bref = b[r0 - 1:r0, :]
                qt = qi * jnp.exp(bi - bref)
                kt = jnp.where(row_k < r0, k * jnp.exp(jnp.minimum(bref - b, 0.0)), 0.0)
                att_i = _dot_nt(qt.astype(BF16), kt.astype(BF16))
            else:
                att_i = jnp.zeros((SB, C), F32)
            for s in range(SB):
                dec = jnp.exp(jnp.minimum(bi - bi[s:s + 1, :], 0.0))
                col = jnp.sum(qi * ki[s:s + 1, :] * dec, axis=1, keepdims=True)
                col = jnp.where(row_sb >= s, col, 0.0)
                att_i = att_i + jnp.where(lane_sb == r0 + s, col, 0.0)
            att_rows.append(att_i)
        att = att_rows[0] if len(att_rows) == 1 else jnp.concatenate(att_rows, axis=0)
        vb = v.astype(BF16)
        o = o + _dot(att.astype(BF16), vb)
        kd = k * jnp.exp(b_last - b)
        st_scr[h] = st * jnp.exp(b_last) + _dot(v.T.astype(BF16), kd.astype(BF16))
        on = o * lax.rsqrt(jnp.mean(o * o, axis=-1, keepdims=True) + EPS) * gh
        gr = gr_ref[h // 2][:, vs]
        og_ref[:, h * GLA_DV:(h + 1) * GLA_DV] = (on * (gr * _sigmoid(gr))).astype(og_ref.dtype)

    @pl.when(c == nc - 1)
    def _():
        for h in range(GLA_HEADS):
            sout_ref[0, h] = st_scr[h].T


def _gla(proj3, s0, g_head, n_seq, seq_len, chunk):
    r = proj3.shape[1]
    nc = seq_len // chunk
    sb = min(GLA_SUB, chunk)
    out_dtype = BF16 if chunk % 16 == 0 else F32
    row = lambda b, c: b * nc + c
    body = functools.partial(_gla_body, C=chunk, SB=sb)
    return pl.pallas_call(
        body,
        grid=(n_seq, nc),
        in_specs=[pl.BlockSpec((1, chunk, PROJ_TILE), lambda b, c: (T_GQ, row(b, c), 0)),
                  pl.BlockSpec((1, chunk, PROJ_TILE), lambda b, c: (T_GK, row(b, c), 0)),
                  pl.BlockSpec((2, chunk, PROJ_TILE), lambda b, c: (T_GV // 2, row(b, c), 0)),
                  pl.BlockSpec((1, chunk, PROJ_TILE), lambda b, c: (T_GA, row(b, c), 0)),
                  pl.BlockSpec((2, chunk, PROJ_TILE), lambda b, c: (T_GR // 2, row(b, c), 0)),
                  pl.BlockSpec((1, GLA_HEADS, GLA_DK, GLA_DV), lambda b, c: (b, 0, 0, 0)),
                  pl.BlockSpec((1, GLA_DV), lambda b, c: (0, 0))],
        out_specs=[pl.BlockSpec((chunk, GLA_HEADS * GLA_DV), lambda b, c: (row(b, c), 0)),
                   pl.BlockSpec((1, GLA_HEADS, GLA_DK, GLA_DV), lambda b, c: (b, 0, 0, 0))],
        out_shape=[jax.ShapeDtypeStruct((r, GLA_HEADS * GLA_DV), out_dtype),
                   jax.ShapeDtypeStruct((n_seq, GLA_HEADS, GLA_DK, GLA_DV), F32)],
        scratch_shapes=[pltpu.VMEM((GLA_HEADS, GLA_DV, GLA_DK), F32)],
        compiler_params=_cparams(("arbitrary", "arbitrary")),
        name="gla",
    )(proj3, proj3, proj3, proj3, proj3, s0, g_head)


def _select_topk(g, n_valid_rows, n_rows, n_sel):
    row = lax.broadcasted_iota(I32, g.shape, 0)
    valid = row < n_valid_rows
    gm = jnp.where(valid, g, NEG_INF)
    rank = jnp.zeros(g.shape, F32)
    for jp in range(n_rows):
        gj = gm[jp:jp + 1, :]
        ahead = jnp.logical_or(gj > gm, jnp.logical_and(gj == gm, jp < row))
        rank = rank + jnp.where(ahead, 1.0, 0.0)
    return jnp.where(jnp.logical_and(valid, rank < n_sel), 1.0, 0.0)


def _moba_p_body(q_ref, k_ref, v_ref, o_ref, k_scr, vt_scr, km_scr, sel_scr, m_scr, l_scr, acc_scr, *, nb, nbp):
    i = pl.program_id(1)
    blk = MOBA_BLOCK
    scale = MOBA_HD ** -0.5

    @pl.when(i == 0)
    def _():
        km_scr[...] = jnp.zeros(km_scr.shape, F32)
        for j in range(nb):
            kj = k_ref[0, j * blk:(j + 1) * blk, :]
            k_scr[j] = kj.astype(BF16)
            km_scr[j:j + 1, :] = jnp.mean(kj, axis=0, keepdims=True)
            vt_scr[j] = v_ref[0, j * blk:(j + 1) * blk, :].T.astype(BF16)

    q = q_ref[0]
    km = km_scr[...]
    km_rep = jnp.concatenate([km] * MOBA_HEADS, axis=0)
    wrow = lax.broadcasted_iota(I32, km_rep.shape, 0)
    wlane = lax.broadcasted_iota(I32, km_rep.shape, 1)
    wt = jnp.where(_div(wlane, MOBA_HD) == _div(wrow, nbp), km_rep, 0.0)
    gates_t = _dot_nt_f32(wt, q)
    for h in range(MOBA_HEADS):
        sel_scr[h] = _select_topk(gates_t[h * nbp:(h + 1) * nbp, :], i, nb, MOBA_TOPK)

    n_pair = MOBA_HEADS // 2
    pair_lane = lax.broadcasted_iota(I32, (blk, LANES), 1)
    key_row = lax.broadcasted_iota(I32, (blk, 2 * blk), 0)
    q_col = _mod(lax.broadcasted_iota(I32, (blk, 2 * blk), 1), blk)
    qms = []
    for pr in range(n_pair):
        qp = q[:, pr * LANES:(pr + 1) * LANES]
        qms.append(jnp.concatenate([jnp.where(_div(pair_lane, MOBA_HD) == u, qp, 0.0) for u in range(2)],
                                   axis=0).astype(BF16))

    def scores(j, pr):
        kj = k_scr[j, :, pr * LANES:(pr + 1) * LANES]
        return _dot_nt(kj, qms[pr]) * scale

    def values(j, h):
        return vt_scr[j, h * MOBA_HD:(h + 1) * MOBA_HD, :]

    for pr in range(n_pair):
        s = jnp.where(key_row <= q_col, scores(i, pr), NEG_INF)
        m = jnp.max(s, axis=0, keepdims=True)
        p = jnp.exp(s - m)
        m_scr[pr] = m
        l_scr[pr] = jnp.sum(p, axis=0, keepdims=True)
        pb = p.astype(BF16)
        for u in range(2):
            acc_scr[2 * pr + u] = _dot(values(i, 2 * pr + u), pb[:, u * blk:(u + 1) * blk])

    def step(j, carry):
        for pr in range(n_pair):
            keep = jnp.concatenate([sel_scr[2 * pr + u, pl.ds(j, 1), :] for u in range(2)], axis=1)
            s = jnp.where(keep > 0.5, scores(j, pr), NEG_INF)
            m_old = m_scr[pr]
            m_new = jnp.maximum(m_old, jnp.max(s, axis=0, keepdims=True))
            alpha = jnp.exp(m_old - m_new)
            p = jnp.exp(s - m_new)
            m_scr[pr] = m_new
            l_scr[pr] = l_scr[pr] * alpha + jnp.sum(p, axis=0, keepdims=True)
            pb = p.astype(BF16)
            for u in range(2):
                h = 2 * pr + u
                cs = slice(u * blk, (u + 1) * blk)
                acc_scr[h] = acc_scr[h] * alpha[:, cs] + _dot(values(j, h), pb[:, cs])
        return carry

    lax.fori_loop(0, i, step, 0)
    outs = []
    for h in range(MOBA_HEADS):
        l = l_scr[h // 2][:, (h % 2) * blk:(h % 2 + 1) * blk]
        outs.append(acc_scr[h] / l)
    out_t = jnp.concatenate(outs, axis=0)
    o_ref[...] = out_t.T.astype(o_ref.dtype)


def _moba_prompt(proj3, n_seq, seq_len):
    r = proj3.shape[1]
    nb = seq_len // MOBA_BLOCK
    nbp = -(-nb // 8) * 8
    body = functools.partial(_moba_p_body, nb=nb, nbp=nbp)
    return pl.pallas_call(
        body,
        grid=(n_seq, nb),
        in_specs=[pl.BlockSpec((1, MOBA_BLOCK, MOBA_W), lambda b, i: (T_MQ, b * nb + i, 0)),
                  pl.BlockSpec((1, seq_len, MOBA_W), lambda b, i: (T_MK, b, 0)),
                  pl.BlockSpec((1, seq_len, MOBA_W), lambda b, i: (T_MV, b, 0))],
        out_specs=pl.BlockSpec((MOBA_BLOCK, MOBA_W), lambda b, i: (b * nb + i, 0)),
        out_shape=jax.ShapeDtypeStruct((r, MOBA_W), BF16),
        scratch_shapes=[pltpu.VMEM((nb, MOBA_BLOCK, MOBA_W), BF16),
                        pltpu.VMEM((nb, MOBA_W, MOBA_BLOCK), BF16),
                        pltpu.VMEM((nbp, MOBA_W), F32),
                        pltpu.VMEM((MOBA_HEADS, nbp, MOBA_BLOCK), F32),
                        pltpu.VMEM((MOBA_HEADS // 2, 1, 2 * MOBA_BLOCK), F32),
                        pltpu.VMEM((MOBA_HEADS // 2, 1, 2 * MOBA_BLOCK), F32),
                        pltpu.VMEM((MOBA_HEADS, MOBA_HD, MOBA_BLOCK), F32)],
        compiler_params=_cparams(("arbitrary", "arbitrary")),
        name="moba_prompt",
    )(proj3, proj3, proj3)


def _moba_s_body(pt_ref, *refs, nbp, L, bps):
    del pt_ref
    page_refs = refs[:4 * bps]
    qn_ref, kn_ref, vn_ref, o_ref, q2_scr, gate_scr, m_scr, l_scr, acc_scr = refs[4 * bps:]
    j = pl.program_id(1)
    n_steps = nbp // bps
    blk = MOBA_BLOCK
    scale = MOBA_HD ** -0.5
    n_col = LANES

    @pl.when(j == 0)
    def _():
        qn = qn_ref[...]
        q2 = jnp.concatenate([qn] * (n_col // L), axis=0)
        row = lax.broadcasted_iota(I32, q2.shape, 0)
        lane = lax.broadcasted_iota(I32, q2.shape, 1)
        q2_scr[...] = jnp.where(_div(lane, MOBA_HD) == _div(row, L), q2, 0.0)
        gate_scr[...] = jnp.full(gate_scr.shape, NEG_INF, F32)
        m_scr[...] = jnp.zeros(m_scr.shape, F32)
        l_scr[...] = jnp.zeros(l_scr.shape, F32)

    def block_partial(kblk, vblk, causal):
        q2 = q2_scr[...]
        s = _dot_nt(kblk.astype(BF16), q2.astype(BF16)) * scale
        if causal:
            key_row = lax.broadcasted_iota(I32, s.shape, 0)
            col = lax.broadcasted_iota(I32, s.shape, 1)
            s = jnp.where(key_row <= _mod(col, L), s, NEG_INF)
        m = jnp.max(s, axis=0, keepdims=True)
        p = jnp.exp(s - m)
        l = jnp.sum(p, axis=0, keepdims=True)
        acc = _dot(vblk.T.astype(BF16), p.astype(BF16))
        return m, l, acc

    @pl.when(j < n_steps)
    def _():
        for t in range(bps):
            ka_ref, kb_ref, va_ref, vb_ref = page_refs[4 * t:4 * t + 4]
            jj = j * bps + t
            kblk = jnp.concatenate([ka_ref[0], kb_ref[0]], axis=0)
            vblk = jnp.concatenate([va_ref[0], vb_ref[0]], axis=0)
            km = jnp.mean(kblk, axis=0, keepdims=True)
            km8 = jnp.concatenate([km] * 8, axis=0)
            gate = _dot_nt_f32(km8, q2_scr[...])
            gate_scr[pl.ds(jj, 1), :] = gate[0:1, :]
            m, l, acc = block_partial(kblk, vblk, False)
            m_scr[pl.ds(jj, 1), :] = m
            l_scr[pl.ds(jj, 1), :] = l
            acc_scr[jj] = acc

    @pl.when(j == n_steps)
    def _():
        pad = jnp.zeros((blk - L, MOBA_W), F32)
        m_own, l_own, acc_own = block_partial(jnp.concatenate([kn_ref[...], pad], axis=0),
                                              jnp.concatenate([vn_ref[...], pad], axis=0), True)
        n_sel = min(MOBA_TOPK, nbp + 1)
        sel = _select_topk(gate_scr[...], nbp, nbp, n_sel)
        m_all = m_scr[...]
        m_top = jnp.maximum(m_own, jnp.max(jnp.where(sel > 0.5, m_all, NEG_INF), axis=0, keepdims=True))
        w = jnp.where(sel > 0.5, jnp.exp(m_all - m_top), 0.0)
        w_own = jnp.exp(m_own - m_top)
        den = jnp.sum(w * l_scr[...], axis=0, keepdims=True) + w_own * l_own
        num = w_own * acc_own
        for jj in range(nbp):
            num = num + w[jj:jj + 1, :] * acc_scr[jj]
        out_t = num / den
        row = lax.broadcasted_iota(I32, out_t.shape, 0)
        col = lax.broadcasted_iota(I32, out_t.shape, 1)
        out_m = jnp.where(_div(row, MOBA_HD) == _div(col, L), out_t, 0.0)
        prow = lax.broadcasted_iota(I32, (L, n_col), 0)
        pcol = lax.broadcasted_iota(I32, (L, n_col), 1)
        pick = jnp.where(jnp.logical_and(_mod(pcol, L) == prow, pcol < MOBA_HEADS * L), 1.0, 0.0).astype(BF16)
        hi, mid, lo = _split3(out_m)
        o_ref[...] = _dot_nt(pick, hi) + _dot_nt(pick, mid) + _dot_nt(pick, lo)


def _moba_sample(proj3, cache_k, cache_v, page_table, n_seq, L):
    n_pool, page = cache_k.shape[0], cache_k.shape[1]
    n_pages = page_table.shape[1]
    ppb = MOBA_BLOCK // page
    nbp = n_pages // ppb
    nbp_pad = -(-nbp // 8) * 8
    assert ppb == 2 and n_pages % ppb == 0 and LANES % L == 0 and MOBA_HEADS * L <= LANES
    ck = cache_k.reshape(n_pool, page, MOBA_W)
    cv = cache_v.reshape(n_pool, page, MOBA_W)
    pt = page_table.reshape(-1).astype(I32)

    bps = MOBA_S_BLOCKS_PER_STEP if nbp % MOBA_S_BLOCKS_PER_STEP == 0 else 1
    n_steps = nbp // bps

    def page_map(t, off):
        return lambda b, j, pt_ref: (
            pt_ref[b * n_pages + ppb * (jnp.minimum(j, n_steps - 1) * bps + t) + off], 0, 0)

    new_map = lambda t: (lambda b, j, pt_ref: (t, b, 0))
    body = functools.partial(_moba_s_body, nbp=nbp, L=L, bps=bps)
    page_specs = [pl.BlockSpec((1, page, MOBA_W), page_map(t, off)) for t in range(bps) for off in (0, 1, 0, 1)]
    page_args = [a for _ in range(bps) for a in (ck, ck, cv, cv)]
    grid_spec = pltpu.PrefetchScalarGridSpec(
        num_scalar_prefetch=1,
        grid=(n_seq, n_steps + 1),
        in_specs=page_specs + [
                  pl.BlockSpec((None, L, MOBA_W), new_map(T_MQ)),
                  pl.BlockSpec((None, L, MOBA_W), new_map(T_MK)),
                  pl.BlockSpec((None, L, MOBA_W), new_map(T_MV))],
        out_specs=pl.BlockSpec((L, MOBA_W), lambda b, j, pt_ref: (b, 0)),
        scratch_shapes=[pltpu.VMEM((LANES, MOBA_W), F32),
                        pltpu.VMEM((nbp_pad, LANES), F32),
                        pltpu.VMEM((nbp_pad, LANES), F32),
                        pltpu.VMEM((nbp_pad, LANES), F32),
                        pltpu.VMEM((nbp, MOBA_W, LANES), F32)])
    return pl.pallas_call(
        body,
        grid_spec=grid_spec,
        out_shape=jax.ShapeDtypeStruct((n_seq * L, MOBA_W), F32),
        compiler_params=_cparams(("arbitrary", "arbitrary")),
        name="moba_sample",
    )(pt, *page_args, proj3, proj3, proj3)


def _post_body(og_ref, om_ref, ba_ref, bb_ref, x_ref, gt_ref, sc_ref, sh_ref, g2_ref, wpg_ref, wpm_ref,
               wo_ref, wr_ref, br_ref, x1_ref, h2_ref, lg_ref):
    ya = _dot(og_ref[...].astype(BF16), wpg_ref[...])
    yb = _dot(om_ref[...].astype(BF16), wpm_ref[...])
    ba = jnp.concatenate([ba_ref[0], ba_ref[1]], axis=1)
    bb = jnp.concatenate([bb_ref[0], bb_ref[1]], axis=1)
    mix = _sigmoid(ba) * ya + _sigmoid(bb) * yb
    x1 = x_ref[...] + gt_ref[0] * _dot(mix.astype(BF16), wo_ref[...])
    x1_ref[...] = x1
    ms = jnp.mean(x1 * x1, axis=-1, keepdims=True)
    h2 = x1 * lax.rsqrt(ms + EPS) * g2_ref[...] * (1.0 + sc_ref[0]) + sh_ref[0]
    h2_ref[...] = h2
    hh = h2.astype(BF16)
    hl = (h2 - hh.astype(F32)).astype(BF16)
    wr = wr_ref[...]
    wh = wr.astype(BF16)
    wl = (wr - wh.astype(F32)).astype(BF16)
    lg_ref[...] = _dot(hh, wh) + _dot(hh, wl) + _dot(hl, wh) + br_ref[...]


def _post(og, om, proj3, x2, gt, sc, sh, g2, wpg, wpm, wo, wr, br, tm, rows_per_mod):
    r, d = x2.shape
    nt = r // tm
    tiles_per_mod = max(rows_per_mod // tm, 1)
    mod_block = (1,) + gt.shape[1:]
    mod_map = lambda i: (i // tiles_per_mod, 0, 0)
    full = lambda a: pl.BlockSpec(a.shape, lambda i: (0,) * a.ndim)
    return pl.pallas_call(
        _post_body,
        grid=(nt,),
        in_specs=[pl.BlockSpec((tm, og.shape[1]), lambda i: (i, 0)),
                  pl.BlockSpec((tm, om.shape[1]), lambda i: (i, 0)),
                  pl.BlockSpec((2, tm, PROJ_TILE), lambda i: (T_BA // 2, i, 0)),
                  pl.BlockSpec((2, tm, PROJ_TILE), lambda i: (T_BB // 2, i, 0)),
                  pl.BlockSpec((tm, d), lambda i: (i, 0)),
                  pl.BlockSpec(mod_block, mod_map),
                  pl.BlockSpec(mod_block, mod_map),
                  pl.BlockSpec(mod_block, mod_map),
                  full(g2), full(wpg), full(wpm), full(wo), full(wr), full(br)],
        out_specs=[pl.BlockSpec((tm, d), lambda i: (i, 0)),
                   pl.BlockSpec((tm, d), lambda i: (i, 0)),
                   pl.BlockSpec((tm, LANES), lambda i: (i, 0))],
        out_shape=[jax.ShapeDtypeStruct((r, d), F32),
                   jax.ShapeDtypeStruct((r, d), F32),
                   jax.ShapeDtypeStruct((r, LANES), F32)],
        compiler_params=_cparams(("arbitrary",)),
        name="post",
    )(og, om, proj3, proj3, x2, gt, sc, sh, g2, wpg, wpm, wo, wr, br)


def _route_body(lg_ref, eidx_ref, rank_ref, gate_ref, cnt_ref, run_scr):
    i = pl.program_id(0)
    tm = lg_ref.shape[0]

    @pl.when(i == 0)
    def _():
        run_scr[...] = jnp.zeros(run_scr.shape, F32)

    l = lg_ref[...]
    lane = lax.broadcasted_iota(I32, l.shape, 1)
    vals, hots = [], []
    for _ in range(TOP_K):
        m = jnp.max(l, axis=1, keepdims=True)
        idx = jnp.min(jnp.where(l == m, lane, LANES), axis=1, keepdims=True)
        hot = lane == idx
        vals.append(m)
        hots.append(hot)
        l = jnp.where(hot, NEG_INF, l)
    es = [jnp.exp(v - vals[0]) for v in vals]
    den = es[0] + es[1] + es[2] + es[3]
    chosen = jnp.zeros(l.shape, F32)
    for hot in hots:
        chosen = chosen + jnp.where(hot, 1.0, 0.0)
    row = lax.broadcasted_iota(I32, (tm, tm), 0)
    col = lax.broadcasted_iota(I32, (tm, tm), 1)
    before = jnp.where(row > col, 1.0, 0.0).astype(BF16)
    pos = _dot(before, chosen.astype(BF16)) + run_scr[...]
    eidx = jnp.zeros(l.shape, I32)
    rank = jnp.zeros(l.shape, I32)
    gate = jnp.zeros(l.shape, F32)
    for k in range(TOP_K):
        e_k = jnp.min(jnp.where(hots[k], lane, LANES), axis=1, keepdims=True)
        r_k = jnp.sum(jnp.where(hots[k], pos, 0.0), axis=1, keepdims=True).astype(I32)
        eidx = jnp.where(lane == k, e_k, eidx)
        rank = jnp.where(lane == k, r_k, rank)
        gate = jnp.where(lane == k, es[k] / den, gate)
    eidx_ref[...] = eidx
    rank_ref[...] = rank
    gate_ref[...] = gate
    run_scr[...] = run_scr[...] + jnp.sum(chosen, axis=0, keepdims=True)
    cnt_ref[...] = jnp.broadcast_to(run_scr[...], cnt_ref.shape)


def _route(logits, tm):
    t = logits.shape[0]
    blk = pl.BlockSpec((tm, LANES), lambda i: (i, 0))
    return pl.pallas_call(
        _route_body,
        grid=(t // tm,),
        in_specs=[blk],
        out_specs=[blk, blk, blk, pl.BlockSpec((8, LANES), lambda i: (0, 0))],
        out_shape=[jax.ShapeDtypeStruct((t, LANES), I32),
                   jax.ShapeDtypeStruct((t, LANES), I32),
                   jax.ShapeDtypeStruct((t, LANES), F32),
                   jax.ShapeDtypeStruct((8, LANES), F32)],
        scratch_shapes=[pltpu.VMEM((1, LANES), F32)],
        compiler_params=_cparams(("arbitrary",)),
        name="route",
    )(logits)


def _dispatch_body(dest_ref, h_ref, zero_ref, xb_ref, sem, *, tm):
    del zero_ref
    i = pl.program_id(0)

    def copy(r, k):
        dst = dest_ref[(i * tm + r) * TOP_K + k]
        return pltpu.make_async_copy(h_ref.at[pl.ds(r, 1), :], xb_ref.at[pl.ds(dst, 1), :], sem)

    def start(r, carry):
        for k in range(TOP_K):
            copy(r, k).start()
        return carry

    def wait(r, carry):
        for k in range(TOP_K):
            copy(r, k).wait()
        return carry

    lax.fori_loop(0, tm, start, 0)
    lax.fori_loop(0, tm, wait, 0)


def _dispatch(dest_flat, h_all, n_rows, tm):
    t, d = h_all.shape
    zeros = jnp.zeros((n_rows, d), h_all.dtype)
    grid_spec = pltpu.PrefetchScalarGridSpec(
        num_scalar_prefetch=1,
        grid=(t // tm,),
        in_specs=[pl.BlockSpec((tm, d), lambda i, dr: (i, 0)), pl.BlockSpec(memory_space=pl.ANY)],
        out_specs=pl.BlockSpec(memory_space=pl.ANY),
        scratch_shapes=[pltpu.SemaphoreType.DMA(())])
    return pl.pallas_call(
        functools.partial(_dispatch_body, tm=tm),
        grid_spec=grid_spec,
        out_shape=jax.ShapeDtypeStruct((n_rows, d), h_all.dtype),
        input_output_aliases={2: 0},
        compiler_params=_cparams(("arbitrary",)),
        name="dispatch",
    )(dest_flat, h_all, zeros)


def _moe_body(be_ref, nu_ref, x_ref, wgu_ref, bgu_ref, wd_ref, bd_ref, y_ref, wgu_scr, wd_scr):
    i = pl.program_id(0)
    prev = be_ref[jnp.maximum(i - 1, 0)]
    fresh = jnp.logical_or(i == 0, be_ref[i] != prev)

    @pl.when(jnp.logical_and(fresh, i < nu_ref[0]))
    def _():
        wgu_scr[...] = wgu_ref[0].astype(BF16)
        wd_scr[...] = wd_ref[0].astype(BF16)

    @pl.when(i < nu_ref[0])
    def _():
        x = x_ref[...].astype(BF16)
        gu = _dot(x, wgu_scr[...]) + bgu_ref[0]
        glu = jnp.minimum(gu[:, :D_FF], SWIGLU_LIMIT)
        lin = jnp.clip(gu[:, D_FF:], -SWIGLU_LIMIT, SWIGLU_LIMIT)
        act = glu * _sigmoid(SWIGLU_ALPHA * glu) * (lin + 1.0)
        y_ref[...] = _dot(act.astype(BF16), wd_scr[...]) + bd_ref[0]

    @pl.when(i >= nu_ref[0])
    def _():
        y_ref[...] = jnp.zeros(y_ref.shape, F32)


def _moe(block_expert, n_used, xb, wgu, bgu, wd, bd):
    n_rows, d = xb.shape
    nblk = n_rows // MOE_ROWS
    ne = wgu.shape[0]
    emap = lambda i, be, nu: (be[i], 0, 0)
    grid_spec = pltpu.PrefetchScalarGridSpec(
        num_scalar_prefetch=2,
        grid=(nblk,),
        in_specs=[pl.BlockSpec((MOE_ROWS, d), lambda i, be, nu: (i, 0)),
                  pl.BlockSpec((1, d, 2 * D_FF), emap),
                  pl.BlockSpec((1, 1, 2 * D_FF), emap),
                  pl.BlockSpec((1, D_FF, d), emap),
                  pl.BlockSpec((1, 1, d), emap)],
        out_specs=pl.BlockSpec((MOE_ROWS, d), lambda i, be, nu: (i, 0)),
        scratch_shapes=[pltpu.VMEM((d, 2 * D_FF), BF16), pltpu.VMEM((D_FF, d), BF16)])
    return pl.pallas_call(
        _moe_body,
        grid_spec=grid_spec,
        out_shape=jax.ShapeDtypeStruct((n_rows, d), F32),
        compiler_params=_cparams(("arbitrary",)),
        name="experts",
    )(block_expert, n_used, xb, wgu, bgu.reshape(ne, 1, 2 * D_FF), wd, bd.reshape(ne, 1, d))


def _combine_body(dest_ref, yb_ref, gate_ref, x1_ref, gt_ref, gf_ref, y_ref, buf, sem, *, tm, tok0):
    i = pl.program_id(0)

    def copy(r, k):
        src = dest_ref[(tok0 + i * tm + r) * TOP_K + k]
        return pltpu.make_async_copy(yb_ref.at[pl.ds(src, 1), :], buf.at[k, pl.ds(r, 1), :], sem)

    def start(r, carry):
        for k in range(TOP_K):
            copy(r, k).start()
        return carry

    def wait(r, carry):
        for k in range(TOP_K):
            copy(r, k).wait()
        return carry

    lax.fori_loop(0, tm, start, 0)
    lax.fori_loop(0, tm, wait, 0)
    g = gate_ref[...]
    y = g[:, 0:1] * buf[0]
    for k in range(1, TOP_K):
        y = y + g[:, k:k + 1] * buf[k]
    x2 = x1_ref[...] + gt_ref[0] * y
    ms = jnp.mean(x2 * x2, axis=-1, keepdims=True)
    y_ref[...] = x2 * lax.rsqrt(ms + EPS) * gf_ref[...]


def _combine(dest_flat, yb, gates, x1, gt, g_final, tm, rows_per_mod, tok0):
    r, d = x1.shape
    nt = r // tm
    tiles_per_mod = max(rows_per_mod // tm, 1)
    tile0 = tok0 // tm
    mod_block = (1,) + gt.shape[1:]
    grid_spec = pltpu.PrefetchScalarGridSpec(
        num_scalar_prefetch=1,
        grid=(nt,),
        in_specs=[pl.BlockSpec(memory_space=pl.ANY),
                  pl.BlockSpec((tm, LANES), lambda i, dr: (tile0 + i, 0)),
                  pl.BlockSpec((tm, d), lambda i, dr: (i, 0)),
                  pl.BlockSpec(mod_block, lambda i, dr: (i // tiles_per_mod, 0, 0)),
                  pl.BlockSpec((1, d), lambda i, dr: (0, 0))],
        out_specs=pl.BlockSpec((tm, d), lambda i, dr: (i, 0)),
        scratch_shapes=[pltpu.VMEM((TOP_K, tm, d), F32), pltpu.SemaphoreType.DMA(())])
    return pl.pallas_call(
        functools.partial(_combine_body, tm=tm, tok0=tok0),
        grid_spec=grid_spec,
        out_shape=jax.ShapeDtypeStruct((r, d), F32),
        compiler_params=_cparams(("arbitrary",)),
        name="combine",
    )(dest_flat, yb, gates, x1, gt, g_final)


def _rope_tables(pos):
    half = MOBA_HD // 2
    inv = 1.0 / (ROPE_THETA ** (jnp.arange(half, dtype=F32) / half))
    ang = pos.astype(F32)[:, None] * inv[None, :]
    cos = jnp.cos(ang)
    sin = jnp.sin(ang)
    cos_h = jnp.concatenate([cos, cos], axis=1)
    sin_h = jnp.concatenate([-sin, sin], axis=1)
    return jnp.tile(cos_h, (1, MOBA_HEADS)), jnp.tile(sin_h, (1, MOBA_HEADS))


def _row_tile(n, cap):
    t = min(n, cap)
    while n % t:
        t //= 2
    return t


def kernel(x_prompt, x_sample, cache_k, cache_v, state_gla, page_table, c_prompt, c_sample, w_ada, b_ada, g_norm1, w_in, w_gla_a2, b_gla_a2, g_gla_head, w_proj_gla, w_proj_moba, w_out, g_norm2, w_router, b_router, w_gate_up, b_gate_up, w_down, b_down, g_final):
    nb_p, seq, d = x_prompt.shape
    nb_s, dec = x_sample.shape[:2]
    past_len = page_table.shape[1] * cache_k.shape[2]
    assert w_ada.shape[0] == 1 and d == D_MODEL
    assert seq % MOBA_BLOCK == 0 and past_len % MOBA_BLOCK == 0 and dec <= MOBA_BLOCK and dec % 8 == 0
    r_p, r_s = nb_p * seq, nb_s * dec
    t_all = r_p + r_s

    c_all = jnp.concatenate([c_prompt, c_sample], axis=0)
    mod = _ada(c_all, w_ada[0], b_ada[0]).reshape(nb_p + nb_s, 6, d)
    mod_p = [mod[:nb_p, k].reshape(nb_p, 1, d) for k in range(6)]
    mod_s = [jnp.broadcast_to(mod[nb_p:, k][:, None, :], (nb_s, dec, d)).reshape(1, r_s, d) for k in range(6)]

    w = w_in[0]
    gkw = GLA_HEADS * GLA_DK
    gvw = GLA_HEADS * GLA_DV
    c_ga = 2 * gkw + 2 * gvw
    pad = jnp.zeros((d, PROJ_TILE - GLA_LOWRANK), w.dtype)
    w_in_p = jnp.concatenate([w[:, :c_ga + GLA_LOWRANK], pad, w[:, c_ga + GLA_LOWRANK:]], axis=1).astype(BF16)
    assert w_in_p.shape[1] == N_PROJ_TILES * PROJ_TILE
    w2_p = jnp.zeros((PROJ_TILE, gkw), BF16).at[:GLA_LOWRANK].set(w_gla_a2[0].astype(BF16))
    b2 = b_gla_a2[0].reshape(1, gkw)
    g1 = g_norm1[0].reshape(1, d)
    g2 = g_norm2[0].reshape(1, d)
    wpg = w_proj_gla[0].astype(BF16)
    wpm = w_proj_moba[0].astype(BF16)
    wo = w_out[0].astype(BF16)
    wr = jnp.zeros((d, LANES), F32).at[:, :N_EXPERTS].set(w_router[0])
    br = jnp.full((1, LANES), NEG_INF, F32).at[0, :N_EXPERTS].set(b_router[0])
    g_head = g_gla_head[0].reshape(1, GLA_DV)

    cos_p, sin_p = _rope_tables(jnp.arange(seq, dtype=I32))
    cos_s, sin_s = _rope_tables(past_len + jnp.arange(dec, dtype=I32))
    cos_s, sin_s = jnp.tile(cos_s, (nb_s, 1)), jnp.tile(sin_s, (nb_s, 1))

    xp2 = x_prompt.reshape(r_p, d)
    xs2 = x_sample.reshape(r_s, d)
    tm_p = _row_tile(seq, 1024)

    proj_p, k5_p, v5_p = _inproj(xp2, mod_p[1], mod_p[0], g1, w_in_p, w2_p, b2, cos_p, sin_p, tm_p, seq)
    s0_p = jnp.zeros((nb_p, GLA_HEADS, GLA_DK, GLA_DV), state_gla.dtype)
    og_p, st_p = _gla(proj_p, s0_p, g_head, nb_p, seq, _row_tile(seq, 64))
    om_p = _moba_prompt(proj_p, nb_p, seq)
    tm_post = _row_tile(seq, 512)
    x1_p, h2_p, lg_p = _post(og_p, om_p, proj_p, xp2, mod_p[2], mod_p[4], mod_p[3], g2, wpg, wpm, wo, wr, br,
                             tm_post, seq)

    proj_s, k5_s, v5_s = _inproj(xs2, mod_s[1], mod_s[0], g1, w_in_p, w2_p, b2, cos_s, sin_s, r_s, r_s)
    og_s, st_s = _gla(proj_s, state_gla[0], g_head, nb_s, dec, dec)
    om_s = _moba_sample(proj_s, cache_k[0], cache_v[0], page_table, nb_s, dec)
    x1_s, h2_s, lg_s = _post(og_s, om_s, proj_s, xs2, mod_s[2], mod_s[4], mod_s[3], g2, wpg, wpm, wo, wr, br,
                             r_s, r_s)

    h_all = jnp.concatenate([h2_p, h2_s], axis=0)
    lg_all = jnp.concatenate([lg_p, lg_s], axis=0)
    tm_r = _row_tile(t_all, 256)
    eidx, rank, gates, cnt = _route(lg_all, tm_r)
    counts = cnt[0, :N_EXPERTS].astype(I32)
    nblk_e = (counts + MOE_ROWS - 1) // MOE_ROWS
    blk_end = jnp.cumsum(nblk_e)
    blk_start = blk_end - nblk_e
    dest = blk_start[eidx[:, :TOP_K]] * MOE_ROWS + rank[:, :TOP_K]
    dest_flat = dest.reshape(-1).astype(I32)
    n_blocks = -(-(t_all * TOP_K) // MOE_ROWS) + N_EXPERTS
    blk_ids = jnp.arange(n_blocks, dtype=I32)
    block_expert = jnp.minimum(jnp.sum((blk_end[None, :] <= blk_ids[:, None]).astype(I32), axis=1), N_EXPERTS - 1)
    n_used = blk_end[-1:].astype(I32)
    xb = _dispatch(dest_flat, h_all, n_blocks * MOE_ROWS, tm_r)
    yb = _moe(block_expert, n_used, xb, w_gate_up[0], b_gate_up[0], w_down[0], b_down[0])
    gf = g_final.reshape(1, d)
    tm_c = _row_tile(seq, 256)
    y_p = _combine(dest_flat, yb, gates, x1_p, mod_p[5], gf, tm_c, seq, 0)
    y_s = _combine(dest_flat, yb, gates, x1_s, mod_s[5], gf, r_s, r_s, r_p)

    k_p = k5_p.reshape(1, nb_p, seq, MOBA_HEADS, MOBA_HD)
    v_p = v5_p.reshape(1, nb_p, seq, MOBA_HEADS, MOBA_HD)
    k_s = k5_s.reshape(1, nb_s, dec, MOBA_HEADS, MOBA_HD)
    v_s = v5_s.reshape(1, nb_s, dec, MOBA_HEADS, MOBA_HD)
    return (y_p.reshape(nb_p, seq, d), y_s.reshape(nb_s, dec, d), k_p, v_p, st_p[None],
            k_s, v_s, st_s[None])
```

```python
import functools

import jax
import jax.numpy as jnp
from jax import lax
from jax.experimental import pallas as pl
from jax.experimental.pallas import tpu as pltpu

F32 = jnp.float32
BF16 = jnp.bfloat16
I32 = jnp.int32

D_MODEL = 1024
GLA_HEADS = 4
GLA_DK = 128
GLA_DV = 256
GLA_LOWRANK = 16
GLA_GATE_NORM = 16.0
GLA_SUB = 16
MOBA_HEADS = 8
MOBA_HD = 64
MOBA_BLOCK = 256
MOBA_TOPK = 3
MOBA_W = MOBA_HEADS * MOBA_HD
ROPE_THETA = 10000.0
N_EXPERTS = 32
TOP_K = 4
D_FF = D_MODEL
SWIGLU_LIMIT = 7.0
SWIGLU_ALPHA = 1.702
EPS = 1e-6
LANES = 128
PROJ_TILE = 512

T_GQ, T_GK, T_GV, T_GR, T_GA, T_MQ, T_MK, T_MV, T_BA, T_BB = 0, 1, 2, 4, 6, 7, 8, 9, 10, 12
N_PROJ_TILES = 14
MOE_ROWS = 256
MOBA_S_BLOCKS_PER_STEP = 4
VMEM_LIMIT = 56 * 1024 * 1024

NEG_INF = float("-inf")


def _sigmoid(x):
    return 1.0 / (1.0 + jnp.exp(-x))


def _split3(x):
    hi = x.astype(BF16)
    r = x - hi.astype(F32)
    mid = r.astype(BF16)
    lo = (r - mid.astype(F32)).astype(BF16)
    return hi, mid, lo


def _dot(a, b):
    return jnp.dot(a, b, preferred_element_type=F32)


def _dot_nt(a, b):
    return lax.dot_general(a, b, (((1,), (1,)), ((), ())), preferred_element_type=F32)


def _dot_nt_f32(a, b):
    ah = a.astype(BF16)
    al = (a - ah.astype(F32)).astype(BF16)
    bh = b.astype(BF16)
    bl = (b - bh.astype(F32)).astype(BF16)
    return _dot_nt(ah, bh) + _dot_nt(ah, bl) + _dot_nt(al, bh)


def _div(x, n):
    assert n & (n - 1) == 0
    return lax.shift_right_logical(x, n.bit_length() - 1)


def _mod(x, n):
    assert n & (n - 1) == 0
    return x & (n - 1)


def _cparams(sem):
    return pltpu.CompilerParams(dimension_semantics=sem, vmem_limit_bytes=VMEM_LIMIT)


def _ada_body(c_ref, w_ref, b_ref, o_ref):
    c = c_ref[...]
    s = c * _sigmoid(c)
    o_ref[...] = _dot(s.astype(BF16), w_ref[...].astype(BF16)) + b_ref[...]


def _ada(c, w, b):
    n = w.shape[1]
    tn = n // 4
    return pl.pallas_call(
        _ada_body,
        grid=(4,),
        in_specs=[pl.BlockSpec(c.shape, lambda j: (0, 0)),
                  pl.BlockSpec((w.shape[0], tn), lambda j: (0, j)),
                  pl.BlockSpec((1, tn), lambda j: (0, j))],
        out_specs=pl.BlockSpec((c.shape[0], tn), lambda j: (0, j)),
        out_shape=jax.ShapeDtypeStruct((c.shape[0], n), F32),
        compiler_params=_cparams(("arbitrary",)),
        name="ada",
    )(c, w, b.reshape(1, n))


def _rotary(x, cos, sin_signed):
    lane = lax.broadcasted_iota(I32, x.shape, 1)
    half = MOBA_HD // 2
    partner = jnp.where((lane & (MOBA_HD - 1)) < half, lane + half, lane - half)
    w = x.shape[1]
    r1 = pltpu.roll(x, half, 1)
    i1 = pltpu.roll(lane, half, 1)
    r2 = pltpu.roll(x, w - half, 1)
    swapped = jnp.where(i1 == partner, r1, r2)
    return x * cos + swapped * sin_signed


def _log_sigmoid(x):
    return jnp.minimum(x, 0.0) - jnp.log(1.0 + jnp.exp(-jnp.abs(x)))


def _store_heads(ref, val, token_minor):
    if token_minor:
        ref[0] = val.T
        return
    tm = val.shape[0]
    for h in range(MOBA_HEADS):
        ref[pl.ds(h, tm, stride=MOBA_HEADS), :] = val[:, h * MOBA_HD:(h + 1) * MOBA_HD]


def _inproj_body(x_ref, sc_ref, sh_ref, g_ref, w_ref, w2_ref, b2_ref, cos_ref, sin_ref, o_ref, k5_ref, v5_ref,
                 h_scr, *, token_minor):
    j = pl.program_id(1)

    @pl.when(j == 0)
    def _():
        x = x_ref[...]
        ms = jnp.mean(x * x, axis=-1, keepdims=True)
        y = x * lax.rsqrt(ms + EPS) * g_ref[...]
        h = y * (1.0 + sc_ref[0]) + sh_ref[0]
        h_scr[...] = h.astype(BF16)

    acc = _dot(h_scr[...], w_ref[...])
    is_la = j == T_GA
    special = functools.reduce(jnp.logical_or, [j == T_MQ, j == T_MK, j == T_MV, is_la])

    @pl.when(j == T_MQ)
    def _():
        o_ref[0] = _rotary(acc, cos_ref[...], sin_ref[...])

    @pl.when(j == T_MK)
    def _():
        rot = _rotary(acc, cos_ref[...], sin_ref[...])
        o_ref[0] = rot
        _store_heads(k5_ref, rot, token_minor)

    @pl.when(j == T_MV)
    def _():
        o_ref[0] = acc
        _store_heads(v5_ref, acc, token_minor)

    @pl.when(is_la)
    def _():
        z = _dot(acc.astype(BF16), w2_ref[...]) + b2_ref[...]
        o_ref[0] = _log_sigmoid(z) * (1.0 / GLA_GATE_NORM)

    @pl.when(jnp.logical_not(special))
    def _():
        o_ref[0] = acc


def _inproj(x2, sc, sh, g1, w_in_p, w2_p, b2, cos, sin, tm, rows_per_mod, token_minor):
    r, d = x2.shape
    nt = r // tm
    tiles_per_mod = max(rows_per_mod // tm, 1)
    tiles_per_tab = cos.shape[0] // tm
    mod_block = (1,) + sc.shape[1:]
    if token_minor:
        seq = cos.shape[0]
        kv_spec = pl.BlockSpec((1, MOBA_W, tm), lambda i, j: (i // tiles_per_tab, 0, i % tiles_per_tab))
        kv_shape = jax.ShapeDtypeStruct((r // seq, MOBA_W, seq), F32)
    else:
        kv_spec = pl.BlockSpec((tm * MOBA_HEADS, MOBA_HD), lambda i, j: (i, 0))
        kv_shape = jax.ShapeDtypeStruct((r * MOBA_HEADS, MOBA_HD), F32)
    return pl.pallas_call(
        functools.partial(_inproj_body, token_minor=token_minor),
        grid=(nt, N_PROJ_TILES),
        in_specs=[pl.BlockSpec((tm, d), lambda i, j: (i, 0)),
                  pl.BlockSpec(mod_block, lambda i, j: (i // tiles_per_mod, 0, 0)),
                  pl.BlockSpec(mod_block, lambda i, j: (i // tiles_per_mod, 0, 0)),
                  pl.BlockSpec((1, d), lambda i, j: (0, 0)),
                  pl.BlockSpec((d, PROJ_TILE), lambda i, j: (0, j)),
                  pl.BlockSpec((PROJ_TILE, PROJ_TILE), lambda i, j: (0, 0)),
                  pl.BlockSpec((1, PROJ_TILE), lambda i, j: (0, 0)),
                  pl.BlockSpec((tm, PROJ_TILE), lambda i, j: (i % tiles_per_tab, 0)),
                  pl.BlockSpec((tm, PROJ_TILE), lambda i, j: (i % tiles_per_tab, 0))],
        out_specs=[pl.BlockSpec((1, tm, PROJ_TILE), lambda i, j: (j, i, 0)), kv_spec, kv_spec],
        out_shape=[jax.ShapeDtypeStruct((N_PROJ_TILES, r, PROJ_TILE), F32), kv_shape, kv_shape],
        scratch_shapes=[pltpu.VMEM((tm, d), BF16)],
        compiler_params=_cparams(("arbitrary", "arbitrary")),
        name="inproj",
    )(x2, sc, sh, g1, w_in_p, w2_p, b2, cos, sin)


def _gla_body(q_ref, k_ref, v_ref, la_ref, gr_ref, s0_ref, gh_ref, og_ref, sout_ref, st_scr, *, C, SB):
    c = pl.program_id(1)
    nc = pl.num_programs(1)

    @pl.when(c == 0)
    def _():
        for h in range(GLA_HEADS):
            st_scr[h] = s0_ref[0, h].T

    la = la_ref[0]
    row_c = lax.broadcasted_iota(I32, (C, C), 0)
    col_c = lax.broadcasted_iota(I32, (C, C), 1)
    tri = jnp.where(row_c >= col_c, 1.0, 0.0).astype(BF16)
    hi, mid, lo = _split3(la)
    b_all = _dot(tri, hi) + _dot(tri, mid) + _dot(tri, lo)
    q_all = q_ref[0] * (GLA_DK ** -0.5)
    k_all = k_ref[0]
    gh = gh_ref[...]
    row_k = lax.broadcasted_iota(I32, (C, GLA_DK), 0)
    row_sb = lax.broadcasted_iota(I32, (SB, 1), 0)
    lane_sb = lax.broadcasted_iota(I32, (SB, C), 1)

    for h in range(GLA_HEADS):
        ks = slice(h * GLA_DK, (h + 1) * GLA_DK)
        vs = slice((h % 2) * GLA_DV, (h % 2 + 1) * GLA_DV)
        b = b_all[:, ks]
        q = q_all[:, ks]
        k = k_all[:, ks]
        v = v_ref[h // 2][:, vs]
        st = st_scr[h]
        b_last = b[C - 1:C, :]
        o = _dot_nt((q * jnp.exp(b)).astype(BF16), st.astype(BF16))
        att_rows = []
        for i in range(C // SB):
            r0 = i * SB
            bi = b[r0:r0 + SB]
            qi = q[r0:r0 + SB]
            ki = k[r0:r0 + SB]
            if i > 0:
                bref = b[r0 - 1:r0, :]
                qt = qi * jnp.exp(bi - bref)
                kt = jnp.where(row_k < r0, k * jnp.exp(jnp.minimum(bref - b, 0.0)), 0.0)
                att_i = _dot_nt(qt.astype(BF16), kt.astype(BF16))
            else:
                att_i = jnp.zeros((SB, C), F32)
            for s in range(SB):
                dec = jnp.exp(jnp.minimum(bi - bi[s:s + 1, :], 0.0))
                col = jnp.sum(qi * ki[s:s + 1, :] * dec, axis=1, keepdims=True)
                col = jnp.where(row_sb >= s, col, 0.0)
                att_i = att_i + jnp.where(lane_sb == r0 + s, col, 0.0)
            att_rows.append(att_i)
        att = att_rows[0] if len(att_rows) == 1 else jnp.concatenate(att_rows, axis=0)
        vb = v.astype(BF16)
        o = o + _dot(att.astype(BF16), vb)
        kd = k * jnp.exp(b_last - b)
        st_scr[h] = st * jnp.exp(b_last) + _dot(v.T.astype(BF16), kd.astype(BF16))
        on = o * lax.rsqrt(jnp.mean(o * o, axis=-1, keepdims=True) + EPS) * gh
        gr = gr_ref[h // 2][:, vs]
        og_ref[:, h * GLA_DV:(h + 1) * GLA_DV] = (on * (gr * _sigmoid(gr))).astype(og_ref.dtype)

    @pl.when(c == nc - 1)
    def _():
        for h in range(GLA_HEADS):
            sout_ref[0, h] = st_scr[h].T


def _gla(proj3, s0, g_head, n_seq, seq_len, chunk):
    r = proj3.shape[1]
    nc = seq_len // chunk
    sb = min(GLA_SUB, chunk)
    out_dtype = BF16 if chunk % 16 == 0 else F32
    row = lambda b, c: b * nc + c
    body = functools.partial(_gla_body, C=chunk, SB=sb)
    return pl.pallas_call(
        body,
        grid=(n_seq, nc),
        in_specs=[pl.BlockSpec((1, chunk, PROJ_TILE), lambda b, c: (T_GQ, row(b, c), 0)),
                  pl.BlockSpec((1, chunk, PROJ_TILE), lambda b, c: (T_GK, row(b, c), 0)),
                  pl.BlockSpec((2, chunk, PROJ_TILE), lambda b, c: (T_GV // 2, row(b, c), 0)),
                  pl.BlockSpec((1, chunk, PROJ_TILE), lambda b, c: (T_GA, row(b, c), 0)),
                  pl.BlockSpec((2, chunk, PROJ_TILE), lambda b, c: (T_GR // 2, row(b, c), 0)),
                  pl.BlockSpec((1, GLA_HEADS, GLA_DK, GLA_DV), lambda b, c: (b, 0, 0, 0)),
                  pl.BlockSpec((1, GLA_DV), lambda b, c: (0, 0))],
        out_specs=[pl.BlockSpec((chunk, GLA_HEADS * GLA_DV), lambda b, c: (row(b, c), 0)),
                   pl.BlockSpec((1, GLA_HEADS, GLA_DK, GLA_DV), lambda b, c: (b, 0, 0, 0))],
        out_shape=[jax.ShapeDtypeStruct((r, GLA_HEADS * GLA_DV), out_dtype),
                   jax.ShapeDtypeStruct((n_seq, GLA_HEADS, GLA_DK, GLA_DV), F32)],
        scratch_shapes=[pltpu.VMEM((GLA_HEADS, GLA_DV, GLA_DK), F32)],
        compiler_params=_cparams(("arbitrary", "arbitrary")),
        name="gla",
    )(proj3, proj3, proj3, proj3, proj3, s0, g_head)


def _select_topk(g, n_valid_rows, n_rows, n_sel):
    row = lax.broadcasted_iota(I32, g.shape, 0)
    valid = row < n_valid_rows
    gm = jnp.where(valid, g, NEG_INF)
    rank = jnp.zeros(g.shape, F32)
    for jp in range(n_rows):
        gj = gm[jp:jp + 1, :]
        ahead = jnp.logical_or(gj > gm, jnp.logical_and(gj == gm, jp < row))
        rank = rank + jnp.where(ahead, 1.0, 0.0)
    return jnp.where(jnp.logical_and(valid, rank < n_sel), 1.0, 0.0)


def _moba_p_body(q_ref, k_ref, v_ref, o_ref, k_scr, vt_scr, km_scr, sel_scr, m_scr, l_scr, acc_scr, *, nb, nbp):
    i = pl.program_id(1)
    blk = MOBA_BLOCK
    scale = MOBA_HD ** -0.5

    @pl.when(i == 0)
    def _():
        km_scr[...] = jnp.zeros(km_scr.shape, F32)
        for j in range(nb):
            kj = k_ref[0, j * blk:(j + 1) * blk, :]
            k_scr[j] = kj.astype(BF16)
            km_scr[j:j + 1, :] = jnp.mean(kj, axis=0, keepdims=True)
            vt_scr[j] = v_ref[0, :, j * blk:(j + 1) * blk].astype(BF16)

    q = q_ref[0]
    km = km_scr[...]
    km_rep = jnp.concatenate([km] * MOBA_HEADS, axis=0)
    wrow = lax.broadcasted_iota(I32, km_rep.shape, 0)
    wlane = lax.broadcasted_iota(I32, km_rep.shape, 1)
    wt = jnp.where(_div(wlane, MOBA_HD) == _div(wrow, nbp), km_rep, 0.0)
    gates_t = _dot_nt_f32(wt, q)
    for h in range(MOBA_HEADS):
        sel_scr[h] = _select_topk(gates_t[h * nbp:(h + 1) * nbp, :], i, nb, MOBA_TOPK)

    n_pair = MOBA_HEADS // 2
    pair_lane = lax.broadcasted_iota(I32, (blk, LANES), 1)
    key_row = lax.broadcasted_iota(I32, (blk, 2 * blk), 0)
    q_col = _mod(lax.broadcasted_iota(I32, (blk, 2 * blk), 1), blk)
    qms = []
    for pr in range(n_pair):
        qp = q[:, pr * LANES:(pr + 1) * LANES]
        qms.append(jnp.concatenate([jnp.where(_div(pair_lane, MOBA_HD) == u, qp, 0.0) for u in range(2)],
                                   axis=0).astype(BF16))

    def scores(j, pr):
        kj = k_scr[j, :, pr * LANES:(pr + 1) * LANES]
        return _dot_nt(kj, qms[pr]) * scale

    def values(j, h):
        return vt_scr[j, h * MOBA_HD:(h + 1) * MOBA_HD, :]

    for pr in range(n_pair):
        s = jnp.where(key_row <= q_col, scores(i, pr), NEG_INF)
        m = jnp.max(s, axis=0, keepdims=True)
        p = jnp.exp(s - m)
        m_scr[pr] = m
        l_scr[pr] = jnp.sum(p, axis=0, keepdims=True)
        pb = p.astype(BF16)
        for u in range(2):
            acc_scr[2 * pr + u] = _dot(values(i, 2 * pr + u), pb[:, u * blk:(u + 1) * blk])

    def step(j, carry):
        for pr in range(n_pair):
            keep = jnp.concatenate([sel_scr[2 * pr + u, pl.ds(j, 1), :] for u in range(2)], axis=1)
            s = jnp.where(keep > 0.5, scores(j, pr), NEG_INF)
            m_old = m_scr[pr]
            m_new = jnp.maximum(m_old, jnp.max(s, axis=0, keepdims=True))
            alpha = jnp.exp(m_old - m_new)
            p = jnp.exp(s - m_new)
            m_scr[pr] = m_new
            l_scr[pr] = l_scr[pr] * alpha + jnp.sum(p, axis=0, keepdims=True)
            pb = p.astype(BF16)
            for u in range(2):
                h = 2 * pr + u
                cs = slice(u * blk, (u + 1) * blk)
                acc_scr[h] = acc_scr[h] * alpha[:, cs] + _dot(values(j, h), pb[:, cs])
        return carry

    lax.fori_loop(0, i, step, 0)
    outs = []
    for h in range(MOBA_HEADS):
        l = l_scr[h // 2][:, (h % 2) * blk:(h % 2 + 1) * blk]
        outs.append(acc_scr[h] / l)
    out_t = jnp.concatenate(outs, axis=0)
    o_ref[...] = out_t.T.astype(o_ref.dtype)


def _moba_prompt(proj3, v_t, n_seq, seq_len):
    r = proj3.shape[1]
    nb = seq_len // MOBA_BLOCK
    nbp = -(-nb // 8) * 8
    body = functools.partial(_moba_p_body, nb=nb, nbp=nbp)
    return pl.pallas_call(
        body,
        grid=(n_seq, nb),
        in_specs=[pl.BlockSpec((1, MOBA_BLOCK, MOBA_W), lambda b, i: (T_MQ, b * nb + i, 0)),
                  pl.BlockSpec((1, seq_len, MOBA_W), lambda b, i: (T_MK, b, 0)),
                  pl.BlockSpec((1, MOBA_W, seq_len), lambda b, i: (b, 0, 0))],
        out_specs=pl.BlockSpec((MOBA_BLOCK, MOBA_W), lambda b, i: (b * nb + i, 0)),
        out_shape=jax.ShapeDtypeStruct((r, MOBA_W), BF16),
        scratch_shapes=[pltpu.VMEM((nb, MOBA_BLOCK, MOBA_W), BF16),
                        pltpu.VMEM((nb, MOBA_W, MOBA_BLOCK), BF16),
                        pltpu.VMEM((nbp, MOBA_W), F32),
                        pltpu.VMEM((MOBA_HEADS, nbp, MOBA_BLOCK), F32),
                        pltpu.VMEM((MOBA_HEADS // 2, 1, 2 * MOBA_BLOCK), F32),
                        pltpu.VMEM((MOBA_HEADS // 2, 1, 2 * MOBA_BLOCK), F32),
                        pltpu.VMEM((MOBA_HEADS, MOBA_HD, MOBA_BLOCK), F32)],
        compiler_params=_cparams(("arbitrary", "arbitrary")),
        name="moba_prompt",
    )(proj3, proj3, v_t)


def _moba_s_body(pt_ref, *refs, nbp, L, bps):
    del pt_ref
    page_refs = refs[:4 * bps]
    qn_ref, kn_ref, vn_ref, o_ref, q2_scr, gate_scr, m_scr, l_scr, acc_scr = refs[4 * bps:]
    j = pl.program_id(1)
    n_steps = nbp // bps
    scale = MOBA_HD ** -0.5
    n_row = MOBA_HEADS * L
    blk_lane = lax.broadcasted_iota(I32, (n_row, LANES), 1)

    @pl.when(j == 0)
    def _():
        qn = qn_ref[...]
        q2 = jnp.concatenate([qn] * MOBA_HEADS, axis=0)
        row = lax.broadcasted_iota(I32, q2.shape, 0)
        lane = lax.broadcasted_iota(I32, q2.shape, 1)
        q2_scr[...] = jnp.where(_div(lane, MOBA_HD) == _div(row, L), q2, 0.0).astype(BF16)
        gate_scr[...] = jnp.full(gate_scr.shape, NEG_INF, F32)
        m_scr[...] = jnp.zeros(m_scr.shape, F32)
        l_scr[...] = jnp.zeros(l_scr.shape, F32)

    def softmax_partial(s):
        m = jnp.max(s, axis=1, keepdims=True)
        p = jnp.exp(s - m)
        return m, jnp.sum(p, axis=1, keepdims=True), p.astype(BF16)

    @pl.when(j < n_steps)
    def _():
        for t in range(bps):
            ka_ref, kb_ref, va_ref, vb_ref = page_refs[4 * t:4 * t + 4]
            jj = j * bps + t
            kt = jnp.concatenate([ka_ref[0], kb_ref[0]], axis=1).astype(BF16)
            vt = jnp.concatenate([va_ref[0], vb_ref[0]], axis=1).astype(BF16)
            s = _dot(q2_scr[...], kt)
            gate = jnp.mean(s, axis=1, keepdims=True)
            m, l, p = softmax_partial(s * scale)
            hot = blk_lane == jj
            gate_scr[...] = jnp.where(hot, gate, gate_scr[...])
            m_scr[...] = jnp.where(hot, m, m_scr[...])
            l_scr[...] = jnp.where(hot, l, l_scr[...])
            acc_scr[jj] = _dot_nt(p, vt)

    @pl.when(j == n_steps)
    def _():
        pad = jnp.zeros((LANES - L, MOBA_W), F32)
        kn = jnp.concatenate([kn_ref[...], pad], axis=0).astype(BF16)
        vn = jnp.concatenate([vn_ref[...], pad], axis=0).astype(BF16)
        s = _dot_nt(q2_scr[...], kn) * scale
        key = lax.broadcasted_iota(I32, s.shape, 1)
        qpos = _mod(lax.broadcasted_iota(I32, s.shape, 0), L)
        m_own, l_own, p_own = softmax_partial(jnp.where(key <= qpos, s, NEG_INF))
        acc_own = _dot(p_own, vn)
        g = gate_scr[...]
        rank = jnp.zeros(g.shape, F32)
        for jp in range(nbp):
            gj = g[:, jp:jp + 1]
            ahead = jnp.logical_or(gj > g, jnp.logical_and(gj == g, jp < blk_lane))
            rank = rank + jnp.where(ahead, 1.0, 0.0)
        sel = jnp.logical_and(blk_lane < nbp, rank < min(MOBA_TOPK, nbp + 1))
        m_all = m_scr[...]
        m_top = jnp.maximum(m_own, jnp.max(jnp.where(sel, m_all, NEG_INF), axis=1, keepdims=True))
        w = jnp.where(sel, jnp.exp(m_all - m_top), 0.0)
        w_own = jnp.exp(m_own - m_top)
        den = jnp.sum(w * l_scr[...], axis=1, keepdims=True) + w_own * l_own
        num = w_own * acc_own
        for jj in range(nbp):
            num = num + w[:, jj:jj + 1] * acc_scr[jj]
        out_r = num / den
        lane = lax.broadcasted_iota(I32, (L, MOBA_W), 1)
        out = jnp.zeros((L, MOBA_W), F32)
        for h in range(MOBA_HEADS):
            out = out + jnp.where(_div(lane, MOBA_HD) == h, out_r[h * L:(h + 1) * L, :], 0.0)
        o_ref[...] = out


def _moba_sample(proj3, cache_k, cache_v, page_table, n_seq, L):
    n_pool, page = cache_k.shape[0], cache_k.shape[1]
    n_pages = page_table.shape[1]
    ppb = MOBA_BLOCK // page
    nbp = n_pages // ppb
    n_row = MOBA_HEADS * L
    assert ppb == 2 and n_pages % ppb == 0 and page == LANES and nbp <= LANES and n_row % 16 == 0 and L <= LANES
    ck = jnp.transpose(cache_k, (0, 2, 3, 1)).reshape(n_pool, MOBA_W, page)
    cv = jnp.transpose(cache_v, (0, 2, 3, 1)).reshape(n_pool, MOBA_W, page)
    pt = page_table.reshape(-1).astype(I32)

    bps = MOBA_S_BLOCKS_PER_STEP if nbp % MOBA_S_BLOCKS_PER_STEP == 0 else 1
    n_steps = nbp // bps

    def page_map(t, off):
        return lambda b, j, pt_ref: (
            pt_ref[b * n_pages + ppb * (jnp.minimum(j, n_steps - 1) * bps + t) + off], 0, 0)

    new_map = lambda t: (lambda b, j, pt_ref: (t, b, 0))
    body = functools.partial(_moba_s_body, nbp=nbp, L=L, bps=bps)
    page_specs = [pl.BlockSpec((1, MOBA_W, page), page_map(t, off)) for t in range(bps) for off in (0, 1, 0, 1)]
    page_args = [a for _ in range(bps) for a in (ck, ck, cv, cv)]
    grid_spec = pltpu.PrefetchScalarGridSpec(
        num_scalar_prefetch=1,
        grid=(n_seq, n_steps + 1),
        in_specs=page_specs + [
                  pl.BlockSpec((None, L, MOBA_W), new_map(T_MQ)),
                  pl.BlockSpec((None, L, MOBA_W), new_map(T_MK)),
                  pl.BlockSpec((None, L, MOBA_W), new_map(T_MV))],
        out_specs=pl.BlockSpec((L, MOBA_W), lambda b, j, pt_ref: (b, 0)),
        scratch_shapes=[pltpu.VMEM((n_row, MOBA_W), BF16),
                        pltpu.VMEM((n_row, LANES), F32),
                        pltpu.VMEM((n_row, LANES), F32),
                        pltpu.VMEM((n_row, LANES), F32),
                        pltpu.VMEM((nbp, n_row, MOBA_W), F32)])
    return pl.pallas_call(
        body,
        grid_spec=grid_spec,
        out_shape=jax.ShapeDtypeStruct((n_seq * L, MOBA_W), F32),
        compiler_params=_cparams(("arbitrary", "arbitrary")),
        name="moba_sample",
    )(pt, *page_args, proj3, proj3, proj3)


def _post_body(og_ref, om_ref, ba_ref, bb_ref, x_ref, gt_ref, sc_ref, sh_ref, g2_ref, wpg_ref, wpm_ref,
               wo_ref, wr_ref, br_ref, x1_ref, h2_ref, lg_ref):
    ya = _dot(og_ref[...].astype(BF16), wpg_ref[...])
    yb = _dot(om_ref[...].astype(BF16), wpm_ref[...])
    ba = jnp.concatenate([ba_ref[0], ba_ref[1]], axis=1)
    bb = jnp.concatenate([bb_ref[0], bb_ref[1]], axis=1)
    mix = _sigmoid(ba) * ya + _sigmoid(bb) * yb
    x1 = x_ref[...] + gt_ref[0] * _dot(mix.astype(BF16), wo_ref[...])
    x1_ref[...] = x1
    ms = jnp.mean(x1 * x1, axis=-1, keepdims=True)
    h2 = x1 * lax.rsqrt(ms + EPS) * g2_ref[...] * (1.0 + sc_ref[0]) + sh_ref[0]
    h2_ref[...] = h2
    hh = h2.astype(BF16)
    hl = (h2 - hh.astype(F32)).astype(BF16)
    wr = wr_ref[...]
    wh = wr.astype(BF16)
    wl = (wr - wh.astype(F32)).astype(BF16)
    lg_ref[...] = _dot(hh, wh) + _dot(hh, wl) + _dot(hl, wh) + br_ref[...]


def _post(og, om, proj3, x2, gt, sc, sh, g2, wpg, wpm, wo, wr, br, tm, rows_per_mod):
    r, d = x2.shape
    nt = r // tm
    tiles_per_mod = max(rows_per_mod // tm, 1)
    mod_block = (1,) + gt.shape[1:]
    mod_map = lambda i: (i // tiles_per_mod, 0, 0)
    full = lambda a: pl.BlockSpec(a.shape, lambda i: (0,) * a.ndim)
    return pl.pallas_call(
        _post_body,
        grid=(nt,),
        in_specs=[pl.BlockSpec((tm, og.shape[1]), lambda i: (i, 0)),
                  pl.BlockSpec((tm, om.shape[1]), lambda i: (i, 0)),
                  pl.BlockSpec((2, tm, PROJ_TILE), lambda i: (T_BA // 2, i, 0)),
                  pl.BlockSpec((2, tm, PROJ_TILE), lambda i: (T_BB // 2, i, 0)),
                  pl.BlockSpec((tm, d), lambda i: (i, 0)),
                  pl.BlockSpec(mod_block, mod_map),
                  pl.BlockSpec(mod_block, mod_map),
                  pl.BlockSpec(mod_block, mod_map),
                  full(g2), full(wpg), full(wpm), full(wo), full(wr), full(br)],
        out_specs=[pl.BlockSpec((tm, d), lambda i: (i, 0)),
                   pl.BlockSpec((tm, d), lambda i: (i, 0)),
                   pl.BlockSpec((tm, LANES), lambda i: (i, 0))],
        out_shape=[jax.ShapeDtypeStruct((r, d), F32),
                   jax.ShapeDtypeStruct((r, d), F32),
                   jax.ShapeDtypeStruct((r, LANES), F32)],
        compiler_params=_cparams(("arbitrary",)),
        name="post",
    )(og, om, proj3, proj3, x2, gt, sc, sh, g2, wpg, wpm, wo, wr, br)


def _route_body(lg_ref, eidx_ref, rank_ref, gate_ref, cnt_ref, run_scr):
    i = pl.program_id(0)
    tm = lg_ref.shape[0]

    @pl.when(i == 0)
    def _():
        run_scr[...] = jnp.zeros(run_scr.shape, F32)

    l = lg_ref[...]
    lane = lax.broadcasted_iota(I32, l.shape, 1)
    vals, hots = [], []
    for _ in range(TOP_K):
        m = jnp.max(l, axis=1, keepdims=True)
        idx = jnp.min(jnp.where(l == m, lane, LANES), axis=1, keepdims=True)
        hot = lane == idx
        vals.append(m)
        hots.append(hot)
        l = jnp.where(hot, NEG_INF, l)
    es = [jnp.exp(v - vals[0]) for v in vals]
    den = es[0] + es[1] + es[2] + es[3]
    chosen = jnp.zeros(l.shape, F32)
    for hot in hots:
        chosen = chosen + jnp.where(hot, 1.0, 0.0)
    row = lax.broadcasted_iota(I32, (tm, tm), 0)
    col = lax.broadcasted_iota(I32, (tm, tm), 1)
    before = jnp.where(row > col, 1.0, 0.0).astype(BF16)
    pos = _dot(before, chosen.astype(BF16)) + run_scr[...]
    eidx = jnp.zeros(l.shape, I32)
    rank = jnp.zeros(l.shape, I32)
    gate = jnp.zeros(l.shape, F32)
    for k in range(TOP_K):
        e_k = jnp.min(jnp.where(hots[k], lane, LANES), axis=1, keepdims=True)
        r_k = jnp.sum(jnp.where(hots[k], pos, 0.0), axis=1, keepdims=True).astype(I32)
        eidx = jnp.where(lane == k, e_k, eidx)
        rank = jnp.where(lane == k, r_k, rank)
        gate = jnp.where(lane == k, es[k] / den, gate)
    eidx_ref[...] = eidx
    rank_ref[...] = rank
    gate_ref[...] = gate
    run_scr[...] = run_scr[...] + jnp.sum(chosen, axis=0, keepdims=True)
    cnt_ref[...] = jnp.broadcast_to(run_scr[...], cnt_ref.shape)


def _route(logits, tm):
    t = logits.shape[0]
    blk = pl.BlockSpec((tm, LANES), lambda i: (i, 0))
    return pl.pallas_call(
        _route_body,
        grid=(t // tm,),
        in_specs=[blk],
        out_specs=[blk, blk, blk, pl.BlockSpec((8, LANES), lambda i: (0, 0))],
        out_shape=[jax.ShapeDtypeStruct((t, LANES), I32),
                   jax.ShapeDtypeStruct((t, LANES), I32),
                   jax.ShapeDtypeStruct((t, LANES), F32),
                   jax.ShapeDtypeStruct((8, LANES), F32)],
        scratch_shapes=[pltpu.VMEM((1, LANES), F32)],
        compiler_params=_cparams(("arbitrary",)),
        name="route",
    )(logits)


def _dispatch_body(dest_ref, h_ref, zero_ref, xb_ref, sem, *, tm):
    del zero_ref
    i = pl.program_id(0)

    def copy(r, k):
        dst = dest_ref[(i * tm + r) * TOP_K + k]
        return pltpu.make_async_copy(h_ref.at[pl.ds(r, 1), :], xb_ref.at[pl.ds(dst, 1), :], sem)

    def start(r, carry):
        for k in range(TOP_K):
            copy(r, k).start()
        return carry

    def wait(r, carry):
        for k in range(TOP_K):
            copy(r, k).wait()
        return carry

    lax.fori_loop(0, tm, start, 0)
    lax.fori_loop(0, tm, wait, 0)


def _dispatch(dest_flat, h_all, n_rows, tm):
    t, d = h_all.shape
    zeros = jnp.zeros((n_rows, d), h_all.dtype)
    grid_spec = pltpu.PrefetchScalarGridSpec(
        num_scalar_prefetch=1,
        grid=(t // tm,),
        in_specs=[pl.BlockSpec((tm, d), lambda i, dr: (i, 0)), pl.BlockSpec(memory_space=pl.ANY)],
        out_specs=pl.BlockSpec(memory_space=pl.ANY),
        scratch_shapes=[pltpu.SemaphoreType.DMA(())])
    return pl.pallas_call(
        functools.partial(_dispatch_body, tm=tm),
        grid_spec=grid_spec,
        out_shape=jax.ShapeDtypeStruct((n_rows, d), h_all.dtype),
        input_output_aliases={2: 0},
        compiler_params=_cparams(("arbitrary",)),
        name="dispatch",
    )(dest_flat, h_all, zeros)


def _moe_body(be_ref, nu_ref, x_ref, wgu_ref, bgu_ref, wd_ref, bd_ref, y_ref, wgu_scr, wd_scr):
    i = pl.program_id(0)
    prev = be_ref[jnp.maximum(i - 1, 0)]
    fresh = jnp.logical_or(i == 0, be_ref[i] != prev)

    @pl.when(jnp.logical_and(fresh, i < nu_ref[0]))
    def _():
        wgu_scr[...] = wgu_ref[0].astype(BF16)
        wd_scr[...] = wd_ref[0].astype(BF16)

    @pl.when(i < nu_ref[0])
    def _():
        x = x_ref[...].astype(BF16)
        gu = _dot(x, wgu_scr[...]) + bgu_ref[0]
        glu = jnp.minimum(gu[:, :D_FF], SWIGLU_LIMIT)
        lin = jnp.clip(gu[:, D_FF:], -SWIGLU_LIMIT, SWIGLU_LIMIT)
        act = glu * _sigmoid(SWIGLU_ALPHA * glu) * (lin + 1.0)
        y_ref[...] = _dot(act.astype(BF16), wd_scr[...]) + bd_ref[0]

    @pl.when(i >= nu_ref[0])
    def _():
        y_ref[...] = jnp.zeros(y_ref.shape, F32)


def _moe(block_expert, n_used, xb, wgu, bgu, wd, bd):
    n_rows, d = xb.shape
    nblk = n_rows // MOE_ROWS
    ne = wgu.shape[0]
    emap = lambda i, be, nu: (be[i], 0, 0)
    grid_spec = pltpu.PrefetchScalarGridSpec(
        num_scalar_prefetch=2,
        grid=(nblk,),
        in_specs=[pl.BlockSpec((MOE_ROWS, d), lambda i, be, nu: (i, 0)),
                  pl.BlockSpec((1, d, 2 * D_FF), emap),
                  pl.BlockSpec((1, 1, 2 * D_FF), emap),
                  pl.BlockSpec((1, D_FF, d), emap),
                  pl.BlockSpec((1, 1, d), emap)],
        out_specs=pl.BlockSpec((MOE_ROWS, d), lambda i, be, nu: (i, 0)),
        scratch_shapes=[pltpu.VMEM((d, 2 * D_FF), BF16), pltpu.VMEM((D_FF, d), BF16)])
    return pl.pallas_call(
        _moe_body,
        grid_spec=grid_spec,
        out_shape=jax.ShapeDtypeStruct((n_rows, d), F32),
        compiler_params=_cparams(("arbitrary",)),
        name="experts",
    )(block_expert, n_used, xb, wgu, bgu.reshape(ne, 1, 2 * D_FF), wd, bd.reshape(ne, 1, d))


def _combine_body(dest_ref, yb_ref, gate_ref, x1_ref, gt_ref, gf_ref, y_ref, buf, sem, *, tm, tok0):
    i = pl.program_id(0)

    def copy(r, k):
        src = dest_ref[(tok0 + i * tm + r) * TOP_K + k]
        return pltpu.make_async_copy(yb_ref.at[pl.ds(src, 1), :], buf.at[k, pl.ds(r, 1), :], sem)

    def start(r, carry):
        for k in range(TOP_K):
            copy(r, k).start()
        return carry

    def wait(r, carry):
        for k in range(TOP_K):
            copy(r, k).wait()
        return carry

    lax.fori_loop(0, tm, start, 0)
    lax.fori_loop(0, tm, wait, 0)
    g = gate_ref[...]
    y = g[:, 0:1] * buf[0]
    for k in range(1, TOP_K):
        y = y + g[:, k:k + 1] * buf[k]
    x2 = x1_ref[...] + gt_ref[0] * y
    ms = jnp.mean(x2 * x2, axis=-1, keepdims=True)
    y_ref[...] = x2 * lax.rsqrt(ms + EPS) * gf_ref[...]


def _combine(dest_flat, yb, gates, x1, gt, g_final, tm, rows_per_mod, tok0):
    r, d = x1.shape
    nt = r // tm
    tiles_per_mod = max(rows_per_mod // tm, 1)
    tile0 = tok0 // tm
    mod_block = (1,) + gt.shape[1:]
    grid_spec = pltpu.PrefetchScalarGridSpec(
        num_scalar_prefetch=1,
        grid=(nt,),
        in_specs=[pl.BlockSpec(memory_space=pl.ANY),
                  pl.BlockSpec((tm, LANES), lambda i, dr: (tile0 + i, 0)),
                  pl.BlockSpec((tm, d), lambda i, dr: (i, 0)),
                  pl.BlockSpec(mod_block, lambda i, dr: (i // tiles_per_mod, 0, 0)),
                  pl.BlockSpec((1, d), lambda i, dr: (0, 0))],
        out_specs=pl.BlockSpec((tm, d), lambda i, dr: (i, 0)),
        scratch_shapes=[pltpu.VMEM((TOP_K, tm, d), F32), pltpu.SemaphoreType.DMA(())])
    return pl.pallas_call(
        functools.partial(_combine_body, tm=tm, tok0=tok0),
        grid_spec=grid_spec,
        out_shape=jax.ShapeDtypeStruct((r, d), F32),
        compiler_params=_cparams(("arbitrary",)),
        name="combine",
    )(dest_flat, yb, gates, x1, gt, g_final)


def _rope_tables(pos):
    half = MOBA_HD // 2
    inv = 1.0 / (ROPE_THETA ** (jnp.arange(half, dtype=F32) / half))
    ang = pos.astype(F32)[:, None] * inv[None, :]
    cos = jnp.cos(ang)
    sin = jnp.sin(ang)
    cos_h = jnp.concatenate([cos, cos], axis=1)
    sin_h = jnp.concatenate([-sin, sin], axis=1)
    return jnp.tile(cos_h, (1, MOBA_HEADS)), jnp.tile(sin_h, (1, MOBA_HEADS))


def _row_tile(n, cap):
    t = min(n, cap)
    while n % t:
        t //= 2
    return t


def kernel(x_prompt, x_sample, cache_k, cache_v, state_gla, page_table, c_prompt, c_sample, w_ada, b_ada, g_norm1, w_in, w_gla_a2, b_gla_a2, g_gla_head, w_proj_gla, w_proj_moba, w_out, g_norm2, w_router, b_router, w_gate_up, b_gate_up, w_down, b_down, g_final):
    nb_p, seq, d = x_prompt.shape
    nb_s, dec = x_sample.shape[:2]
    past_len = page_table.shape[1] * cache_k.shape[2]
    assert w_ada.shape[0] == 1 and d == D_MODEL
    assert seq % MOBA_BLOCK == 0 and past_len % MOBA_BLOCK == 0 and dec <= MOBA_BLOCK and dec % 8 == 0
    r_p, r_s = nb_p * seq, nb_s * dec
    t_all = r_p + r_s

    c_all = jnp.concatenate([c_prompt, c_sample], axis=0)
    mod = _ada(c_all, w_ada[0], b_ada[0]).reshape(nb_p + nb_s, 6, d)
    mod_p = [mod[:nb_p, k].reshape(nb_p, 1, d) for k in range(6)]
    mod_s = [jnp.broadcast_to(mod[nb_p:, k][:, None, :], (nb_s, dec, d)).reshape(1, r_s, d) for k in range(6)]

    w = w_in[0]
    gkw = GLA_HEADS * GLA_DK
    gvw = GLA_HEADS * GLA_DV
    c_ga = 2 * gkw + 2 * gvw
    pad = jnp.zeros((d, PROJ_TILE - GLA_LOWRANK), w.dtype)
    w_in_p = jnp.concatenate([w[:, :c_ga + GLA_LOWRANK], pad, w[:, c_ga + GLA_LOWRANK:]], axis=1).astype(BF16)
    assert w_in_p.shape[1] == N_PROJ_TILES * PROJ_TILE
    w2_p = jnp.zeros((PROJ_TILE, gkw), BF16).at[:GLA_LOWRANK].set(w_gla_a2[0].astype(BF16))
    b2 = b_gla_a2[0].reshape(1, gkw)
    g1 = g_norm1[0].reshape(1, d)
    g2 = g_norm2[0].reshape(1, d)
    wpg = w_proj_gla[0].astype(BF16)
    wpm = w_proj_moba[0].astype(BF16)
    wo = w_out[0].astype(BF16)
    wr = jnp.zeros((d, LANES), F32).at[:, :N_EXPERTS].set(w_router[0])
    br = jnp.full((1, LANES), NEG_INF, F32).at[0, :N_EXPERTS].set(b_router[0])
    g_head = g_gla_head[0].reshape(1, GLA_DV)

    cos_p, sin_p = _rope_tables(jnp.arange(seq, dtype=I32))
    cos_s, sin_s = _rope_tables(past_len + jnp.arange(dec, dtype=I32))
    cos_s, sin_s = jnp.tile(cos_s, (nb_s, 1)), jnp.tile(sin_s, (nb_s, 1))

    xp2 = x_prompt.reshape(r_p, d)
    xs2 = x_sample.reshape(r_s, d)
    tm_p = _row_tile(seq, 1024)

    proj_p, kt_p, vt_p = _inproj(xp2, mod_p[1], mod_p[0], g1, w_in_p, w2_p, b2, cos_p, sin_p, tm_p, seq, True)
    s0_p = jnp.zeros((nb_p, GLA_HEADS, GLA_DK, GLA_DV), state_gla.dtype)
    og_p, st_p = _gla(proj_p, s0_p, g_head, nb_p, seq, _row_tile(seq, 64))
    om_p = _moba_prompt(proj_p, vt_p, nb_p, seq)
    tm_post = _row_tile(seq, 512)
    x1_p, h2_p, lg_p = _post(og_p, om_p, proj_p, xp2, mod_p[2], mod_p[4], mod_p[3], g2, wpg, wpm, wo, wr, br,
                             tm_post, seq)

    proj_s, k5_s, v5_s = _inproj(xs2, mod_s[1], mod_s[0], g1, w_in_p, w2_p, b2, cos_s, sin_s, r_s, r_s, False)
    og_s, st_s = _gla(proj_s, state_gla[0], g_head, nb_s, dec, dec)
    om_s = _moba_sample(proj_s, cache_k[0], cache_v[0], page_table, nb_s, dec)
    x1_s, h2_s, lg_s = _post(og_s, om_s, proj_s, xs2, mod_s[2], mod_s[4], mod_s[3], g2, wpg, wpm, wo, wr, br,
                             r_s, r_s)

    h_all = jnp.concatenate([h2_p, h2_s], axis=0)
    lg_all = jnp.concatenate([lg_p, lg_s], axis=0)
    tm_r = _row_tile(t_all, 256)
    eidx, rank, gates, cnt = _route(lg_all, tm_r)
    counts = cnt[0, :N_EXPERTS].astype(I32)
    nblk_e = (counts + MOE_ROWS - 1) // MOE_ROWS
    blk_end = jnp.cumsum(nblk_e)
    blk_start = blk_end - nblk_e
    dest = blk_start[eidx[:, :TOP_K]] * MOE_ROWS + rank[:, :TOP_K]
    dest_flat = dest.reshape(-1).astype(I32)
    n_blocks = -(-(t_all * TOP_K) // MOE_ROWS) + N_EXPERTS
    blk_ids = jnp.arange(n_blocks, dtype=I32)
    block_expert = jnp.minimum(jnp.sum((blk_end[None, :] <= blk_ids[:, None]).astype(I32), axis=1), N_EXPERTS - 1)
    n_used = blk_end[-1:].astype(I32)
    xb = _dispatch(dest_flat, h_all, n_blocks * MOE_ROWS, tm_r)
    yb = _moe(block_expert, n_used, xb, w_gate_up[0], b_gate_up[0], w_down[0], b_down[0])
    gf = g_final.reshape(1, d)
    tm_c = _row_tile(seq, 256)
    y_p = _combine(dest_flat, yb, gates, x1_p, mod_p[5], gf, tm_c, seq, 0)
    y_s = _combine(dest_flat, yb, gates, x1_s, mod_s[5], gf, r_s, r_s, r_p)

    to_out = lambda a: a.reshape(1, nb_p, MOBA_HEADS, MOBA_HD, seq).transpose(0, 1, 4, 2, 3)
    k_p = to_out(kt_p)
    v_p = to_out(vt_p)
    k_s = k5_s.reshape(1, nb_s, dec, MOBA_HEADS, MOBA_HD)
    v_s = v5_s.reshape(1, nb_s, dec, MOBA_HEADS, MOBA_HD)
    return (y_p.reshape(nb_p, seq, d), y_s.reshape(nb_s, dec, d), k_p, v_p, st_p[None],
            k_s, v_s, st_s[None])
```

```python
import functools

import jax
import jax.numpy as jnp
from jax import lax
from jax.experimental import pallas as pl
from jax.experimental.pallas import tpu as pltpu

F32 = jnp.float32
BF16 = jnp.bfloat16
I32 = jnp.int32

D_MODEL = 1024
GLA_HEADS = 4
GLA_DK = 128
GLA_DV = 256
GLA_LOWRANK = 16
GLA_GATE_NORM = 16.0
GLA_SUB = 16
MOBA_HEADS = 8
MOBA_HD = 64
MOBA_BLOCK = 256
MOBA_TOPK = 3
MOBA_W = MOBA_HEADS * MOBA_HD
ROPE_THETA = 10000.0
N_EXPERTS = 32
TOP_K = 4
D_FF = D_MODEL
SWIGLU_LIMIT = 7.0
SWIGLU_ALPHA = 1.702
EPS = 1e-6
LANES = 128
PROJ_TILE = 512

T_GQ, T_GK, T_GV, T_GR, T_GA, T_MQ, T_MK, T_MV, T_BA, T_BB = 0, 1, 2, 4, 6, 7, 8, 9, 10, 12
N_PROJ_TILES = 14
MOE_ROWS = 256
MOBA_S_BLOCKS_PER_STEP = 4
VMEM_LIMIT = 56 * 1024 * 1024

NEG_INF = float("-inf")


def _sigmoid(x):
    return 1.0 / (1.0 + jnp.exp(-x))


def _split3(x):
    hi = x.astype(BF16)
    r = x - hi.astype(F32)
    mid = r.astype(BF16)
    lo = (r - mid.astype(F32)).astype(BF16)
    return hi, mid, lo


def _dot(a, b):
    return jnp.dot(a, b, preferred_element_type=F32)


def _dot_nt(a, b):
    return lax.dot_general(a, b, (((1,), (1,)), ((), ())), preferred_element_type=F32)


def _dot_nt_f32(a, b):
    ah = a.astype(BF16)
    al = (a - ah.astype(F32)).astype(BF16)
    bh = b.astype(BF16)
    bl = (b - bh.astype(F32)).astype(BF16)
    return _dot_nt(ah, bh) + _dot_nt(ah, bl) + _dot_nt(al, bh)


def _div(x, n):
    assert n & (n - 1) == 0
    return lax.shift_right_logical(x, n.bit_length() - 1)


def _mod(x, n):
    assert n & (n - 1) == 0
    return x & (n - 1)


def _cparams(sem):
    return pltpu.CompilerParams(dimension_semantics=sem, vmem_limit_bytes=VMEM_LIMIT)


def _ada_body(c_ref, w_ref, b_ref, o_ref):
    c = c_ref[...]
    s = c * _sigmoid(c)
    o_ref[...] = _dot(s.astype(BF16), w_ref[...].astype(BF16)) + b_ref[...]


def _ada(c, w, b):
    n = w.shape[1]
    tn = n // 4
    return pl.pallas_call(
        _ada_body,
        grid=(4,),
        in_specs=[pl.BlockSpec(c.shape, lambda j: (0, 0)),
                  pl.BlockSpec((w.shape[0], tn), lambda j: (0, j)),
                  pl.BlockSpec((1, tn), lambda j: (0, j))],
        out_specs=pl.BlockSpec((c.shape[0], tn), lambda j: (0, j)),
        out_shape=jax.ShapeDtypeStruct((c.shape[0], n), F32),
        compiler_params=_cparams(("arbitrary",)),
        name="ada",
    )(c, w, b.reshape(1, n))


def _rotary(x, cos, sin_signed):
    lane = lax.broadcasted_iota(I32, x.shape, 1)
    half = MOBA_HD // 2
    partner = jnp.where((lane & (MOBA_HD - 1)) < half, lane + half, lane - half)
    w = x.shape[1]
    r1 = pltpu.roll(x, half, 1)
    i1 = pltpu.roll(lane, half, 1)
    r2 = pltpu.roll(x, w - half, 1)
    swapped = jnp.where(i1 == partner, r1, r2)
    return x * cos + swapped * sin_signed


def _log_sigmoid(x):
    return jnp.minimum(x, 0.0) - jnp.log(1.0 + jnp.exp(-jnp.abs(x)))


def _store_heads(ref, val, token_minor):
    if token_minor:
        ref[0] = val.T
        return
    tm = val.shape[0]
    for h in range(MOBA_HEADS):
        ref[pl.ds(h, tm, stride=MOBA_HEADS), :] = val[:, h * MOBA_HD:(h + 1) * MOBA_HD]


def _inproj_body(x_ref, sc_ref, sh_ref, g_ref, w_ref, w2_ref, b2_ref, cos_ref, sin_ref, o_ref, k5_ref, v5_ref,
                 h_scr, *, token_minor):
    j = pl.program_id(1)

    @pl.when(j == 0)
    def _():
        x = x_ref[...]
        ms = jnp.mean(x * x, axis=-1, keepdims=True)
        y = x * lax.rsqrt(ms + EPS) * g_ref[...]
        h = y * (1.0 + sc_ref[0]) + sh_ref[0]
        h_scr[...] = h.astype(BF16)

    acc = _dot(h_scr[...], w_ref[...])
    is_la = j == T_GA
    special = functools.reduce(jnp.logical_or, [j == T_MQ, j == T_MK, j == T_MV, is_la])

    @pl.when(j == T_MQ)
    def _():
        o_ref[0] = _rotary(acc, cos_ref[...], sin_ref[...])

    @pl.when(j == T_MK)
    def _():
        rot = _rotary(acc, cos_ref[...], sin_ref[...])
        o_ref[0] = rot
        _store_heads(k5_ref, rot, token_minor)

    @pl.when(j == T_MV)
    def _():
        o_ref[0] = acc
        _store_heads(v5_ref, acc, token_minor)

    @pl.when(is_la)
    def _():
        z = _dot(acc.astype(BF16), w2_ref[...]) + b2_ref[...]
        o_ref[0] = _log_sigmoid(z) * (1.0 / GLA_GATE_NORM)

    @pl.when(jnp.logical_not(special))
    def _():
        o_ref[0] = acc


def _inproj(x2, sc, sh, g1, w_in_p, w2_p, b2, cos, sin, tm, rows_per_mod, token_minor):
    r, d = x2.shape
    nt = r // tm
    tiles_per_mod = max(rows_per_mod // tm, 1)
    tiles_per_tab = cos.shape[0] // tm
    mod_block = (1,) + sc.shape[1:]
    if token_minor:
        seq = cos.shape[0]
        kv_spec = pl.BlockSpec((1, MOBA_W, tm), lambda i, j: (i // tiles_per_tab, 0, i % tiles_per_tab))
        kv_shape = jax.ShapeDtypeStruct((r // seq, MOBA_W, seq), F32)
    else:
        kv_spec = pl.BlockSpec((tm * MOBA_HEADS, MOBA_HD), lambda i, j: (i, 0))
        kv_shape = jax.ShapeDtypeStruct((r * MOBA_HEADS, MOBA_HD), F32)
    return pl.pallas_call(
        functools.partial(_inproj_body, token_minor=token_minor),
        grid=(nt, N_PROJ_TILES),
        in_specs=[pl.BlockSpec((tm, d), lambda i, j: (i, 0)),
                  pl.BlockSpec(mod_block, lambda i, j: (i // tiles_per_mod, 0, 0)),
                  pl.BlockSpec(mod_block, lambda i, j: (i // tiles_per_mod, 0, 0)),
                  pl.BlockSpec((1, d), lambda i, j: (0, 0)),
                  pl.BlockSpec((d, PROJ_TILE), lambda i, j: (0, j)),
                  pl.BlockSpec((PROJ_TILE, PROJ_TILE), lambda i, j: (0, 0)),
                  pl.BlockSpec((1, PROJ_TILE), lambda i, j: (0, 0)),
                  pl.BlockSpec((tm, PROJ_TILE), lambda i, j: (i % tiles_per_tab, 0)),
                  pl.BlockSpec((tm, PROJ_TILE), lambda i, j: (i % tiles_per_tab, 0))],
        out_specs=[pl.BlockSpec((1, tm, PROJ_TILE), lambda i, j: (j, i, 0)), kv_spec, kv_spec],
        out_shape=[jax.ShapeDtypeStruct((N_PROJ_TILES, r, PROJ_TILE), F32), kv_shape, kv_shape],
        scratch_shapes=[pltpu.VMEM((tm, d), BF16)],
        compiler_params=_cparams(("arbitrary", "arbitrary")),
        name="inproj",
    )(x2, sc, sh, g1, w_in_p, w2_p, b2, cos, sin)


def _gla_body(q_ref, k_ref, v_ref, la_ref, gr_ref, s0_ref, gh_ref, og_ref, sout_ref, st_scr, *, C, SB):
    c = pl.program_id(1)
    nc = pl.num_programs(1)

    @pl.when(c == 0)
    def _():
        for h in range(GLA_HEADS):
            st_scr[h] = s0_ref[0, h].T

    la = la_ref[0]
    row_c = lax.broadcasted_iota(I32, (C, C), 0)
    col_c = lax.broadcasted_iota(I32, (C, C), 1)
    tri = jnp.where(row_c >= col_c, 1.0, 0.0).astype(BF16)
    hi, mid, lo = _split3(la)
    b_all = _dot(tri, hi) + _dot(tri, mid) + _dot(tri, lo)
    q_all = q_ref[0] * (GLA_DK ** -0.5)
    k_all = k_ref[0]
    gh = gh_ref[...]
    row_k = lax.broadcasted_iota(I32, (C, GLA_DK), 0)
    row_sb = lax.broadcasted_iota(I32, (SB, 1), 0)
    lane_sb = lax.broadcasted_iota(I32, (SB, C), 1)

    for h in range(GLA_HEADS):
        ks = slice(h * GLA_DK, (h + 1) * GLA_DK)
        vs = slice((h % 2) * GLA_DV, (h % 2 + 1) * GLA_DV)
        b = b_all[:, ks]
        q = q_all[:, ks]
        k = k_all[:, ks]
        v = v_ref[h // 2][:, vs]
        st = st_scr[h]
        b_last = b[C - 1:C, :]
        o = _dot_nt((q * jnp.exp(b)).astype(BF16), st.astype(BF16))
        att_rows = []
        for i in range(C // SB):
            r0 = i * SB
            bi = b[r0:r0 + SB]
            qi = q[r0:r0 + SB]
            ki = k[r0:r0 + SB]
            if i > 0:
                bref = b[r0 - 1:r0, :]
                qt = qi * jnp.exp(bi - bref)
                kt = jnp.where(row_k < r0, k * jnp.exp(jnp.minimum(bref - b, 0.0)), 0.0)
                att_i = _dot_nt(qt.astype(BF16), kt.astype(BF16))
            else:
                att_i = jnp.zeros((SB, C), F32)
            for s in range(SB):
                dec = jnp.exp(jnp.minimum(bi - bi[s:s + 1, :], 0.0))
                col = jnp.sum(qi * ki[s:s + 1, :] * dec, axis=1, keepdims=True)
                col = jnp.where(row_sb >= s, col, 0.0)
                att_i = att_i + jnp.where(lane_sb == r0 + s, col, 0.0)
            att_rows.append(att_i)
        att = att_rows[0] if len(att_rows) == 1 else jnp.concatenate(att_rows, axis=0)
        vb = v.astype(BF16)
        o = o + _dot(att.astype(BF16), vb)
        kd = k * jnp.exp(b_last - b)
        st_scr[h] = st * jnp.exp(b_last) + _dot(v.T.astype(BF16), kd.astype(BF16))
        on = o * lax.rsqrt(jnp.mean(o * o, axis=-1, keepdims=True) + EPS) * gh
        gr = gr_ref[h // 2][:, vs]
        og_ref[:, h * GLA_DV:(h + 1) * GLA_DV] = (on * (gr * _sigmoid(gr))).astype(og_ref.dtype)

    @pl.when(c == nc - 1)
    def _():
        for h in range(GLA_HEADS):
            sout_ref[0, h] = st_scr[h].T


def _gla(proj3, s0, g_head, n_seq, seq_len, chunk):
    r = proj3.shape[1]
    nc = seq_len // chunk
    sb = min(GLA_SUB, chunk)
    out_dtype = BF16 if chunk % 16 == 0 else F32
    row = lambda b, c: b * nc + c
    body = functools.partial(_gla_body, C=chunk, SB=sb)
    return pl.pallas_call(
        body,
        grid=(n_seq, nc),
        in_specs=[pl.BlockSpec((1, chunk, PROJ_TILE), lambda b, c: (T_GQ, row(b, c), 0)),
                  pl.BlockSpec((1, chunk, PROJ_TILE), lambda b, c: (T_GK, row(b, c), 0)),
                  pl.BlockSpec((2, chunk, PROJ_TILE), lambda b, c: (T_GV // 2, row(b, c), 0)),
                  pl.BlockSpec((1, chunk, PROJ_TILE), lambda b, c: (T_GA, row(b, c), 0)),
                  pl.BlockSpec((2, chunk, PROJ_TILE), lambda b, c: (T_GR // 2, row(b, c), 0)),
                  pl.BlockSpec((1, GLA_HEADS, GLA_DK, GLA_DV), lambda b, c: (b, 0, 0, 0)),
                  pl.BlockSpec((1, GLA_DV), lambda b, c: (0, 0))],
        out_specs=[pl.BlockSpec((chunk, GLA_HEADS * GLA_DV), lambda b, c: (row(b, c), 0)),
                   pl.BlockSpec((1, GLA_HEADS, GLA_DK, GLA_DV), lambda b, c: (b, 0, 0, 0))],
        out_shape=[jax.ShapeDtypeStruct((r, GLA_HEADS * GLA_DV), out_dtype),
                   jax.ShapeDtypeStruct((n_seq, GLA_HEADS, GLA_DK, GLA_DV), F32)],
        scratch_shapes=[pltpu.VMEM((GLA_HEADS, GLA_DV, GLA_DK), F32)],
        compiler_params=_cparams(("arbitrary", "arbitrary")),
        name="gla",
    )(proj3, proj3, proj3, proj3, proj3, s0, g_head)


def _select_topk(g, n_valid_rows, n_rows, n_sel):
    row = lax.broadcasted_iota(I32, g.shape, 0)
    valid = row < n_valid_rows
    gm = jnp.where(valid, g, NEG_INF)
    rank = jnp.zeros(g.shape, F32)
    for jp in range(n_rows):
        gj = gm[jp:jp + 1, :]
        ahead = jnp.logical_or(gj > gm, jnp.logical_and(gj == gm, jp < row))
        rank = rank + jnp.where(ahead, 1.0, 0.0)
    return jnp.where(jnp.logical_and(valid, rank < n_sel), 1.0, 0.0)


def _moba_p_body(q_ref, k_ref, v_ref, o_ref, k_scr, vt_scr, km_scr, sel_scr, m_scr, l_scr, acc_scr, *, nb, nbp):
    i = pl.program_id(1)
    blk = MOBA_BLOCK
    scale = MOBA_HD ** -0.5

    @pl.when(i == 0)
    def _():
        km_scr[...] = jnp.zeros(km_scr.shape, F32)
        for j in range(nb):
            kj = k_ref[0, j * blk:(j + 1) * blk, :]
            k_scr[j] = kj.astype(BF16)
            km_scr[j:j + 1, :] = jnp.mean(kj, axis=0, keepdims=True)
            vt_scr[j] = v_ref[0, :, j * blk:(j + 1) * blk].astype(BF16)

    q = q_ref[0]
    km = km_scr[...]
    km_rep = jnp.concatenate([km] * MOBA_HEADS, axis=0)
    wrow = lax.broadcasted_iota(I32, km_rep.shape, 0)
    wlane = lax.broadcasted_iota(I32, km_rep.shape, 1)
    wt = jnp.where(_div(wlane, MOBA_HD) == _div(wrow, nbp), km_rep, 0.0)
    gates_t = _dot_nt_f32(wt, q)
    for h in range(MOBA_HEADS):
        sel_scr[h] = _select_topk(gates_t[h * nbp:(h + 1) * nbp, :], i, nb, MOBA_TOPK)

    n_pair = MOBA_HEADS // 2
    pair_lane = lax.broadcasted_iota(I32, (blk, LANES), 1)
    key_row = lax.broadcasted_iota(I32, (blk, 2 * blk), 0)
    q_col = _mod(lax.broadcasted_iota(I32, (blk, 2 * blk), 1), blk)
    qms = []
    for pr in range(n_pair):
        qp = q[:, pr * LANES:(pr + 1) * LANES]
        qms.append(jnp.concatenate([jnp.where(_div(pair_lane, MOBA_HD) == u, qp, 0.0) for u in range(2)],
                                   axis=0).astype(BF16))

    def scores(j, pr):
        kj = k_scr[j, :, pr * LANES:(pr + 1) * LANES]
        return _dot_nt(kj, qms[pr]) * scale

    def values(j, h):
        return vt_scr[j, h * MOBA_HD:(h + 1) * MOBA_HD, :]

    for pr in range(n_pair):
        s = jnp.where(key_row <= q_col, scores(i, pr), NEG_INF)
        m = jnp.max(s, axis=0, keepdims=True)
        p = jnp.exp(s - m)
        m_scr[pr] = m
        l_scr[pr] = jnp.sum(p, axis=0, keepdims=True)
        pb = p.astype(BF16)
        for u in range(2):
            acc_scr[2 * pr + u] = _dot(values(i, 2 * pr + u), pb[:, u * blk:(u + 1) * blk])

    def step(j, carry):
        for pr in range(n_pair):
            keep = jnp.concatenate([sel_scr[2 * pr + u, pl.ds(j, 1), :] for u in range(2)], axis=1)
            s = jnp.where(keep > 0.5, scores(j, pr), NEG_INF)
            m_old = m_scr[pr]
            m_new = jnp.maximum(m_old, jnp.max(s, axis=0, keepdims=True))
            alpha = jnp.exp(m_old - m_new)
            p = jnp.exp(s - m_new)
            m_scr[pr] = m_new
            l_scr[pr] = l_scr[pr] * alpha + jnp.sum(p, axis=0, keepdims=True)
            pb = p.astype(BF16)
            for u in range(2):
                h = 2 * pr + u
                cs = slice(u * blk, (u + 1) * blk)
                acc_scr[h] = acc_scr[h] * alpha[:, cs] + _dot(values(j, h), pb[:, cs])
        return carry

    lax.fori_loop(0, i, step, 0)
    outs = []
    for h in range(MOBA_HEADS):
        l = l_scr[h // 2][:, (h % 2) * blk:(h % 2 + 1) * blk]
        outs.append(acc_scr[h] / l)
    out_t = jnp.concatenate(outs, axis=0)
    o_ref[...] = out_t.T.astype(o_ref.dtype)


def _moba_prompt(proj3, v_t, n_seq, seq_len):
    r = proj3.shape[1]
    nb = seq_len // MOBA_BLOCK
    nbp = -(-nb // 8) * 8
    body = functools.partial(_moba_p_body, nb=nb, nbp=nbp)
    return pl.pallas_call(
        body,
        grid=(n_seq, nb),
        in_specs=[pl.BlockSpec((1, MOBA_BLOCK, MOBA_W), lambda b, i: (T_MQ, b * nb + i, 0)),
                  pl.BlockSpec((1, seq_len, MOBA_W), lambda b, i: (T_MK, b, 0)),
                  pl.BlockSpec((1, MOBA_W, seq_len), lambda b, i: (b, 0, 0))],
        out_specs=pl.BlockSpec((MOBA_BLOCK, MOBA_W), lambda b, i: (b * nb + i, 0)),
        out_shape=jax.ShapeDtypeStruct((r, MOBA_W), BF16),
        scratch_shapes=[pltpu.VMEM((nb, MOBA_BLOCK, MOBA_W), BF16),
                        pltpu.VMEM((nb, MOBA_W, MOBA_BLOCK), BF16),
                        pltpu.VMEM((nbp, MOBA_W), F32),
                        pltpu.VMEM((MOBA_HEADS, nbp, MOBA_BLOCK), F32),
                        pltpu.VMEM((MOBA_HEADS // 2, 1, 2 * MOBA_BLOCK), F32),
                        pltpu.VMEM((MOBA_HEADS // 2, 1, 2 * MOBA_BLOCK), F32),
                        pltpu.VMEM((MOBA_HEADS, MOBA_HD, MOBA_BLOCK), F32)],
        compiler_params=_cparams(("arbitrary", "arbitrary")),
        name="moba_prompt",
    )(proj3, proj3, v_t)


def _moba_s_body(pt_ref, *refs, nbp, L, bps):
    del pt_ref
    page_refs = refs[:4 * bps]
    qn_ref, kn_ref, vn_ref, o_ref, q2_scr, gate_scr, m_scr, l_scr, acc_scr = refs[4 * bps:]
    j = pl.program_id(1)
    n_steps = nbp // bps
    scale = MOBA_HD ** -0.5
    n_row = MOBA_HEADS * L
    blk_lane = lax.broadcasted_iota(I32, (n_row, LANES), 1)

    @pl.when(j == 0)
    def _():
        qn = qn_ref[...]
        q2 = jnp.concatenate([qn] * MOBA_HEADS, axis=0)
        row = lax.broadcasted_iota(I32, q2.shape, 0)
        lane = lax.broadcasted_iota(I32, q2.shape, 1)
        q2_scr[...] = jnp.where(_div(lane, MOBA_HD) == _div(row, L), q2, 0.0).astype(BF16)
        gate_scr[...] = jnp.full(gate_scr.shape, NEG_INF, F32)
        m_scr[...] = jnp.zeros(m_scr.shape, F32)
        l_scr[...] = jnp.zeros(l_scr.shape, F32)

    def softmax_partial(s):
        m = jnp.max(s, axis=1, keepdims=True)
        p = jnp.exp(s - m)
        return m, jnp.sum(p, axis=1, keepdims=True), p.astype(BF16)

    @pl.when(j < n_steps)
    def _():
        for t in range(bps):
            ka_ref, kb_ref, va_ref, vb_ref = page_refs[4 * t:4 * t + 4]
            jj = j * bps + t
            kt = jnp.concatenate([ka_ref[0], kb_ref[0]], axis=1).astype(BF16)
            vt = jnp.concatenate([va_ref[0], vb_ref[0]], axis=1).astype(BF16)
            s = _dot(q2_scr[...], kt)
            gate = jnp.mean(s, axis=1, keepdims=True)
            m, l, p = softmax_partial(s * scale)
            hot = blk_lane == jj
            gate_scr[...] = jnp.where(hot, gate, gate_scr[...])
            m_scr[...] = jnp.where(hot, m, m_scr[...])
            l_scr[...] = jnp.where(hot, l, l_scr[...])
            acc_scr[jj] = _dot_nt(p, vt)

    @pl.when(j == n_steps)
    def _():
        pad = jnp.zeros((LANES - L, MOBA_W), F32)
        kn = jnp.concatenate([kn_ref[...], pad], axis=0).astype(BF16)
        vn = jnp.concatenate([vn_ref[...], pad], axis=0).astype(BF16)
        s = _dot_nt(q2_scr[...], kn) * scale
        key = lax.broadcasted_iota(I32, s.shape, 1)
        qpos = _mod(lax.broadcasted_iota(I32, s.shape, 0), L)
        m_own, l_own, p_own = softmax_partial(jnp.where(key <= qpos, s, NEG_INF))
        acc_own = _dot(p_own, vn)
        g = gate_scr[...]
        rank = jnp.zeros(g.shape, F32)
        for jp in range(nbp):
            gj = g[:, jp:jp + 1]
            ahead = jnp.logical_or(gj > g, jnp.logical_and(gj == g, jp < blk_lane))
            rank = rank + jnp.where(ahead, 1.0, 0.0)
        sel = jnp.logical_and(blk_lane < nbp, rank < min(MOBA_TOPK, nbp + 1))
        m_all = m_scr[...]
        m_top = jnp.maximum(m_own, jnp.max(jnp.where(sel, m_all, NEG_INF), axis=1, keepdims=True))
        w = jnp.where(sel, jnp.exp(m_all - m_top), 0.0)
        w_own = jnp.exp(m_own - m_top)
        den = jnp.sum(w * l_scr[...], axis=1, keepdims=True) + w_own * l_own
        num = w_own * acc_own
        for jj in range(nbp):
            num = num + w[:, jj:jj + 1] * acc_scr[jj]
        out_r = num / den
        lane = lax.broadcasted_iota(I32, (L, MOBA_W), 1)
        out = jnp.zeros((L, MOBA_W), F32)
        for h in range(MOBA_HEADS):
            out = out + jnp.where(_div(lane, MOBA_HD) == h, out_r[h * L:(h + 1) * L, :], 0.0)
        o_ref[...] = out


def _moba_sample(proj3, cache_k, cache_v, page_table, n_seq, L):
    n_pool, page = cache_k.shape[0], cache_k.shape[1]
    n_pages = page_table.shape[1]
    ppb = MOBA_BLOCK // page
    nbp = n_pages // ppb
    n_row = MOBA_HEADS * L
    assert ppb == 2 and n_pages % ppb == 0 and page == LANES and nbp <= LANES and n_row % 16 == 0 and L <= LANES
    ck = jnp.transpose(cache_k, (0, 2, 3, 1)).reshape(n_pool, MOBA_W, page)
    cv = jnp.transpose(cache_v, (0, 2, 3, 1)).reshape(n_pool, MOBA_W, page)
    pt = page_table.reshape(-1).astype(I32)

    bps = MOBA_S_BLOCKS_PER_STEP if nbp % MOBA_S_BLOCKS_PER_STEP == 0 else 1
    n_steps = nbp // bps

    def page_map(t, off):
        return lambda b, j, pt_ref: (
            pt_ref[b * n_pages + ppb * (jnp.minimum(j, n_steps - 1) * bps + t) + off], 0, 0)

    new_map = lambda t: (lambda b, j, pt_ref: (t, b, 0))
    body = functools.partial(_moba_s_body, nbp=nbp, L=L, bps=bps)
    page_specs = [pl.BlockSpec((1, MOBA_W, page), page_map(t, off)) for t in range(bps) for off in (0, 1, 0, 1)]
    page_args = [a for _ in range(bps) for a in (ck, ck, cv, cv)]
    grid_spec = pltpu.PrefetchScalarGridSpec(
        num_scalar_prefetch=1,
        grid=(n_seq, n_steps + 1),
        in_specs=page_specs + [
                  pl.BlockSpec((None, L, MOBA_W), new_map(T_MQ)),
                  pl.BlockSpec((None, L, MOBA_W), new_map(T_MK)),
                  pl.BlockSpec((None, L, MOBA_W), new_map(T_MV))],
        out_specs=pl.BlockSpec((L, MOBA_W), lambda b, j, pt_ref: (b, 0)),
        scratch_shapes=[pltpu.VMEM((n_row, MOBA_W), BF16),
                        pltpu.VMEM((n_row, LANES), F32),
                        pltpu.VMEM((n_row, LANES), F32),
                        pltpu.VMEM((n_row, LANES), F32),
                        pltpu.VMEM((nbp, n_row, MOBA_W), F32)])
    return pl.pallas_call(
        body,
        grid_spec=grid_spec,
        out_shape=jax.ShapeDtypeStruct((n_seq * L, MOBA_W), F32),
        compiler_params=_cparams(("arbitrary", "arbitrary")),
        name="moba_sample",
    )(pt, *page_args, proj3, proj3, proj3)


def _post_body(og_ref, om_ref, ba_ref, bb_ref, x_ref, gt_ref, sc_ref, sh_ref, g2_ref, wpg_ref, wpm_ref,
               wo_ref, wr_ref, br_ref, x1_ref, h2_ref, lg_ref):
    ya = _dot(og_ref[...].astype(BF16), wpg_ref[...])
    yb = _dot(om_ref[...].astype(BF16), wpm_ref[...])
    ba = jnp.concatenate([ba_ref[0], ba_ref[1]], axis=1)
    bb = jnp.concatenate([bb_ref[0], bb_ref[1]], axis=1)
    mix = _sigmoid(ba) * ya + _sigmoid(bb) * yb
    x1 = x_ref[...] + gt_ref[0] * _dot(mix.astype(BF16), wo_ref[...])
    x1_ref[...] = x1
    ms = jnp.mean(x1 * x1, axis=-1, keepdims=True)
    h2 = x1 * lax.rsqrt(ms + EPS) * g2_ref[...] * (1.0 + sc_ref[0]) + sh_ref[0]
    h2_ref[...] = h2
    hh = h2.astype(BF16)
    hl = (h2 - hh.astype(F32)).astype(BF16)
    wr = wr_ref[...]
    wh = wr.astype(BF16)
    wl = (wr - wh.astype(F32)).astype(BF16)
    lg_ref[...] = _dot(hh, wh) + _dot(hh, wl) + _dot(hl, wh) + br_ref[...]


def _post(og, om, proj3, x2, gt, sc, sh, g2, wpg, wpm, wo, wr, br, tm, rows_per_mod):
    r, d = x2.shape
    nt = r // tm
    tiles_per_mod = max(rows_per_mod // tm, 1)
    mod_block = (1,) + gt.shape[1:]
    mod_map = lambda i: (i // tiles_per_mod, 0, 0)
    full = lambda a: pl.BlockSpec(a.shape, lambda i: (0,) * a.ndim)
    return pl.pallas_call(
        _post_body,
        grid=(nt,),
        in_specs=[pl.BlockSpec((tm, og.shape[1]), lambda i: (i, 0)),
                  pl.BlockSpec((tm, om.shape[1]), lambda i: (i, 0)),
                  pl.BlockSpec((2, tm, PROJ_TILE), lambda i: (T_BA // 2, i, 0)),
                  pl.BlockSpec((2, tm, PROJ_TILE), lambda i: (T_BB // 2, i, 0)),
                  pl.BlockSpec((tm, d), lambda i: (i, 0)),
                  pl.BlockSpec(mod_block, mod_map),
                  pl.BlockSpec(mod_block, mod_map),
                  pl.BlockSpec(mod_block, mod_map),
                  full(g2), full(wpg), full(wpm), full(wo), full(wr), full(br)],
        out_specs=[pl.BlockSpec((tm, d), lambda i: (i, 0)),
                   pl.BlockSpec((tm, d), lambda i: (i, 0)),
                   pl.BlockSpec((tm, LANES), lambda i: (i, 0))],
        out_shape=[jax.ShapeDtypeStruct((r, d), F32),
                   jax.ShapeDtypeStruct((r, d), F32),
                   jax.ShapeDtypeStruct((r, LANES), F32)],
        compiler_params=_cparams(("arbitrary",)),
        name="post",
    )(og, om, proj3, proj3, x2, gt, sc, sh, g2, wpg, wpm, wo, wr, br)


def _route_body(lg_ref, eidx_ref, rank_ref, gate_ref, cnt_ref, run_scr):
    i = pl.program_id(0)
    tm = lg_ref.shape[0]

    @pl.when(i == 0)
    def _():
        run_scr[...] = jnp.zeros(run_scr.shape, F32)

    l = lg_ref[...]
    lane = lax.broadcasted_iota(I32, l.shape, 1)
    vals, hots = [], []
    for _ in range(TOP_K):
        m = jnp.max(l, axis=1, keepdims=True)
        idx = jnp.min(jnp.where(l == m, lane, LANES), axis=1, keepdims=True)
        hot = lane == idx
        vals.append(m)
        hots.append(hot)
        l = jnp.where(hot, NEG_INF, l)
    es = [jnp.exp(v - vals[0]) for v in vals]
    den = es[0] + es[1] + es[2] + es[3]
    chosen = jnp.zeros(l.shape, F32)
    for hot in hots:
        chosen = chosen + jnp.where(hot, 1.0, 0.0)
    row = lax.broadcasted_iota(I32, (tm, tm), 0)
    col = lax.broadcasted_iota(I32, (tm, tm), 1)
    before = jnp.where(row > col, 1.0, 0.0).astype(BF16)
    pos = _dot(before, chosen.astype(BF16)) + run_scr[...]
    eidx = jnp.zeros(l.shape, I32)
    rank = jnp.zeros(l.shape, I32)
    gate = jnp.zeros(l.shape, F32)
    for k in range(TOP_K):
        e_k = jnp.min(jnp.where(hots[k], lane, LANES), axis=1, keepdims=True)
        r_k = jnp.sum(jnp.where(hots[k], pos, 0.0), axis=1, keepdims=True).astype(I32)
        eidx = jnp.where(lane == k, e_k, eidx)
        rank = jnp.where(lane == k, r_k, rank)
        gate = jnp.where(lane == k, es[k] / den, gate)
    eidx_ref[...] = eidx
    rank_ref[...] = rank
    gate_ref[...] = gate
    run_scr[...] = run_scr[...] + jnp.sum(chosen, axis=0, keepdims=True)
    cnt_ref[...] = jnp.broadcast_to(run_scr[...], cnt_ref.shape)


def _route(logits, tm):
    t = logits.shape[0]
    blk = pl.BlockSpec((tm, LANES), lambda i: (i, 0))
    return pl.pallas_call(
        _route_body,
        grid=(t // tm,),
        in_specs=[blk],
        out_specs=[blk, blk, blk, pl.BlockSpec((8, LANES), lambda i: (0, 0))],
        out_shape=[jax.ShapeDtypeStruct((t, LANES), I32),
                   jax.ShapeDtypeStruct((t, LANES), I32),
                   jax.ShapeDtypeStruct((t, LANES), F32),
                   jax.ShapeDtypeStruct((8, LANES), F32)],
        scratch_shapes=[pltpu.VMEM((1, LANES), F32)],
        compiler_params=_cparams(("arbitrary",)),
        name="route",
    )(logits)


def _moe_body(be_ref, nu_ref, tok_ref, h_ref, wgu_ref, bgu_ref, wd_ref, bd_ref, y_ref, xbuf, x16, wgu_scr, wd_scr,
              sem):
    i = pl.program_id(0)
    n_used = nu_ref[0]
    slot = i & 1
    prev = be_ref[jnp.maximum(i - 1, 0)]
    fresh = jnp.logical_or(i == 0, be_ref[i] != prev)

    def row_copy(blk, r, s):
        tok = tok_ref[blk * MOE_ROWS + r]
        return pltpu.make_async_copy(h_ref.at[pl.ds(tok, 1), :], xbuf.at[s, pl.ds(r, 1), :], sem.at[s])

    def start_gather(blk, s):
        for r in range(MOE_ROWS):
            row_copy(blk, r, s).start()

    def compute():
        gu = _dot(x16[...], wgu_scr[...]) + bgu_ref[0]
        glu = jnp.minimum(gu[:, :D_FF], SWIGLU_LIMIT)
        lin = jnp.clip(gu[:, D_FF:], -SWIGLU_LIMIT, SWIGLU_LIMIT)
        act = glu * _sigmoid(SWIGLU_ALPHA * glu) * (lin + 1.0)
        y_ref[...] = _dot(act.astype(BF16), wd_scr[...]) + bd_ref[0]

    @pl.when(jnp.logical_and(i == 0, n_used > 0))
    def _():
        start_gather(0, 0)

    @pl.when(i < n_used)
    def _():
        for r in range(MOE_ROWS):
            row_copy(i, r, slot).wait()

    @pl.when(jnp.logical_and(fresh, i < n_used))
    def _():
        wgu_scr[...] = wgu_ref[0].astype(BF16)
        wd_scr[...] = wd_ref[0].astype(BF16)

    @pl.when(i < n_used)
    def _():
        x16[...] = xbuf[slot].astype(BF16)

    @pl.when(i + 1 < n_used)
    def _():
        start_gather(i + 1, 1 - slot)
        compute()

    @pl.when(i + 1 == n_used)
    def _():
        compute()

    @pl.when(i >= n_used)
    def _():
        y_ref[...] = jnp.zeros(y_ref.shape, F32)


def _moe(block_expert, n_used, row_tok, h_all, wgu, bgu, wd, bd):
    d = h_all.shape[1]
    n_rows = row_tok.shape[0]
    nblk = n_rows // MOE_ROWS
    ne = wgu.shape[0]
    emap = lambda i, be, nu, rt: (be[i], 0, 0)
    grid_spec = pltpu.PrefetchScalarGridSpec(
        num_scalar_prefetch=3,
        grid=(nblk,),
        in_specs=[pl.BlockSpec(memory_space=pl.ANY),
                  pl.BlockSpec((1, d, 2 * D_FF), emap),
                  pl.BlockSpec((1, 1, 2 * D_FF), emap),
                  pl.BlockSpec((1, D_FF, d), emap),
                  pl.BlockSpec((1, 1, d), emap)],
        out_specs=pl.BlockSpec((MOE_ROWS, d), lambda i, be, nu, rt: (i, 0)),
        scratch_shapes=[pltpu.VMEM((2, MOE_ROWS, d), F32), pltpu.VMEM((MOE_ROWS, d), BF16),
                        pltpu.VMEM((d, 2 * D_FF), BF16), pltpu.VMEM((D_FF, d), BF16),
                        pltpu.SemaphoreType.DMA((2,))])
    return pl.pallas_call(
        _moe_body,
        grid_spec=grid_spec,
        out_shape=jax.ShapeDtypeStruct((n_rows, d), F32),
        compiler_params=_cparams(("arbitrary",)),
        name="experts",
    )(block_expert, n_used, row_tok, h_all, wgu, bgu.reshape(ne, 1, 2 * D_FF), wd, bd.reshape(ne, 1, d))


def _combine_body(dest_ref, yb_ref, gate_ref, x1_ref, gt_ref, gf_ref, y_ref, buf, sem, *, tm, tok0):
    i = pl.program_id(0)

    def copy(r, k):
        src = dest_ref[(tok0 + i * tm + r) * TOP_K + k]
        return pltpu.make_async_copy(yb_ref.at[pl.ds(src, 1), :], buf.at[k, pl.ds(r, 1), :], sem)

    def start(r, carry):
        for k in range(TOP_K):
            copy(r, k).start()
        return carry

    def wait(r, carry):
        for k in range(TOP_K):
            copy(r, k).wait()
        return carry

    lax.fori_loop(0, tm, start, 0)
    lax.fori_loop(0, tm, wait, 0)
    g = gate_ref[...]
    y = g[:, 0:1] * buf[0]
    for k in range(1, TOP_K):
        y = y + g[:, k:k + 1] * buf[k]
    x2 = x1_ref[...] + gt_ref[0] * y
    ms = jnp.mean(x2 * x2, axis=-1, keepdims=True)
    y_ref[...] = x2 * lax.rsqrt(ms + EPS) * gf_ref[...]


def _combine(dest_flat, yb, gates, x1, gt, g_final, tm, rows_per_mod, tok0):
    r, d = x1.shape
    nt = r // tm
    tiles_per_mod = max(rows_per_mod // tm, 1)
    tile0 = tok0 // tm
    mod_block = (1,) + gt.shape[1:]
    grid_spec = pltpu.PrefetchScalarGridSpec(
        num_scalar_prefetch=1,
        grid=(nt,),
        in_specs=[pl.BlockSpec(memory_space=pl.ANY),
                  pl.BlockSpec((tm, LANES), lambda i, dr: (tile0 + i, 0)),
                  pl.BlockSpec((tm, d), lambda i, dr: (i, 0)),
                  pl.BlockSpec(mod_block, lambda i, dr: (i // tiles_per_mod, 0, 0)),
                  pl.BlockSpec((1, d), lambda i, dr: (0, 0))],
        out_specs=pl.BlockSpec((tm, d), lambda i, dr: (i, 0)),
        scratch_shapes=[pltpu.VMEM((TOP_K, tm, d), F32), pltpu.SemaphoreType.DMA(())])
    return pl.pallas_call(
        functools.partial(_combine_body, tm=tm, tok0=tok0),
        grid_spec=grid_spec,
        out_shape=jax.ShapeDtypeStruct((r, d), F32),
        compiler_params=_cparams(("arbitrary",)),
        name="combine",
    )(dest_flat, yb, gates, x1, gt, g_final)


def _rope_tables(pos):
    half = MOBA_HD // 2
    inv = 1.0 / (ROPE_THETA ** (jnp.arange(half, dtype=F32) / half))
    ang = pos.astype(F32)[:, None] * inv[None, :]
    cos = jnp.cos(ang)
    sin = jnp.sin(ang)
    cos_h = jnp.concatenate([cos, cos], axis=1)
    sin_h = jnp.concatenate([-sin, sin], axis=1)
    return jnp.tile(cos_h, (1, MOBA_HEADS)), jnp.tile(sin_h, (1, MOBA_HEADS))


def _row_tile(n, cap):
    t = min(n, cap)
    while n % t:
        t //= 2
    return t


def kernel(x_prompt, x_sample, cache_k, cache_v, state_gla, page_table, c_prompt, c_sample, w_ada, b_ada, g_norm1, w_in, w_gla_a2, b_gla_a2, g_gla_head, w_proj_gla, w_proj_moba, w_out, g_norm2, w_router, b_router, w_gate_up, b_gate_up, w_down, b_down, g_final):
    nb_p, seq, d = x_prompt.shape
    nb_s, dec = x_sample.shape[:2]
    past_len = page_table.shape[1] * cache_k.shape[2]
    assert w_ada.shape[0] == 1 and d == D_MODEL
    assert seq % MOBA_BLOCK == 0 and past_len % MOBA_BLOCK == 0 and dec <= MOBA_BLOCK and dec % 8 == 0
    r_p, r_s = nb_p * seq, nb_s * dec
    t_all = r_p + r_s

    c_all = jnp.concatenate([c_prompt, c_sample], axis=0)
    mod = _ada(c_all, w_ada[0], b_ada[0]).reshape(nb_p + nb_s, 6, d)
    mod_p = [mod[:nb_p, k].reshape(nb_p, 1, d) for k in range(6)]
    mod_s = [jnp.broadcast_to(mod[nb_p:, k][:, None, :], (nb_s, dec, d)).reshape(1, r_s, d) for k in range(6)]

    w = w_in[0]
    gkw = GLA_HEADS * GLA_DK
    gvw = GLA_HEADS * GLA_DV
    c_ga = 2 * gkw + 2 * gvw
    pad = jnp.zeros((d, PROJ_TILE - GLA_LOWRANK), w.dtype)
    w_in_p = jnp.concatenate([w[:, :c_ga + GLA_LOWRANK], pad, w[:, c_ga + GLA_LOWRANK:]], axis=1).astype(BF16)
    assert w_in_p.shape[1] == N_PROJ_TILES * PROJ_TILE
    w2_p = jnp.zeros((PROJ_TILE, gkw), BF16).at[:GLA_LOWRANK].set(w_gla_a2[0].astype(BF16))
    b2 = b_gla_a2[0].reshape(1, gkw)
    g1 = g_norm1[0].reshape(1, d)
    g2 = g_norm2[0].reshape(1, d)
    wpg = w_proj_gla[0].astype(BF16)
    wpm = w_proj_moba[0].astype(BF16)
    wo = w_out[0].astype(BF16)
    wr = jnp.zeros((d, LANES), F32).at[:, :N_EXPERTS].set(w_router[0])
    br = jnp.full((1, LANES), NEG_INF, F32).at[0, :N_EXPERTS].set(b_router[0])
    g_head = g_gla_head[0].reshape(1, GLA_DV)

    cos_p, sin_p = _rope_tables(jnp.arange(seq, dtype=I32))
    cos_s, sin_s = _rope_tables(past_len + jnp.arange(dec, dtype=I32))
    cos_s, sin_s = jnp.tile(cos_s, (nb_s, 1)), jnp.tile(sin_s, (nb_s, 1))

    xp2 = x_prompt.reshape(r_p, d)
    xs2 = x_sample.reshape(r_s, d)
    tm_p = _row_tile(seq, 1024)

    proj_p, kt_p, vt_p = _inproj(xp2, mod_p[1], mod_p[0], g1, w_in_p, w2_p, b2, cos_p, sin_p, tm_p, seq, True)
    s0_p = jnp.zeros((nb_p, GLA_HEADS, GLA_DK, GLA_DV), state_gla.dtype)
    og_p, st_p = _gla(proj_p, s0_p, g_head, nb_p, seq, _row_tile(seq, 64))
    om_p = _moba_prompt(proj_p, vt_p, nb_p, seq)
    tm_post = _row_tile(seq, 512)
    x1_p, h2_p, lg_p = _post(og_p, om_p, proj_p, xp2, mod_p[2], mod_p[4], mod_p[3], g2, wpg, wpm, wo, wr, br,
                             tm_post, seq)

    proj_s, k5_s, v5_s = _inproj(xs2, mod_s[1], mod_s[0], g1, w_in_p, w2_p, b2, cos_s, sin_s, r_s, r_s, False)
    og_s, st_s = _gla(proj_s, state_gla[0], g_head, nb_s, dec, dec)
    om_s = _moba_sample(proj_s, cache_k[0], cache_v[0], page_table, nb_s, dec)
    x1_s, h2_s, lg_s = _post(og_s, om_s, proj_s, xs2, mod_s[2], mod_s[4], mod_s[3], g2, wpg, wpm, wo, wr, br,
                             r_s, r_s)

    h_all = jnp.concatenate([h2_p, h2_s], axis=0)
    lg_all = jnp.concatenate([lg_p, lg_s], axis=0)
    tm_r = _row_tile(t_all, 256)
    eidx, rank, gates, cnt = _route(lg_all, tm_r)
    counts = cnt[0, :N_EXPERTS].astype(I32)
    nblk_e = (counts + MOE_ROWS - 1) // MOE_ROWS
    blk_end = jnp.cumsum(nblk_e)
    blk_start = blk_end - nblk_e
    dest = blk_start[eidx[:, :TOP_K]] * MOE_ROWS + rank[:, :TOP_K]
    dest_flat = dest.reshape(-1).astype(I32)
    n_blocks = -(-(t_all * TOP_K) // MOE_ROWS) + N_EXPERTS
    blk_ids = jnp.arange(n_blocks, dtype=I32)
    block_expert = jnp.minimum(jnp.sum((blk_end[None, :] <= blk_ids[:, None]).astype(I32), axis=1), N_EXPERTS - 1)
    n_used = blk_end[-1:].astype(I32)
    row_tok = jnp.zeros((n_blocks * MOE_ROWS,), I32).at[dest_flat].set(
        jnp.arange(t_all * TOP_K, dtype=I32) // TOP_K, unique_indices=True)
    yb = _moe(block_expert, n_used, row_tok, h_all, w_gate_up[0], b_gate_up[0], w_down[0], b_down[0])
    gf = g_final.reshape(1, d)
    tm_c = _row_tile(seq, 256)
    y_p = _combine(dest_flat, yb, gates, x1_p, mod_p[5], gf, tm_c, seq, 0)
    y_s = _combine(dest_flat, yb, gates, x1_s, mod_s[5], gf, r_s, r_s, r_p)

    to_out = lambda a: a.reshape(1, nb_p, MOBA_HEADS, MOBA_HD, seq).transpose(0, 1, 4, 2, 3)
    k_p = to_out(kt_p)
    v_p = to_out(vt_p)
    k_s = k5_s.reshape(1, nb_s, dec, MOBA_HEADS, MOBA_HD)
    v_s = v5_s.reshape(1, nb_s, dec, MOBA_HEADS, MOBA_HD)
    return (y_p.reshape(nb_p, seq, d), y_s.reshape(nb_s, dec, d), k_p, v_p, st_p[None],
            k_s, v_s, st_s[None])
```

```python
import functools

import jax
import jax.numpy as jnp
from jax import lax
from jax.experimental import pallas as pl
from jax.experimental.pallas import tpu as pltpu

F32 = jnp.float32
BF16 = jnp.bfloat16
I32 = jnp.int32

D_MODEL = 1024
GLA_HEADS = 4
GLA_DK = 128
GLA_DV = 256
GLA_LOWRANK = 16
GLA_GATE_NORM = 16.0
GLA_SUB = 16
MOBA_HEADS = 8
MOBA_HD = 64
MOBA_BLOCK = 256
MOBA_TOPK = 3
MOBA_W = MOBA_HEADS * MOBA_HD
ROPE_THETA = 10000.0
N_EXPERTS = 32
TOP_K = 4
D_FF = D_MODEL
SWIGLU_LIMIT = 7.0
SWIGLU_ALPHA = 1.702
EPS = 1e-6
LANES = 128
PROJ_TILE = 512

T_GQ, T_GK, T_GV, T_GR, T_GA, T_MQ, T_MK, T_MV, T_BA, T_BB = 0, 1, 2, 4, 6, 7, 8, 9, 10, 12
N_PROJ_TILES = 14
MOE_ROWS = 512
MOBA_S_BLOCKS_PER_STEP = 8
VMEM_LIMIT = 56 * 1024 * 1024

NEG_INF = float("-inf")


def _sigmoid(x):
    return 1.0 / (1.0 + jnp.exp(-x))


def _split3(x):
    hi = x.astype(BF16)
    r = x - hi.astype(F32)
    mid = r.astype(BF16)
    lo = (r - mid.astype(F32)).astype(BF16)
    return hi, mid, lo


def _dot(a, b):
    return jnp.dot(a, b, preferred_element_type=F32)


def _dot_nt(a, b):
    return lax.dot_general(a, b, (((1,), (1,)), ((), ())), preferred_element_type=F32)


def _dot_nt_f32(a, b):
    ah = a.astype(BF16)
    al = (a - ah.astype(F32)).astype(BF16)
    bh = b.astype(BF16)
    bl = (b - bh.astype(F32)).astype(BF16)
    return _dot_nt(ah, bh) + _dot_nt(ah, bl) + _dot_nt(al, bh)


def _div(x, n):
    assert n & (n - 1) == 0
    return lax.shift_right_logical(x, n.bit_length() - 1)


def _mod(x, n):
    assert n & (n - 1) == 0
    return x & (n - 1)


def _cparams(sem):
    return pltpu.CompilerParams(dimension_semantics=sem, vmem_limit_bytes=VMEM_LIMIT)


def _ada_body(c_ref, w_ref, b_ref, o_ref):
    c = c_ref[...]
    s = c * _sigmoid(c)
    o_ref[...] = _dot(s.astype(BF16), w_ref[...].astype(BF16)) + b_ref[...]


def _ada(c, w, b):
    n = w.shape[1]
    tn = n // 4
    return pl.pallas_call(
        _ada_body,
        grid=(4,),
        in_specs=[pl.BlockSpec(c.shape, lambda j: (0, 0)),
                  pl.BlockSpec((w.shape[0], tn), lambda j: (0, j)),
                  pl.BlockSpec((1, tn), lambda j: (0, j))],
        out_specs=pl.BlockSpec((c.shape[0], tn), lambda j: (0, j)),
        out_shape=jax.ShapeDtypeStruct((c.shape[0], n), F32),
        compiler_params=_cparams(("arbitrary",)),
        name="ada",
    )(c, w, b.reshape(1, n))


def _rotary(x, cos, sin_signed):
    lane = lax.broadcasted_iota(I32, x.shape, 1)
    half = MOBA_HD // 2
    partner = jnp.where((lane & (MOBA_HD - 1)) < half, lane + half, lane - half)
    w = x.shape[1]
    r1 = pltpu.roll(x, half, 1)
    i1 = pltpu.roll(lane, half, 1)
    r2 = pltpu.roll(x, w - half, 1)
    swapped = jnp.where(i1 == partner, r1, r2)
    return x * cos + swapped * sin_signed


def _log_sigmoid(x):
    return jnp.minimum(x, 0.0) - jnp.log(1.0 + jnp.exp(-jnp.abs(x)))


def _store_heads(ref, val, token_minor):
    if token_minor:
        ref[0] = val.T
        return
    tm = val.shape[0]
    for h in range(MOBA_HEADS):
        ref[pl.ds(h, tm, stride=MOBA_HEADS), :] = val[:, h * MOBA_HD:(h + 1) * MOBA_HD]


def _inproj_body(x_ref, sc_ref, sh_ref, g_ref, w_ref, w2_ref, b2_ref, cos_ref, sin_ref, o_ref, k5_ref, v5_ref,
                 h_scr, *, token_minor):
    j = pl.program_id(1)

    @pl.when(j == 0)
    def _():
        x = x_ref[...]
        ms = jnp.mean(x * x, axis=-1, keepdims=True)
        y = x * lax.rsqrt(ms + EPS) * g_ref[...]
        h = y * (1.0 + sc_ref[0]) + sh_ref[0]
        h_scr[...] = h.astype(BF16)

    acc = _dot(h_scr[...], w_ref[...])
    is_la = j == T_GA
    special = functools.reduce(jnp.logical_or, [j == T_MQ, j == T_MK, j == T_MV, is_la])

    @pl.when(j == T_MQ)
    def _():
        o_ref[0] = _rotary(acc, cos_ref[...], sin_ref[...])

    @pl.when(j == T_MK)
    def _():
        rot = _rotary(acc, cos_ref[...], sin_ref[...])
        o_ref[0] = rot
        _store_heads(k5_ref, rot, token_minor)

    @pl.when(j == T_MV)
    def _():
        o_ref[0] = acc
        _store_heads(v5_ref, acc, token_minor)

    @pl.when(is_la)
    def _():
        z = _dot(acc.astype(BF16), w2_ref[...]) + b2_ref[...]
        o_ref[0] = _log_sigmoid(z) * (1.0 / GLA_GATE_NORM)

    @pl.when(jnp.logical_not(special))
    def _():
        o_ref[0] = acc


def _inproj(x2, sc, sh, g1, w_in_p, w2_p, b2, cos, sin, tm, rows_per_mod, token_minor):
    r, d = x2.shape
    nt = r // tm
    tiles_per_mod = max(rows_per_mod // tm, 1)
    tiles_per_tab = cos.shape[0] // tm
    mod_block = (1,) + sc.shape[1:]
    if token_minor:
        seq = cos.shape[0]
        kv_spec = pl.BlockSpec((1, MOBA_W, tm), lambda i, j: (i // tiles_per_tab, 0, i % tiles_per_tab))
        kv_shape = jax.ShapeDtypeStruct((r // seq, MOBA_W, seq), F32)
    else:
        kv_spec = pl.BlockSpec((tm * MOBA_HEADS, MOBA_HD), lambda i, j: (i, 0))
        kv_shape = jax.ShapeDtypeStruct((r * MOBA_HEADS, MOBA_HD), F32)
    return pl.pallas_call(
        functools.partial(_inproj_body, token_minor=token_minor),
        grid=(nt, N_PROJ_TILES),
        in_specs=[pl.BlockSpec((tm, d), lambda i, j: (i, 0)),
                  pl.BlockSpec(mod_block, lambda i, j: (i // tiles_per_mod, 0, 0)),
                  pl.BlockSpec(mod_block, lambda i, j: (i // tiles_per_mod, 0, 0)),
                  pl.BlockSpec((1, d), lambda i, j: (0, 0)),
                  pl.BlockSpec((d, PROJ_TILE), lambda i, j: (0, j)),
                  pl.BlockSpec((PROJ_TILE, PROJ_TILE), lambda i, j: (0, 0)),
                  pl.BlockSpec((1, PROJ_TILE), lambda i, j: (0, 0)),
                  pl.BlockSpec((tm, PROJ_TILE), lambda i, j: (i % tiles_per_tab, 0)),
                  pl.BlockSpec((tm, PROJ_TILE), lambda i, j: (i % tiles_per_tab, 0))],
        out_specs=[pl.BlockSpec((1, tm, PROJ_TILE), lambda i, j: (j, i, 0)), kv_spec, kv_spec],
        out_shape=[jax.ShapeDtypeStruct((N_PROJ_TILES, r, PROJ_TILE), F32), kv_shape, kv_shape],
        scratch_shapes=[pltpu.VMEM((tm, d), BF16)],
        compiler_params=_cparams(("arbitrary", "arbitrary")),
        name="inproj",
    )(x2, sc, sh, g1, w_in_p, w2_p, b2, cos, sin)


def _gla_body(q_ref, k_ref, v_ref, la_ref, gr_ref, s0_ref, gh_ref, og_ref, sout_ref, st_scr, *, C, SB):
    c = pl.program_id(1)
    nc = pl.num_programs(1)

    @pl.when(c == 0)
    def _():
        for h in range(GLA_HEADS):
            st_scr[h] = s0_ref[0, h].T

    la = la_ref[0]
    row_c = lax.broadcasted_iota(I32, (C, C), 0)
    col_c = lax.broadcasted_iota(I32, (C, C), 1)
    tri = jnp.where(row_c >= col_c, 1.0, 0.0).astype(BF16)
    hi, mid, lo = _split3(la)
    b_all = _dot(tri, hi) + _dot(tri, mid) + _dot(tri, lo)
    q_all = q_ref[0] * (GLA_DK ** -0.5)
    k_all = k_ref[0]
    gh = gh_ref[...]
    row_k = lax.broadcasted_iota(I32, (C, GLA_DK), 0)
    row_sb = lax.broadcasted_iota(I32, (SB, 1), 0)
    lane_sb = lax.broadcasted_iota(I32, (SB, C), 1)

    for h in range(GLA_HEADS):
        ks = slice(h * GLA_DK, (h + 1) * GLA_DK)
        vs = slice((h % 2) * GLA_DV, (h % 2 + 1) * GLA_DV)
        b = b_all[:, ks]
        q = q_all[:, ks]
        k = k_all[:, ks]
        v = v_ref[h // 2][:, vs]
        st = st_scr[h]
        b_last = b[C - 1:C, :]
        o = _dot_nt((q * jnp.exp(b)).astype(BF16), st.astype(BF16))
        att_rows = []
        for i in range(C // SB):
            r0 = i * SB
            bi = b[r0:r0 + SB]
            qi = q[r0:r0 + SB]
            ki = k[r0:r0 + SB]
            if i > 0:
                bref = b[r0 - 1:r0, :]
                qt = qi * jnp.exp(bi - bref)
                kt = jnp.where(row_k < r0, k * jnp.exp(jnp.minimum(bref - b, 0.0)), 0.0)
                att_i = _dot_nt(qt.astype(BF16), kt.astype(BF16))
            else:
                att_i = jnp.zeros((SB, C), F32)
            for s in range(SB):
                dec = jnp.exp(jnp.minimum(bi - bi[s:s + 1, :], 0.0))
                col = jnp.sum(qi * ki[s:s + 1, :] * dec, axis=1, keepdims=True)
                col = jnp.where(row_sb >= s, col, 0.0)
                att_i = att_i + jnp.where(lane_sb == r0 + s, col, 0.0)
            att_rows.append(att_i)
        att = att_rows[0] if len(att_rows) == 1 else jnp.concatenate(att_rows, axis=0)
        vb = v.astype(BF16)
        o = o + _dot(att.astype(BF16), vb)
        kd = k * jnp.exp(b_last - b)
        st_scr[h] = st * jnp.exp(b_last) + _dot(v.T.astype(BF16), kd.astype(BF16))
        on = o * lax.rsqrt(jnp.mean(o * o, axis=-1, keepdims=True) + EPS) * gh
        gr = gr_ref[h // 2][:, vs]
        og_ref[:, h * GLA_DV:(h + 1) * GLA_DV] = (on * (gr * _sigmoid(gr))).astype(og_ref.dtype)

    @pl.when(c == nc - 1)
    def _():
        for h in range(GLA_HEADS):
            sout_ref[0, h] = st_scr[h].T


def _gla(proj3, s0, g_head, n_seq, seq_len, chunk):
    r = proj3.shape[1]
    nc = seq_len // chunk
    sb = min(GLA_SUB, chunk)
    out_dtype = BF16 if chunk % 16 == 0 else F32
    row = lambda b, c: b * nc + c
    body = functools.partial(_gla_body, C=chunk, SB=sb)
    return pl.pallas_call(
        body,
        grid=(n_seq, nc),
        in_specs=[pl.BlockSpec((1, chunk, PROJ_TILE), lambda b, c: (T_GQ, row(b, c), 0)),
                  pl.BlockSpec((1, chunk, PROJ_TILE), lambda b, c: (T_GK, row(b, c), 0)),
                  pl.BlockSpec((2, chunk, PROJ_TILE), lambda b, c: (T_GV // 2, row(b, c), 0)),
                  pl.BlockSpec((1, chunk, PROJ_TILE), lambda b, c: (T_GA, row(b, c), 0)),
                  pl.BlockSpec((2, chunk, PROJ_TILE), lambda b, c: (T_GR // 2, row(b, c), 0)),
                  pl.BlockSpec((1, GLA_HEADS, GLA_DK, GLA_DV), lambda b, c: (b, 0, 0, 0)),
                  pl.BlockSpec((1, GLA_DV), lambda b, c: (0, 0))],
        out_specs=[pl.BlockSpec((chunk, GLA_HEADS * GLA_DV), lambda b, c: (row(b, c), 0)),
                   pl.BlockSpec((1, GLA_HEADS, GLA_DK, GLA_DV), lambda b, c: (b, 0, 0, 0))],
        out_shape=[jax.ShapeDtypeStruct((r, GLA_HEADS * GLA_DV), out_dtype),
                   jax.ShapeDtypeStruct((n_seq, GLA_HEADS, GLA_DK, GLA_DV), F32)],
        scratch_shapes=[pltpu.VMEM((GLA_HEADS, GLA_DV, GLA_DK), F32)],
        compiler_params=_cparams(("arbitrary", "arbitrary")),
        name="gla",
    )(proj3, proj3, proj3, proj3, proj3, s0, g_head)


def _select_topk(g, n_valid_rows, n_rows, n_sel):
    row = lax.broadcasted_iota(I32, g.shape, 0)
    valid = row < n_valid_rows
    gm = jnp.where(valid, g, NEG_INF)
    rank = jnp.zeros(g.shape, F32)
    for jp in range(n_rows):
        gj = gm[jp:jp + 1, :]
        ahead = jnp.logical_or(gj > gm, jnp.logical_and(gj == gm, jp < row))
        rank = rank + jnp.where(ahead, 1.0, 0.0)
    return jnp.where(jnp.logical_and(valid, rank < n_sel), 1.0, 0.0)


def _moba_p_body(q_ref, k_ref, v_ref, o_ref, k_scr, vt_scr, km_scr, sel_scr, m_scr, l_scr, acc_scr, *, nb, nbp):
    i = pl.program_id(1)
    blk = MOBA_BLOCK
    scale = MOBA_HD ** -0.5

    @pl.when(i == 0)
    def _():
        km_scr[...] = jnp.zeros(km_scr.shape, F32)
        for j in range(nb):
            kj = k_ref[0, j * blk:(j + 1) * blk, :]
            k_scr[j] = kj.astype(BF16)
            km_scr[j:j + 1, :] = jnp.mean(kj, axis=0, keepdims=True)
            vt_scr[j] = v_ref[0, :, j * blk:(j + 1) * blk].astype(BF16)

    q = q_ref[0]
    km = km_scr[...]
    km_rep = jnp.concatenate([km] * MOBA_HEADS, axis=0)
    wrow = lax.broadcasted_iota(I32, km_rep.shape, 0)
    wlane = lax.broadcasted_iota(I32, km_rep.shape, 1)
    wt = jnp.where(_div(wlane, MOBA_HD) == _div(wrow, nbp), km_rep, 0.0)
    gates_t = _dot_nt_f32(wt, q)
    for h in range(MOBA_HEADS):
        sel_scr[h] = _select_topk(gates_t[h * nbp:(h + 1) * nbp, :], i, nb, MOBA_TOPK)

    n_pair = MOBA_HEADS // 2
    pair_lane = lax.broadcasted_iota(I32, (blk, LANES), 1)
    key_row = lax.broadcasted_iota(I32, (blk, 2 * blk), 0)
    q_col = _mod(lax.broadcasted_iota(I32, (blk, 2 * blk), 1), blk)
    qms = []
    for pr in range(n_pair):
        qp = q[:, pr * LANES:(pr + 1) * LANES]
        qms.append(jnp.concatenate([jnp.where(_div(pair_lane, MOBA_HD) == u, qp, 0.0) for u in range(2)],
                                   axis=0).astype(BF16))

    def scores(j, pr):
        kj = k_scr[j, :, pr * LANES:(pr + 1) * LANES]
        return _dot_nt(kj, qms[pr]) * scale

    def values(j, h):
        return vt_scr[j, h * MOBA_HD:(h + 1) * MOBA_HD, :]

    for pr in range(n_pair):
        s = jnp.where(key_row <= q_col, scores(i, pr), NEG_INF)
        m = jnp.max(s, axis=0, keepdims=True)
        p = jnp.exp(s - m)
        m_scr[pr] = m
        l_scr[pr] = jnp.sum(p, axis=0, keepdims=True)
        pb = p.astype(BF16)
        for u in range(2):
            acc_scr[2 * pr + u] = _dot(values(i, 2 * pr + u), pb[:, u * blk:(u + 1) * blk])

    def step(j, carry):
        for pr in range(n_pair):
            keep = jnp.concatenate([sel_scr[2 * pr + u, pl.ds(j, 1), :] for u in range(2)], axis=1)
            s = jnp.where(keep > 0.5, scores(j, pr), NEG_INF)
            m_old = m_scr[pr]
            m_new = jnp.maximum(m_old, jnp.max(s, axis=0, keepdims=True))
            alpha = jnp.exp(m_old - m_new)
            p = jnp.exp(s - m_new)
            m_scr[pr] = m_new
            l_scr[pr] = l_scr[pr] * alpha + jnp.sum(p, axis=0, keepdims=True)
            pb = p.astype(BF16)
            for u in range(2):
                h = 2 * pr + u
                cs = slice(u * blk, (u + 1) * blk)
                acc_scr[h] = acc_scr[h] * alpha[:, cs] + _dot(values(j, h), pb[:, cs])
        return carry

    lax.fori_loop(0, i, step, 0)
    outs = []
    for h in range(MOBA_HEADS):
        l = l_scr[h // 2][:, (h % 2) * blk:(h % 2 + 1) * blk]
        outs.append(acc_scr[h] / l)
    out_t = jnp.concatenate(outs, axis=0)
    o_ref[...] = out_t.T.astype(o_ref.dtype)


def _moba_prompt(proj3, v_t, n_seq, seq_len):
    r = proj3.shape[1]
    nb = seq_len // MOBA_BLOCK
    nbp = -(-nb // 8) * 8
    body = functools.partial(_moba_p_body, nb=nb, nbp=nbp)
    return pl.pallas_call(
        body,
        grid=(n_seq, nb),
        in_specs=[pl.BlockSpec((1, MOBA_BLOCK, MOBA_W), lambda b, i: (T_MQ, b * nb + i, 0)),
                  pl.BlockSpec((1, seq_len, MOBA_W), lambda b, i: (T_MK, b, 0)),
                  pl.BlockSpec((1, MOBA_W, seq_len), lambda b, i: (b, 0, 0))],
        out_specs=pl.BlockSpec((MOBA_BLOCK, MOBA_W), lambda b, i: (b * nb + i, 0)),
        out_shape=jax.ShapeDtypeStruct((r, MOBA_W), BF16),
        scratch_shapes=[pltpu.VMEM((nb, MOBA_BLOCK, MOBA_W), BF16),
                        pltpu.VMEM((nb, MOBA_W, MOBA_BLOCK), BF16),
                        pltpu.VMEM((nbp, MOBA_W), F32),
                        pltpu.VMEM((MOBA_HEADS, nbp, MOBA_BLOCK), F32),
                        pltpu.VMEM((MOBA_HEADS // 2, 1, 2 * MOBA_BLOCK), F32),
                        pltpu.VMEM((MOBA_HEADS // 2, 1, 2 * MOBA_BLOCK), F32),
                        pltpu.VMEM((MOBA_HEADS, MOBA_HD, MOBA_BLOCK), F32)],
        compiler_params=_cparams(("arbitrary", "arbitrary")),
        name="moba_prompt",
    )(proj3, proj3, v_t)


def _moba_s_body(pt_ref, *refs, nbp, L, bps):
    del pt_ref
    page_refs = refs[:4 * bps]
    qn_ref, kn_ref, vn_ref, o_ref, q2_scr, gate_scr, m_scr, l_scr, acc_scr = refs[4 * bps:]
    j = pl.program_id(1)
    n_steps = nbp // bps
    scale = MOBA_HD ** -0.5
    n_row = MOBA_HEADS * L
    blk_lane = lax.broadcasted_iota(I32, (n_row, LANES), 1)

    @pl.when(j == 0)
    def _():
        qn = qn_ref[...]
        q2 = jnp.concatenate([qn] * MOBA_HEADS, axis=0)
        row = lax.broadcasted_iota(I32, q2.shape, 0)
        lane = lax.broadcasted_iota(I32, q2.shape, 1)
        q2_scr[...] = jnp.where(_div(lane, MOBA_HD) == _div(row, L), q2, 0.0).astype(BF16)
        gate_scr[...] = jnp.full(gate_scr.shape, NEG_INF, F32)
        m_scr[...] = jnp.zeros(m_scr.shape, F32)
        l_scr[...] = jnp.zeros(l_scr.shape, F32)

    def softmax_partial(s):
        m = jnp.max(s, axis=1, keepdims=True)
        p = jnp.exp(s - m)
        return m, jnp.sum(p, axis=1, keepdims=True), p.astype(BF16)

    @pl.when(j < n_steps)
    def _():
        for t in range(bps):
            ka_ref, kb_ref, va_ref, vb_ref = page_refs[4 * t:4 * t + 4]
            jj = j * bps + t
            kt = jnp.concatenate([ka_ref[0], kb_ref[0]], axis=1).astype(BF16)
            vt = jnp.concatenate([va_ref[0], vb_ref[0]], axis=1).astype(BF16)
            s = _dot(q2_scr[...], kt)
            gate = jnp.mean(s, axis=1, keepdims=True)
            m, l, p = softmax_partial(s * scale)
            hot = blk_lane == jj
            gate_scr[...] = jnp.where(hot, gate, gate_scr[...])
            m_scr[...] = jnp.where(hot, m, m_scr[...])
            l_scr[...] = jnp.where(hot, l, l_scr[...])
            acc_scr[jj] = _dot_nt(p, vt)

    @pl.when(j == n_steps)
    def _():
        pad = jnp.zeros((LANES - L, MOBA_W), F32)
        kn = jnp.concatenate([kn_ref[...], pad], axis=0).astype(BF16)
        vn = jnp.concatenate([vn_ref[...], pad], axis=0).astype(BF16)
        s = _dot_nt(q2_scr[...], kn) * scale
        key = lax.broadcasted_iota(I32, s.shape, 1)
        qpos = _mod(lax.broadcasted_iota(I32, s.shape, 0), L)
        m_own, l_own, p_own = softmax_partial(jnp.where(key <= qpos, s, NEG_INF))
        acc_own = _dot(p_own, vn)
        g = gate_scr[...]
        rank = jnp.zeros(g.shape, F32)
        for jp in range(nbp):
            gj = g[:, jp:jp + 1]
            ahead = jnp.logical_or(gj > g, jnp.logical_and(gj == g, jp < blk_lane))
            rank = rank + jnp.where(ahead, 1.0, 0.0)
        sel = jnp.logical_and(blk_lane < nbp, rank < min(MOBA_TOPK, nbp + 1))
        m_all = m_scr[...]
        m_top = jnp.maximum(m_own, jnp.max(jnp.where(sel, m_all, NEG_INF), axis=1, keepdims=True))
        w = jnp.where(sel, jnp.exp(m_all - m_top), 0.0)
        w_own = jnp.exp(m_own - m_top)
        den = jnp.sum(w * l_scr[...], axis=1, keepdims=True) + w_own * l_own
        num = w_own * acc_own
        for jj in range(nbp):
            num = num + w[:, jj:jj + 1] * acc_scr[jj]
        out_r = num / den
        lane = lax.broadcasted_iota(I32, (L, MOBA_W), 1)
        out = jnp.zeros((L, MOBA_W), F32)
        for h in range(MOBA_HEADS):
            out = out + jnp.where(_div(lane, MOBA_HD) == h, out_r[h * L:(h + 1) * L, :], 0.0)
        o_ref[...] = out


def _moba_sample(proj3, cache_k, cache_v, page_table, n_seq, L):
    n_pool, page = cache_k.shape[0], cache_k.shape[1]
    n_pages = page_table.shape[1]
    ppb = MOBA_BLOCK // page
    nbp = n_pages // ppb
    n_row = MOBA_HEADS * L
    assert ppb == 2 and n_pages % ppb == 0 and page == LANES and nbp <= LANES and n_row % 16 == 0 and L <= LANES
    ck = jnp.transpose(cache_k, (0, 2, 3, 1)).reshape(n_pool, MOBA_W, page)
    cv = jnp.transpose(cache_v, (0, 2, 3, 1)).reshape(n_pool, MOBA_W, page)
    pt = page_table.reshape(-1).astype(I32)

    bps = MOBA_S_BLOCKS_PER_STEP if nbp % MOBA_S_BLOCKS_PER_STEP == 0 else 1
    n_steps = nbp // bps

    def page_map(t, off):
        return lambda b, j, pt_ref: (
            pt_ref[b * n_pages + ppb * (jnp.minimum(j, n_steps - 1) * bps + t) + off], 0, 0)

    new_map = lambda t: (lambda b, j, pt_ref: (t, b, 0))
    body = functools.partial(_moba_s_body, nbp=nbp, L=L, bps=bps)
    page_specs = [pl.BlockSpec((1, MOBA_W, page), page_map(t, off)) for t in range(bps) for off in (0, 1, 0, 1)]
    page_args = [a for _ in range(bps) for a in (ck, ck, cv, cv)]
    grid_spec = pltpu.PrefetchScalarGridSpec(
        num_scalar_prefetch=1,
        grid=(n_seq, n_steps + 1),
        in_specs=page_specs + [
                  pl.BlockSpec((None, L, MOBA_W), new_map(T_MQ)),
                  pl.BlockSpec((None, L, MOBA_W), new_map(T_MK)),
                  pl.BlockSpec((None, L, MOBA_W), new_map(T_MV))],
        out_specs=pl.BlockSpec((L, MOBA_W), lambda b, j, pt_ref: (b, 0)),
        scratch_shapes=[pltpu.VMEM((n_row, MOBA_W), BF16),
                        pltpu.VMEM((n_row, LANES), F32),
                        pltpu.VMEM((n_row, LANES), F32),
                        pltpu.VMEM((n_row, LANES), F32),
                        pltpu.VMEM((nbp, n_row, MOBA_W), F32)])
    return pl.pallas_call(
        body,
        grid_spec=grid_spec,
        out_shape=jax.ShapeDtypeStruct((n_seq * L, MOBA_W), F32),
        compiler_params=_cparams(("arbitrary", "arbitrary")),
        name="moba_sample",
    )(pt, *page_args, proj3, proj3, proj3)


def _post_body(og_ref, om_ref, ba_ref, bb_ref, x_ref, gt_ref, sc_ref, sh_ref, g2_ref, wpg_ref, wpm_ref,
               wo_ref, wr_ref, br_ref, x1_ref, h2_ref, lg_ref):
    ya = _dot(og_ref[...].astype(BF16), wpg_ref[...])
    yb = _dot(om_ref[...].astype(BF16), wpm_ref[...])
    ba = jnp.concatenate([ba_ref[0], ba_ref[1]], axis=1)
    bb = jnp.concatenate([bb_ref[0], bb_ref[1]], axis=1)
    mix = _sigmoid(ba) * ya + _sigmoid(bb) * yb
    x1 = x_ref[...] + gt_ref[0] * _dot(mix.astype(BF16), wo_ref[...])
    x1_ref[...] = x1
    ms = jnp.mean(x1 * x1, axis=-1, keepdims=True)
    h2 = x1 * lax.rsqrt(ms + EPS) * g2_ref[...] * (1.0 + sc_ref[0]) + sh_ref[0]
    h2_ref[...] = h2
    hh = h2.astype(BF16)
    hl = (h2 - hh.astype(F32)).astype(BF16)
    wr = wr_ref[...]
    wh = wr.astype(BF16)
    wl = (wr - wh.astype(F32)).astype(BF16)
    lg_ref[...] = _dot(hh, wh) + _dot(hh, wl) + _dot(hl, wh) + br_ref[...]


def _post(og, om, proj3, x2, gt, sc, sh, g2, wpg, wpm, wo, wr, br, tm, rows_per_mod):
    r, d = x2.shape
    nt = r // tm
    tiles_per_mod = max(rows_per_mod // tm, 1)
    mod_block = (1,) + gt.shape[1:]
    mod_map = lambda i: (i // tiles_per_mod, 0, 0)
    full = lambda a: pl.BlockSpec(a.shape, lambda i: (0,) * a.ndim)
    return pl.pallas_call(
        _post_body,
        grid=(nt,),
        in_specs=[pl.BlockSpec((tm, og.shape[1]), lambda i: (i, 0)),
                  pl.BlockSpec((tm, om.shape[1]), lambda i: (i, 0)),
                  pl.BlockSpec((2, tm, PROJ_TILE), lambda i: (T_BA // 2, i, 0)),
                  pl.BlockSpec((2, tm, PROJ_TILE), lambda i: (T_BB // 2, i, 0)),
                  pl.BlockSpec((tm, d), lambda i: (i, 0)),
                  pl.BlockSpec(mod_block, mod_map),
                  pl.BlockSpec(mod_block, mod_map),
                  pl.BlockSpec(mod_block, mod_map),
                  full(g2), full(wpg), full(wpm), full(wo), full(wr), full(br)],
        out_specs=[pl.BlockSpec((tm, d), lambda i: (i, 0)),
                   pl.BlockSpec((tm, d), lambda i: (i, 0)),
                   pl.BlockSpec((tm, LANES), lambda i: (i, 0))],
        out_shape=[jax.ShapeDtypeStruct((r, d), F32),
                   jax.ShapeDtypeStruct((r, d), F32),
                   jax.ShapeDtypeStruct((r, LANES), F32)],
        compiler_params=_cparams(("arbitrary",)),
        name="post",
    )(og, om, proj3, proj3, x2, gt, sc, sh, g2, wpg, wpm, wo, wr, br)


def _route_body(lg_ref, eidx_ref, rank_ref, gate_ref, cnt_ref, run_scr):
    i = pl.program_id(0)
    tm = lg_ref.shape[0]

    @pl.when(i == 0)
    def _():
        run_scr[...] = jnp.zeros(run_scr.shape, F32)

    l = lg_ref[...]
    lane = lax.broadcasted_iota(I32, l.shape, 1)
    vals, hots = [], []
    for _ in range(TOP_K):
        m = jnp.max(l, axis=1, keepdims=True)
        idx = jnp.min(jnp.where(l == m, lane, LANES), axis=1, keepdims=True)
        hot = lane == idx
        vals.append(m)
        hots.append(hot)
        l = jnp.where(hot, NEG_INF, l)
    es = [jnp.exp(v - vals[0]) for v in vals]
    den = es[0] + es[1] + es[2] + es[3]
    chosen = jnp.zeros(l.shape, F32)
    for hot in hots:
        chosen = chosen + jnp.where(hot, 1.0, 0.0)
    row = lax.broadcasted_iota(I32, (tm, tm), 0)
    col = lax.broadcasted_iota(I32, (tm, tm), 1)
    before = jnp.where(row > col, 1.0, 0.0).astype(BF16)
    pos = _dot(before, chosen.astype(BF16)) + run_scr[...]
    eidx = jnp.zeros(l.shape, I32)
    rank = jnp.zeros(l.shape, I32)
    gate = jnp.zeros(l.shape, F32)
    for k in range(TOP_K):
        e_k = jnp.min(jnp.where(hots[k], lane, LANES), axis=1, keepdims=True)
        r_k = jnp.sum(jnp.where(hots[k], pos, 0.0), axis=1, keepdims=True).astype(I32)
        eidx = jnp.where(lane == k, e_k, eidx)
        rank = jnp.where(lane == k, r_k, rank)
        gate = jnp.where(lane == k, es[k] / den, gate)
    eidx_ref[...] = eidx
    rank_ref[...] = rank
    gate_ref[...] = gate
    run_scr[...] = run_scr[...] + jnp.sum(chosen, axis=0, keepdims=True)
    cnt_ref[...] = jnp.broadcast_to(run_scr[...], cnt_ref.shape)


def _route(logits, tm):
    t = logits.shape[0]
    blk = pl.BlockSpec((tm, LANES), lambda i: (i, 0))
    return pl.pallas_call(
        _route_body,
        grid=(t // tm,),
        in_specs=[blk],
        out_specs=[blk, blk, blk, pl.BlockSpec((8, LANES), lambda i: (0, 0))],
        out_shape=[jax.ShapeDtypeStruct((t, LANES), I32),
                   jax.ShapeDtypeStruct((t, LANES), I32),
                   jax.ShapeDtypeStruct((t, LANES), F32),
                   jax.ShapeDtypeStruct((8, LANES), F32)],
        scratch_shapes=[pltpu.VMEM((1, LANES), F32)],
        compiler_params=_cparams(("arbitrary",)),
        name="route",
    )(logits)


def _dispatch_body(dest_ref, h_ref, zero_ref, xb_ref, sem, *, tm):
    del zero_ref
    i = pl.program_id(0)

    def copy(r, k):
        dst = dest_ref[(i * tm + r) * TOP_K + k]
        return pltpu.make_async_copy(h_ref.at[pl.ds(r, 1), :], xb_ref.at[pl.ds(dst, 1), :], sem)

    def start(r, carry):
        for k in range(TOP_K):
            copy(r, k).start()
        return carry

    def wait(r, carry):
        for k in range(TOP_K):
            copy(r, k).wait()
        return carry

    lax.fori_loop(0, tm, start, 0)
    lax.fori_loop(0, tm, wait, 0)


def _dispatch(dest_flat, h_all, n_rows, tm):
    t, d = h_all.shape
    zeros = jnp.zeros((n_rows, d), h_all.dtype)
    grid_spec = pltpu.PrefetchScalarGridSpec(
        num_scalar_prefetch=1,
        grid=(t // tm,),
        in_specs=[pl.BlockSpec((tm, d), lambda i, dr: (i, 0)), pl.BlockSpec(memory_space=pl.ANY)],
        out_specs=pl.BlockSpec(memory_space=pl.ANY),
        scratch_shapes=[pltpu.SemaphoreType.DMA(())])
    return pl.pallas_call(
        functools.partial(_dispatch_body, tm=tm),
        grid_spec=grid_spec,
        out_shape=jax.ShapeDtypeStruct((n_rows, d), h_all.dtype),
        input_output_aliases={2: 0},
        compiler_params=_cparams(("arbitrary",)),
        name="dispatch",
    )(dest_flat, h_all, zeros)


def _moe_body(be_ref, nu_ref, x_ref, wgu_ref, bgu_ref, wd_ref, bd_ref, y_ref, wgu_scr, wd_scr):
    i = pl.program_id(0)
    prev = be_ref[jnp.maximum(i - 1, 0)]
    fresh = jnp.logical_or(i == 0, be_ref[i] != prev)

    @pl.when(jnp.logical_and(fresh, i < nu_ref[0]))
    def _():
        wgu_scr[...] = wgu_ref[0].astype(BF16)
        wd_scr[...] = wd_ref[0].astype(BF16)

    @pl.when(i < nu_ref[0])
    def _():
        x = x_ref[...].astype(BF16)
        gu = _dot(x, wgu_scr[...]) + bgu_ref[0]
        glu = jnp.minimum(gu[:, :D_FF], SWIGLU_LIMIT)
        lin = jnp.clip(gu[:, D_FF:], -SWIGLU_LIMIT, SWIGLU_LIMIT)
        act = glu * _sigmoid(SWIGLU_ALPHA * glu) * (lin + 1.0)
        y_ref[...] = _dot(act.astype(BF16), wd_scr[...]) + bd_ref[0]

    @pl.when(i >= nu_ref[0])
    def _():
        y_ref[...] = jnp.zeros(y_ref.shape, F32)


def _moe(block_expert, n_used, xb, wgu, bgu, wd, bd):
    n_rows, d = xb.shape
    nblk = n_rows // MOE_ROWS
    ne = wgu.shape[0]
    emap = lambda i, be, nu: (be[i], 0, 0)
    grid_spec = pltpu.PrefetchScalarGridSpec(
        num_scalar_prefetch=2,
        grid=(nblk,),
        in_specs=[pl.BlockSpec((MOE_ROWS, d), lambda i, be, nu: (i, 0)),
                  pl.BlockSpec((1, d, 2 * D_FF), emap),
                  pl.BlockSpec((1, 1, 2 * D_FF), emap),
                  pl.BlockSpec((1, D_FF, d), emap),
                  pl.BlockSpec((1, 1, d), emap)],
        out_specs=pl.BlockSpec((MOE_ROWS, d), lambda i, be, nu: (i, 0)),
        scratch_shapes=[pltpu.VMEM((d, 2 * D_FF), BF16), pltpu.VMEM((D_FF, d), BF16)])
    return pl.pallas_call(
        _moe_body,
        grid_spec=grid_spec,
        out_shape=jax.ShapeDtypeStruct((n_rows, d), F32),
        compiler_params=_cparams(("arbitrary",)),
        name="experts",
    )(block_expert, n_used, xb, wgu, bgu.reshape(ne, 1, 2 * D_FF), wd, bd.reshape(ne, 1, d))


def _combine_body(dest_ref, yb_ref, gate_ref, x1_ref, gt_ref, gf_ref, y_ref, buf, sem, *, tm, tok0):
    i = pl.program_id(0)

    def copy(r, k):
        src = dest_ref[(tok0 + i * tm + r) * TOP_K + k]
        return pltpu.make_async_copy(yb_ref.at[pl.ds(src, 1), :], buf.at[k, pl.ds(r, 1), :], sem)

    def start(r, carry):
        for k in range(TOP_K):
            copy(r, k).start()
        return carry

    def wait(r, carry):
        for k in range(TOP_K):
            copy(r, k).wait()
        return carry

    lax.fori_loop(0, tm, start, 0)
    lax.fori_loop(0, tm, wait, 0)
    g = gate_ref[...]
    y = g[:, 0:1] * buf[0]
    for k in range(1, TOP_K):
        y = y + g[:, k:k + 1] * buf[k]
    x2 = x1_ref[...] + gt_ref[0] * y
    ms = jnp.mean(x2 * x2, axis=-1, keepdims=True)
    y_ref[...] = x2 * lax.rsqrt(ms + EPS) * gf_ref[...]


def _combine(dest_flat, yb, gates, x1, gt, g_final, tm, rows_per_mod, tok0):
    r, d = x1.shape
    nt = r // tm
    tiles_per_mod = max(rows_per_mod // tm, 1)
    tile0 = tok0 // tm
    mod_block = (1,) + gt.shape[1:]
    grid_spec = pltpu.PrefetchScalarGridSpec(
        num_scalar_prefetch=1,
        grid=(nt,),
        in_specs=[pl.BlockSpec(memory_space=pl.ANY),
                  pl.BlockSpec((tm, LANES), lambda i, dr: (tile0 + i, 0)),
                  pl.BlockSpec((tm, d), lambda i, dr: (i, 0)),
                  pl.BlockSpec(mod_block, lambda i, dr: (i // tiles_per_mod, 0, 0)),
                  pl.BlockSpec((1, d), lambda i, dr: (0, 0))],
        out_specs=pl.BlockSpec((tm, d), lambda i, dr: (i, 0)),
        scratch_shapes=[pltpu.VMEM((TOP_K, tm, d), F32), pltpu.SemaphoreType.DMA(())])
    return pl.pallas_call(
        functools.partial(_combine_body, tm=tm, tok0=tok0),
        grid_spec=grid_spec,
        out_shape=jax.ShapeDtypeStruct((r, d), F32),
        compiler_params=_cparams(("arbitrary",)),
        name="combine",
    )(dest_flat, yb, gates, x1, gt, g_final)


def _rope_tables(pos):
    half = MOBA_HD // 2
    inv = 1.0 / (ROPE_THETA ** (jnp.arange(half, dtype=F32) / half))
    ang = pos.astype(F32)[:, None] * inv[None, :]
    cos = jnp.cos(ang)
    sin = jnp.sin(ang)
    cos_h = jnp.concatenate([cos, cos], axis=1)
    sin_h = jnp.concatenate([-sin, sin], axis=1)
    return jnp.tile(cos_h, (1, MOBA_HEADS)), jnp.tile(sin_h, (1, MOBA_HEADS))


def _row_tile(n, cap):
    t = min(n, cap)
    while n % t:
        t //= 2
    return t


def kernel(x_prompt, x_sample, cache_k, cache_v, state_gla, page_table, c_prompt, c_sample, w_ada, b_ada, g_norm1, w_in, w_gla_a2, b_gla_a2, g_gla_head, w_proj_gla, w_proj_moba, w_out, g_norm2, w_router, b_router, w_gate_up, b_gate_up, w_down, b_down, g_final):
    nb_p, seq, d = x_prompt.shape
    nb_s, dec = x_sample.shape[:2]
    past_len = page_table.shape[1] * cache_k.shape[2]
    assert w_ada.shape[0] == 1 and d == D_MODEL
    assert seq % MOBA_BLOCK == 0 and past_len % MOBA_BLOCK == 0 and dec <= MOBA_BLOCK and dec % 8 == 0
    r_p, r_s = nb_p * seq, nb_s * dec
    t_all = r_p + r_s

    c_all = jnp.concatenate([c_prompt, c_sample], axis=0)
    mod = _ada(c_all, w_ada[0], b_ada[0]).reshape(nb_p + nb_s, 6, d)
    mod_p = [mod[:nb_p, k].reshape(nb_p, 1, d) for k in range(6)]
    mod_s = [jnp.broadcast_to(mod[nb_p:, k][:, None, :], (nb_s, dec, d)).reshape(1, r_s, d) for k in range(6)]

    w = w_in[0]
    gkw = GLA_HEADS * GLA_DK
    gvw = GLA_HEADS * GLA_DV
    c_ga = 2 * gkw + 2 * gvw
    pad = jnp.zeros((d, PROJ_TILE - GLA_LOWRANK), w.dtype)
    w_in_p = jnp.concatenate([w[:, :c_ga + GLA_LOWRANK], pad, w[:, c_ga + GLA_LOWRANK:]], axis=1).astype(BF16)
    assert w_in_p.shape[1] == N_PROJ_TILES * PROJ_TILE
    w2_p = jnp.zeros((PROJ_TILE, gkw), BF16).at[:GLA_LOWRANK].set(w_gla_a2[0].astype(BF16))
    b2 = b_gla_a2[0].reshape(1, gkw)
    g1 = g_norm1[0].reshape(1, d)
    g2 = g_norm2[0].reshape(1, d)
    wpg = w_proj_gla[0].astype(BF16)
    wpm = w_proj_moba[0].astype(BF16)
    wo = w_out[0].astype(BF16)
    wr = jnp.zeros((d, LANES), F32).at[:, :N_EXPERTS].set(w_router[0])
    br = jnp.full((1, LANES), NEG_INF, F32).at[0, :N_EXPERTS].set(b_router[0])
    g_head = g_gla_head[0].reshape(1, GLA_DV)

    cos_p, sin_p = _rope_tables(jnp.arange(seq, dtype=I32))
    cos_s, sin_s = _rope_tables(past_len + jnp.arange(dec, dtype=I32))
    cos_s, sin_s = jnp.tile(cos_s, (nb_s, 1)), jnp.tile(sin_s, (nb_s, 1))

    xp2 = x_prompt.reshape(r_p, d)
    xs2 = x_sample.reshape(r_s, d)
    tm_p = _row_tile(seq, 1024)

    proj_p, kt_p, vt_p = _inproj(xp2, mod_p[1], mod_p[0], g1, w_in_p, w2_p, b2, cos_p, sin_p, tm_p, seq, True)
    s0_p = jnp.zeros((nb_p, GLA_HEADS, GLA_DK, GLA_DV), state_gla.dtype)
    og_p, st_p = _gla(proj_p, s0_p, g_head, nb_p, seq, _row_tile(seq, 64))
    om_p = _moba_prompt(proj_p, vt_p, nb_p, seq)
    tm_post = _row_tile(seq, 512)
    x1_p, h2_p, lg_p = _post(og_p, om_p, proj_p, xp2, mod_p[2], mod_p[4], mod_p[3], g2, wpg, wpm, wo, wr, br,
                             tm_post, seq)

    proj_s, k5_s, v5_s = _inproj(xs2, mod_s[1], mod_s[0], g1, w_in_p, w2_p, b2, cos_s, sin_s, r_s, r_s, False)
    og_s, st_s = _gla(proj_s, state_gla[0], g_head, nb_s, dec, dec)
    om_s = _moba_sample(proj_s, cache_k[0], cache_v[0], page_table, nb_s, dec)
    x1_s, h2_s, lg_s = _post(og_s, om_s, proj_s, xs2, mod_s[2], mod_s[4], mod_s[3], g2, wpg, wpm, wo, wr, br,
                             r_s, r_s)

    h_all = jnp.concatenate([h2_p, h2_s], axis=0)
    lg_all = jnp.concatenate([lg_p, lg_s], axis=0)
    tm_r = _row_tile(t_all, 256)
    eidx, rank, gates, cnt = _route(lg_all, tm_r)
    counts = cnt[0, :N_EXPERTS].astype(I32)
    nblk_e = (counts + MOE_ROWS - 1) // MOE_ROWS
    blk_end = jnp.cumsum(nblk_e)
    blk_start = blk_end - nblk_e
    dest = blk_start[eidx[:, :TOP_K]] * MOE_ROWS + rank[:, :TOP_K]
    dest_flat = dest.reshape(-1).astype(I32)
    n_blocks = -(-(t_all * TOP_K) // MOE_ROWS) + N_EXPERTS
    blk_ids = jnp.arange(n_blocks, dtype=I32)
    block_expert = jnp.minimum(jnp.sum((blk_end[None, :] <= blk_ids[:, None]).astype(I32), axis=1), N_EXPERTS - 1)
    n_used = blk_end[-1:].astype(I32)
    xb = _dispatch(dest_flat, h_all, n_blocks * MOE_ROWS, tm_r)
    yb = _moe(block_expert, n_used, xb, w_gate_up[0], b_gate_up[0], w_down[0], b_down[0])
    gf = g_final.reshape(1, d)
    tm_c = _row_tile(seq, 256)
    y_p = _combine(dest_flat, yb, gates, x1_p, mod_p[5], gf, tm_c, seq, 0)
    y_s = _combine(dest_flat, yb, gates, x1_s, mod_s[5], gf, r_s, r_s, r_p)

    to_out = lambda a: a.reshape(1, nb_p, MOBA_HEADS, MOBA_HD, seq).transpose(0, 1, 4, 2, 3)
    k_p = to_out(kt_p)
    v_p = to_out(vt_p)
    k_s = k5_s.reshape(1, nb_s, dec, MOBA_HEADS, MOBA_HD)
    v_s = v5_s.reshape(1, nb_s, dec, MOBA_HEADS, MOBA_HD)
    return (y_p.reshape(nb_p, seq, d), y_s.reshape(nb_s, dec, d), k_p, v_p, st_p[None],
            k_s, v_s, st_s[None])
```

```python
import functools

import jax
import jax.numpy as jnp
from jax import lax
from jax.experimental import pallas as pl
from jax.experimental.pallas import tpu as pltpu

F32 = jnp.float32
BF16 = jnp.bfloat16
I32 = jnp.int32

D_MODEL = 1024
GLA_HEADS = 4
GLA_DK = 128
GLA_DV = 256
GLA_LOWRANK = 16
GLA_GATE_NORM = 16.0
GLA_SUB = 16
MOBA_HEADS = 8
MOBA_HD = 64
MOBA_BLOCK = 256
MOBA_TOPK = 3
MOBA_W = MOBA_HEADS * MOBA_HD
ROPE_THETA = 10000.0
N_EXPERTS = 32
TOP_K = 4
D_FF = D_MODEL
SWIGLU_LIMIT = 7.0
SWIGLU_ALPHA = 1.702
EPS = 1e-6
LANES = 128
PROJ_TILE = 512

T_GQ, T_GK, T_GV, T_GR, T_GA, T_MQ, T_MK, T_MV, T_BA, T_BB = 0, 1, 2, 4, 6, 7, 8, 9, 10, 12
N_PROJ_TILES = 14
MOE_ROWS = 512
MOBA_S_BLOCKS_PER_STEP = 8
VMEM_LIMIT = 56 * 1024 * 1024

NEG_INF = float("-inf")


def _sigmoid(x):
    return 1.0 / (1.0 + jnp.exp(-x))


def _split3(x):
    hi = x.astype(BF16)
    r = x - hi.astype(F32)
    mid = r.astype(BF16)
    lo = (r - mid.astype(F32)).astype(BF16)
    return hi, mid, lo


def _dot(a, b):
    return jnp.dot(a, b, preferred_element_type=F32)


def _dot_nt(a, b):
    return lax.dot_general(a, b, (((1,), (1,)), ((), ())), preferred_element_type=F32)


def _dot_nt_f32(a, b):
    ah = a.astype(BF16)
    al = (a - ah.astype(F32)).astype(BF16)
    bh = b.astype(BF16)
    bl = (b - bh.astype(F32)).astype(BF16)
    return _dot_nt(ah, bh) + _dot_nt(ah, bl) + _dot_nt(al, bh)


def _div(x, n):
    assert n & (n - 1) == 0
    return lax.shift_right_logical(x, n.bit_length() - 1)


def _mod(x, n):
    assert n & (n - 1) == 0
    return x & (n - 1)


ROW_TILE = D_MODEL // LANES


def _store_row_tiles(ref, val):
    n = val.shape[0]
    for s in range(ROW_TILE):
        ref[pl.ds(s, n, stride=ROW_TILE), :] = val[:, s * LANES:(s + 1) * LANES]


def _load_row_tiles(ref, n, lead=()):
    return jnp.concatenate([ref[lead + (pl.ds(s, n, stride=ROW_TILE), slice(None))] for s in range(ROW_TILE)],
                           axis=1)


def _cparams(sem):
    return pltpu.CompilerParams(dimension_semantics=sem, vmem_limit_bytes=VMEM_LIMIT)


def _ada_body(c_ref, w_ref, b_ref, o_ref):
    c = c_ref[...]
    s = c * _sigmoid(c)
    o_ref[...] = _dot(s.astype(BF16), w_ref[...].astype(BF16)) + b_ref[...]


def _ada(c, w, b):
    n = w.shape[1]
    tn = n // 4
    return pl.pallas_call(
        _ada_body,
        grid=(4,),
        in_specs=[pl.BlockSpec(c.shape, lambda j: (0, 0)),
                  pl.BlockSpec((w.shape[0], tn), lambda j: (0, j)),
                  pl.BlockSpec((1, tn), lambda j: (0, j))],
        out_specs=pl.BlockSpec((c.shape[0], tn), lambda j: (0, j)),
        out_shape=jax.ShapeDtypeStruct((c.shape[0], n), F32),
        compiler_params=_cparams(("arbitrary",)),
        name="ada",
    )(c, w, b.reshape(1, n))


def _rotary(x, cos, sin_signed):
    lane = lax.broadcasted_iota(I32, x.shape, 1)
    half = MOBA_HD // 2
    partner = jnp.where((lane & (MOBA_HD - 1)) < half, lane + half, lane - half)
    w = x.shape[1]
    r1 = pltpu.roll(x, half, 1)
    i1 = pltpu.roll(lane, half, 1)
    r2 = pltpu.roll(x, w - half, 1)
    swapped = jnp.where(i1 == partner, r1, r2)
    return x * cos + swapped * sin_signed


def _log_sigmoid(x):
    return jnp.minimum(x, 0.0) - jnp.log(1.0 + jnp.exp(-jnp.abs(x)))


def _store_heads(ref, val, token_minor):
    if token_minor:
        ref[0] = val.T
        return
    tm = val.shape[0]
    for h in range(MOBA_HEADS):
        ref[pl.ds(h, tm, stride=MOBA_HEADS), :] = val[:, h * MOBA_HD:(h + 1) * MOBA_HD]


def _inproj_body(x_ref, sc_ref, sh_ref, g_ref, w_ref, w2_ref, b2_ref, cos_ref, sin_ref, o_ref, k5_ref, v5_ref,
                 h_scr, *, token_minor):
    j = pl.program_id(1)

    @pl.when(j == 0)
    def _():
        x = x_ref[...]
        ms = jnp.mean(x * x, axis=-1, keepdims=True)
        y = x * lax.rsqrt(ms + EPS) * g_ref[...]
        h = y * (1.0 + sc_ref[0]) + sh_ref[0]
        h_scr[...] = h.astype(BF16)

    acc = _dot(h_scr[...], w_ref[...])
    is_la = j == T_GA
    special = functools.reduce(jnp.logical_or, [j == T_MQ, j == T_MK, j == T_MV, is_la])

    @pl.when(j == T_MQ)
    def _():
        o_ref[0] = _rotary(acc, cos_ref[...], sin_ref[...])

    @pl.when(j == T_MK)
    def _():
        rot = _rotary(acc, cos_ref[...], sin_ref[...])
        o_ref[0] = rot
        _store_heads(k5_ref, rot, token_minor)

    @pl.when(j == T_MV)
    def _():
        o_ref[0] = acc
        _store_heads(v5_ref, acc, token_minor)

    @pl.when(is_la)
    def _():
        z = _dot(acc.astype(BF16), w2_ref[...]) + b2_ref[...]
        o_ref[0] = _log_sigmoid(z) * (1.0 / GLA_GATE_NORM)

    @pl.when(jnp.logical_not(special))
    def _():
        o_ref[0] = acc


def _inproj(x2, sc, sh, g1, w_in_p, w2_p, b2, cos, sin, tm, rows_per_mod, token_minor):
    r, d = x2.shape
    nt = r // tm
    tiles_per_mod = max(rows_per_mod // tm, 1)
    tiles_per_tab = cos.shape[0] // tm
    mod_block = (1,) + sc.shape[1:]
    if token_minor:
        seq = cos.shape[0]
        kv_spec = pl.BlockSpec((1, MOBA_W, tm), lambda i, j: (i // tiles_per_tab, 0, i % tiles_per_tab))
        kv_shape = jax.ShapeDtypeStruct((r // seq, MOBA_W, seq), F32)
    else:
        kv_spec = pl.BlockSpec((tm * MOBA_HEADS, MOBA_HD), lambda i, j: (i, 0))
        kv_shape = jax.ShapeDtypeStruct((r * MOBA_HEADS, MOBA_HD), F32)
    return pl.pallas_call(
        functools.partial(_inproj_body, token_minor=token_minor),
        grid=(nt, N_PROJ_TILES),
        in_specs=[pl.BlockSpec((tm, d), lambda i, j: (i, 0)),
                  pl.BlockSpec(mod_block, lambda i, j: (i // tiles_per_mod, 0, 0)),
                  pl.BlockSpec(mod_block, lambda i, j: (i // tiles_per_mod, 0, 0)),
                  pl.BlockSpec((1, d), lambda i, j: (0, 0)),
                  pl.BlockSpec((d, PROJ_TILE), lambda i, j: (0, j)),
                  pl.BlockSpec((PROJ_TILE, PROJ_TILE), lambda i, j: (0, 0)),
                  pl.BlockSpec((1, PROJ_TILE), lambda i, j: (0, 0)),
                  pl.BlockSpec((tm, PROJ_TILE), lambda i, j: (i % tiles_per_tab, 0)),
                  pl.BlockSpec((tm, PROJ_TILE), lambda i, j: (i % tiles_per_tab, 0))],
        out_specs=[pl.BlockSpec((1, tm, PROJ_TILE), lambda i, j: (j, i, 0)), kv_spec, kv_spec],
        out_shape=[jax.ShapeDtypeStruct((N_PROJ_TILES, r, PROJ_TILE), F32), kv_shape, kv_shape],
        scratch_shapes=[pltpu.VMEM((tm, d), BF16)],
        compiler_params=_cparams(("arbitrary", "arbitrary")),
        name="inproj",
    )(x2, sc, sh, g1, w_in_p, w2_p, b2, cos, sin)


def _gla_body(q_ref, k_ref, v_ref, la_ref, gr_ref, s0_ref, gh_ref, og_ref, sout_ref, st_scr, *, C, SB):
    c = pl.program_id(1)
    nc = pl.num_programs(1)

    @pl.when(c == 0)
    def _():
        for h in range(GLA_HEADS):
            st_scr[h] = s0_ref[0, h].T

    la = la_ref[0]
    row_c = lax.broadcasted_iota(I32, (C, C), 0)
    col_c = lax.broadcasted_iota(I32, (C, C), 1)
    tri = jnp.where(row_c >= col_c, 1.0, 0.0).astype(BF16)
    hi, mid, lo = _split3(la)
    b_all = _dot(tri, hi) + _dot(tri, mid) + _dot(tri, lo)
    q_all = q_ref[0] * (GLA_DK ** -0.5)
    k_all = k_ref[0]
    gh = gh_ref[...]
    row_k = lax.broadcasted_iota(I32, (C, GLA_DK), 0)
    row_sb = lax.broadcasted_iota(I32, (SB, 1), 0)
    lane_sb = lax.broadcasted_iota(I32, (SB, C), 1)

    for h in range(GLA_HEADS):
        ks = slice(h * GLA_DK, (h + 1) * GLA_DK)
        vs = slice((h % 2) * GLA_DV, (h % 2 + 1) * GLA_DV)
        b = b_all[:, ks]
        q = q_all[:, ks]
        k = k_all[:, ks]
        v = v_ref[h // 2][:, vs]
        st = st_scr[h]
        b_last = b[C - 1:C, :]
        o = _dot_nt((q * jnp.exp(b)).astype(BF16), st.astype(BF16))
        att_rows = []
        for i in range(C // SB):
            r0 = i * SB
            bi = b[r0:r0 + SB]
            qi = q[r0:r0 + SB]
            ki = k[r0:r0 + SB]
            if i > 0:
                bref = b[r0 - 1:r0, :]
                qt = qi * jnp.exp(bi - bref)
                kt = jnp.where(row_k < r0, k * jnp.exp(jnp.minimum(bref - b, 0.0)), 0.0)
                att_i = _dot_nt(qt.astype(BF16), kt.astype(BF16))
            else:
                att_i = jnp.zeros((SB, C), F32)
            for s in range(SB):
                dec = jnp.exp(jnp.minimum(bi - bi[s:s + 1, :], 0.0))
                col = jnp.sum(qi * ki[s:s + 1, :] * dec, axis=1, keepdims=True)
                col = jnp.where(row_sb >= s, col, 0.0)
                att_i = att_i + jnp.where(lane_sb == r0 + s, col, 0.0)
            att_rows.append(att_i)
        att = att_rows[0] if len(att_rows) == 1 else jnp.concatenate(att_rows, axis=0)
        vb = v.astype(BF16)
        o = o + _dot(att.astype(BF16), vb)
        kd = k * jnp.exp(b_last - b)
        st_scr[h] = st * jnp.exp(b_last) + _dot(v.T.astype(BF16), kd.astype(BF16))
        on = o * lax.rsqrt(jnp.mean(o * o, axis=-1, keepdims=True) + EPS) * gh
        gr = gr_ref[h // 2][:, vs]
        og_ref[:, h * GLA_DV:(h + 1) * GLA_DV] = (on * (gr * _sigmoid(gr))).astype(og_ref.dtype)

    @pl.when(c == nc - 1)
    def _():
        for h in range(GLA_HEADS):
            sout_ref[0, h] = st_scr[h].T


def _gla(proj3, s0, g_head, n_seq, seq_len, chunk):
    r = proj3.shape[1]
    nc = seq_len // chunk
    sb = min(GLA_SUB, chunk)
    out_dtype = BF16 if chunk % 16 == 0 else F32
    row = lambda b, c: b * nc + c
    body = functools.partial(_gla_body, C=chunk, SB=sb)
    return pl.pallas_call(
        body,
        grid=(n_seq, nc),
        in_specs=[pl.BlockSpec((1, chunk, PROJ_TILE), lambda b, c: (T_GQ, row(b, c), 0)),
                  pl.BlockSpec((1, chunk, PROJ_TILE), lambda b, c: (T_GK, row(b, c), 0)),
                  pl.BlockSpec((2, chunk, PROJ_TILE), lambda b, c: (T_GV // 2, row(b, c), 0)),
                  pl.BlockSpec((1, chunk, PROJ_TILE), lambda b, c: (T_GA, row(b, c), 0)),
                  pl.BlockSpec((2, chunk, PROJ_TILE), lambda b, c: (T_GR // 2, row(b, c), 0)),
                  pl.BlockSpec((1, GLA_HEADS, GLA_DK, GLA_DV), lambda b, c: (b, 0, 0, 0)),
                  pl.BlockSpec((1, GLA_DV), lambda b, c: (0, 0))],
        out_specs=[pl.BlockSpec((chunk, GLA_HEADS * GLA_DV), lambda b, c: (row(b, c), 0)),
                   pl.BlockSpec((1, GLA_HEADS, GLA_DK, GLA_DV), lambda b, c: (b, 0, 0, 0))],
        out_shape=[jax.ShapeDtypeStruct((r, GLA_HEADS * GLA_DV), out_dtype),
                   jax.ShapeDtypeStruct((n_seq, GLA_HEADS, GLA_DK, GLA_DV), F32)],
        scratch_shapes=[pltpu.VMEM((GLA_HEADS, GLA_DV, GLA_DK), F32)],
        compiler_params=_cparams(("arbitrary", "arbitrary")),
        name="gla",
    )(proj3, proj3, proj3, proj3, proj3, s0, g_head)


def _select_topk(g, n_valid_rows, n_rows, n_sel):
    row = lax.broadcasted_iota(I32, g.shape, 0)
    valid = row < n_valid_rows
    gm = jnp.where(valid, g, NEG_INF)
    rank = jnp.zeros(g.shape, F32)
    for jp in range(n_rows):
        gj = gm[jp:jp + 1, :]
        ahead = jnp.logical_or(gj > gm, jnp.logical_and(gj == gm, jp < row))
        rank = rank + jnp.where(ahead, 1.0, 0.0)
    return jnp.where(jnp.logical_and(valid, rank < n_sel), 1.0, 0.0)


def _moba_p_body(q_ref, k_ref, v_ref, o_ref, k_scr, vt_scr, km_scr, sel_scr, m_scr, l_scr, acc_scr, *, nb, nbp):
    i = pl.program_id(1)
    blk = MOBA_BLOCK
    scale = MOBA_HD ** -0.5

    @pl.when(i == 0)
    def _():
        km_scr[...] = jnp.zeros(km_scr.shape, F32)
        for j in range(nb):
            kj = k_ref[0, j * blk:(j + 1) * blk, :]
            k_scr[j] = kj.astype(BF16)
            km_scr[j:j + 1, :] = jnp.mean(kj, axis=0, keepdims=True)
            vt_scr[j] = v_ref[0, :, j * blk:(j + 1) * blk].astype(BF16)

    q = q_ref[0]
    km = km_scr[...]
    km_rep = jnp.concatenate([km] * MOBA_HEADS, axis=0)
    wrow = lax.broadcasted_iota(I32, km_rep.shape, 0)
    wlane = lax.broadcasted_iota(I32, km_rep.shape, 1)
    wt = jnp.where(_div(wlane, MOBA_HD) == _div(wrow, nbp), km_rep, 0.0)
    gates_t = _dot_nt_f32(wt, q)
    for h in range(MOBA_HEADS):
        sel_scr[h] = _select_topk(gates_t[h * nbp:(h + 1) * nbp, :], i, nb, MOBA_TOPK)

    n_pair = MOBA_HEADS // 2
    pair_lane = lax.broadcasted_iota(I32, (blk, LANES), 1)
    key_row = lax.broadcasted_iota(I32, (blk, 2 * blk), 0)
    q_col = _mod(lax.broadcasted_iota(I32, (blk, 2 * blk), 1), blk)
    qms = []
    for pr in range(n_pair):
        qp = q[:, pr * LANES:(pr + 1) * LANES]
        qms.append(jnp.concatenate([jnp.where(_div(pair_lane, MOBA_HD) == u, qp, 0.0) for u in range(2)],
                                   axis=0).astype(BF16))

    def scores(j, pr):
        kj = k_scr[j, :, pr * LANES:(pr + 1) * LANES]
        return _dot_nt(kj, qms[pr]) * scale

    def values(j, h):
        return vt_scr[j, h * MOBA_HD:(h + 1) * MOBA_HD, :]

    for pr in range(n_pair):
        s = jnp.where(key_row <= q_col, scores(i, pr), NEG_INF)
        m = jnp.max(s, axis=0, keepdims=True)
        p = jnp.exp(s - m)
        m_scr[pr] = m
        l_scr[pr] = jnp.sum(p, axis=0, keepdims=True)
        pb = p.astype(BF16)
        for u in range(2):
            acc_scr[2 * pr + u] = _dot(values(i, 2 * pr + u), pb[:, u * blk:(u + 1) * blk])

    def step(j, carry):
        for pr in range(n_pair):
            keep = jnp.concatenate([sel_scr[2 * pr + u, pl.ds(j, 1), :] for u in range(2)], axis=1)
            s = jnp.where(keep > 0.5, scores(j, pr), NEG_INF)
            m_old = m_scr[pr]
            m_new = jnp.maximum(m_old, jnp.max(s, axis=0, keepdims=True))
            alpha = jnp.exp(m_old - m_new)
            p = jnp.exp(s - m_new)
            m_scr[pr] = m_new
            l_scr[pr] = l_scr[pr] * alpha + jnp.sum(p, axis=0, keepdims=True)
            pb = p.astype(BF16)
            for u in range(2):
                h = 2 * pr + u
                cs = slice(u * blk, (u + 1) * blk)
                acc_scr[h] = acc_scr[h] * alpha[:, cs] + _dot(values(j, h), pb[:, cs])
        return carry

    lax.fori_loop(0, i, step, 0)
    outs = []
    for h in range(MOBA_HEADS):
        l = l_scr[h // 2][:, (h % 2) * blk:(h % 2 + 1) * blk]
        outs.append(acc_scr[h] / l)
    out_t = jnp.concatenate(outs, axis=0)
    o_ref[...] = out_t.T.astype(o_ref.dtype)


def _moba_prompt(proj3, v_t, n_seq, seq_len):
    r = proj3.shape[1]
    nb = seq_len // MOBA_BLOCK
    nbp = -(-nb // 8) * 8
    body = functools.partial(_moba_p_body, nb=nb, nbp=nbp)
    return pl.pallas_call(
        body,
        grid=(n_seq, nb),
        in_specs=[pl.BlockSpec((1, MOBA_BLOCK, MOBA_W), lambda b, i: (T_MQ, b * nb + i, 0)),
                  pl.BlockSpec((1, seq_len, MOBA_W), lambda b, i: (T_MK, b, 0)),
                  pl.BlockSpec((1, MOBA_W, seq_len), lambda b, i: (b, 0, 0))],
        out_specs=pl.BlockSpec((MOBA_BLOCK, MOBA_W), lambda b, i: (b * nb + i, 0)),
        out_shape=jax.ShapeDtypeStruct((r, MOBA_W), BF16),
        scratch_shapes=[pltpu.VMEM((nb, MOBA_BLOCK, MOBA_W), BF16),
                        pltpu.VMEM((nb, MOBA_W, MOBA_BLOCK), BF16),
                        pltpu.VMEM((nbp, MOBA_W), F32),
                        pltpu.VMEM((MOBA_HEADS, nbp, MOBA_BLOCK), F32),
                        pltpu.VMEM((MOBA_HEADS // 2, 1, 2 * MOBA_BLOCK), F32),
                        pltpu.VMEM((MOBA_HEADS // 2, 1, 2 * MOBA_BLOCK), F32),
                        pltpu.VMEM((MOBA_HEADS, MOBA_HD, MOBA_BLOCK), F32)],
        compiler_params=_cparams(("arbitrary", "arbitrary")),
        name="moba_prompt",
    )(proj3, proj3, v_t)


def _moba_s_body(pt_ref, *refs, nbp, L, bps):
    del pt_ref
    page_refs = refs[:4 * bps]
    qn_ref, kn_ref, vn_ref, o_ref, q2_scr, gate_scr, m_scr, l_scr, acc_scr = refs[4 * bps:]
    j = pl.program_id(1)
    n_steps = nbp // bps
    scale = MOBA_HD ** -0.5
    n_row = MOBA_HEADS * L
    blk_lane = lax.broadcasted_iota(I32, (n_row, LANES), 1)

    @pl.when(j == 0)
    def _():
        qn = qn_ref[...]
        q2 = jnp.concatenate([qn] * MOBA_HEADS, axis=0)
        row = lax.broadcasted_iota(I32, q2.shape, 0)
        lane = lax.broadcasted_iota(I32, q2.shape, 1)
        q2_scr[...] = jnp.where(_div(lane, MOBA_HD) == _div(row, L), q2, 0.0).astype(BF16)
        gate_scr[...] = jnp.full(gate_scr.shape, NEG_INF, F32)
        m_scr[...] = jnp.zeros(m_scr.shape, F32)
        l_scr[...] = jnp.zeros(l_scr.shape, F32)

    def softmax_partial(s):
        m = jnp.max(s, axis=1, keepdims=True)
        p = jnp.exp(s - m)
        return m, jnp.sum(p, axis=1, keepdims=True), p.astype(BF16)

    @pl.when(j < n_steps)
    def _():
        for t in range(bps):
            ka_ref, kb_ref, va_ref, vb_ref = page_refs[4 * t:4 * t + 4]
            jj = j * bps + t
            kt = jnp.concatenate([ka_ref[0], kb_ref[0]], axis=1).astype(BF16)
            vt = jnp.concatenate([va_ref[0], vb_ref[0]], axis=1).astype(BF16)
            s = _dot(q2_scr[...], kt)
            gate = jnp.mean(s, axis=1, keepdims=True)
            m, l, p = softmax_partial(s * scale)
            hot = blk_lane == jj
            gate_scr[...] = jnp.where(hot, gate, gate_scr[...])
            m_scr[...] = jnp.where(hot, m, m_scr[...])
            l_scr[...] = jnp.where(hot, l, l_scr[...])
            acc_scr[jj] = _dot_nt(p, vt)

    @pl.when(j == n_steps)
    def _():
        pad = jnp.zeros((LANES - L, MOBA_W), F32)
        kn = jnp.concatenate([kn_ref[...], pad], axis=0).astype(BF16)
        vn = jnp.concatenate([vn_ref[...], pad], axis=0).astype(BF16)
        s = _dot_nt(q2_scr[...], kn) * scale
        key = lax.broadcasted_iota(I32, s.shape, 1)
        qpos = _mod(lax.broadcasted_iota(I32, s.shape, 0), L)
        m_own, l_own, p_own = softmax_partial(jnp.where(key <= qpos, s, NEG_INF))
        acc_own = _dot(p_own, vn)
        g = gate_scr[...]
        rank = jnp.zeros(g.shape, F32)
        for jp in range(nbp):
            gj = g[:, jp:jp + 1]
            ahead = jnp.logical_or(gj > g, jnp.logical_and(gj == g, jp < blk_lane))
            rank = rank + jnp.where(ahead, 1.0, 0.0)
        sel = jnp.logical_and(blk_lane < nbp, rank < min(MOBA_TOPK, nbp + 1))
        m_all = m_scr[...]
        m_top = jnp.maximum(m_own, jnp.max(jnp.where(sel, m_all, NEG_INF), axis=1, keepdims=True))
        w = jnp.where(sel, jnp.exp(m_all - m_top), 0.0)
        w_own = jnp.exp(m_own - m_top)
        den = jnp.sum(w * l_scr[...], axis=1, keepdims=True) + w_own * l_own
        num = w_own * acc_own
        for jj in range(nbp):
            num = num + w[:, jj:jj + 1] * acc_scr[jj]
        out_r = num / den
        lane = lax.broadcasted_iota(I32, (L, MOBA_W), 1)
        out = jnp.zeros((L, MOBA_W), F32)
        for h in range(MOBA_HEADS):
            out = out + jnp.where(_div(lane, MOBA_HD) == h, out_r[h * L:(h + 1) * L, :], 0.0)
        o_ref[...] = out


def _moba_sample(proj3, cache_k, cache_v, page_table, n_seq, L):
    n_pool, page = cache_k.shape[0], cache_k.shape[1]
    n_pages = page_table.shape[1]
    ppb = MOBA_BLOCK // page
    nbp = n_pages // ppb
    n_row = MOBA_HEADS * L
    assert ppb == 2 and n_pages % ppb == 0 and page == LANES and nbp <= LANES and n_row % 16 == 0 and L <= LANES
    ck = jnp.transpose(cache_k, (0, 2, 3, 1)).reshape(n_pool, MOBA_W, page)
    cv = jnp.transpose(cache_v, (0, 2, 3, 1)).reshape(n_pool, MOBA_W, page)
    pt = page_table.reshape(-1).astype(I32)

    bps = MOBA_S_BLOCKS_PER_STEP if nbp % MOBA_S_BLOCKS_PER_STEP == 0 else 1
    n_steps = nbp // bps

    def page_map(t, off):
        return lambda b, j, pt_ref: (
            pt_ref[b * n_pages + ppb * (jnp.minimum(j, n_steps - 1) * bps + t) + off], 0, 0)

    new_map = lambda t: (lambda b, j, pt_ref: (t, b, 0))
    body = functools.partial(_moba_s_body, nbp=nbp, L=L, bps=bps)
    page_specs = [pl.BlockSpec((1, MOBA_W, page), page_map(t, off)) for t in range(bps) for off in (0, 1, 0, 1)]
    page_args = [a for _ in range(bps) for a in (ck, ck, cv, cv)]
    grid_spec = pltpu.PrefetchScalarGridSpec(
        num_scalar_prefetch=1,
        grid=(n_seq, n_steps + 1),
        in_specs=page_specs + [
                  pl.BlockSpec((None, L, MOBA_W), new_map(T_MQ)),
                  pl.BlockSpec((None, L, MOBA_W), new_map(T_MK)),
                  pl.BlockSpec((None, L, MOBA_W), new_map(T_MV))],
        out_specs=pl.BlockSpec((L, MOBA_W), lambda b, j, pt_ref: (b, 0)),
        scratch_shapes=[pltpu.VMEM((n_row, MOBA_W), BF16),
                        pltpu.VMEM((n_row, LANES), F32),
                        pltpu.VMEM((n_row, LANES), F32),
                        pltpu.VMEM((n_row, LANES), F32),
                        pltpu.VMEM((nbp, n_row, MOBA_W), F32)])
    return pl.pallas_call(
        body,
        grid_spec=grid_spec,
        out_shape=jax.ShapeDtypeStruct((n_seq * L, MOBA_W), F32),
        compiler_params=_cparams(("arbitrary", "arbitrary")),
        name="moba_sample",
    )(pt, *page_args, proj3, proj3, proj3)


def _post_body(og_ref, om_ref, ba_ref, bb_ref, x_ref, gt_ref, sc_ref, sh_ref, g2_ref, wpg_ref, wpm_ref,
               wo_ref, wr_ref, br_ref, x1_ref, h2_ref, lg_ref):
    ya = _dot(og_ref[...].astype(BF16), wpg_ref[...])
    yb = _dot(om_ref[...].astype(BF16), wpm_ref[...])
    ba = jnp.concatenate([ba_ref[0], ba_ref[1]], axis=1)
    bb = jnp.concatenate([bb_ref[0], bb_ref[1]], axis=1)
    mix = _sigmoid(ba) * ya + _sigmoid(bb) * yb
    x1 = x_ref[...] + gt_ref[0] * _dot(mix.astype(BF16), wo_ref[...])
    x1_ref[...] = x1
    ms = jnp.mean(x1 * x1, axis=-1, keepdims=True)
    h2 = x1 * lax.rsqrt(ms + EPS) * g2_ref[...] * (1.0 + sc_ref[0]) + sh_ref[0]
    _store_row_tiles(h2_ref, h2)
    hh = h2.astype(BF16)
    hl = (h2 - hh.astype(F32)).astype(BF16)
    wr = wr_ref[...]
    wh = wr.astype(BF16)
    wl = (wr - wh.astype(F32)).astype(BF16)
    lg_ref[...] = _dot(hh, wh) + _dot(hh, wl) + _dot(hl, wh) + br_ref[...]


def _post(og, om, proj3, x2, gt, sc, sh, g2, wpg, wpm, wo, wr, br, tm, rows_per_mod):
    r, d = x2.shape
    nt = r // tm
    tiles_per_mod = max(rows_per_mod // tm, 1)
    mod_block = (1,) + gt.shape[1:]
    mod_map = lambda i: (i // tiles_per_mod, 0, 0)
    full = lambda a: pl.BlockSpec(a.shape, lambda i: (0,) * a.ndim)
    return pl.pallas_call(
        _post_body,
        grid=(nt,),
        in_specs=[pl.BlockSpec((tm, og.shape[1]), lambda i: (i, 0)),
                  pl.BlockSpec((tm, om.shape[1]), lambda i: (i, 0)),
                  pl.BlockSpec((2, tm, PROJ_TILE), lambda i: (T_BA // 2, i, 0)),
                  pl.BlockSpec((2, tm, PROJ_TILE), lambda i: (T_BB // 2, i, 0)),
                  pl.BlockSpec((tm, d), lambda i: (i, 0)),
                  pl.BlockSpec(mod_block, mod_map),
                  pl.BlockSpec(mod_block, mod_map),
                  pl.BlockSpec(mod_block, mod_map),
                  full(g2), full(wpg), full(wpm), full(wo), full(wr), full(br)],
        out_specs=[pl.BlockSpec((tm, d), lambda i: (i, 0)),
                   pl.BlockSpec((tm * ROW_TILE, LANES), lambda i: (i, 0)),
                   pl.BlockSpec((tm, LANES), lambda i: (i, 0))],
        out_shape=[jax.ShapeDtypeStruct((r, d), F32),
                   jax.ShapeDtypeStruct((r * ROW_TILE, LANES), F32),
                   jax.ShapeDtypeStruct((r, LANES), F32)],
        compiler_params=_cparams(("arbitrary",)),
        name="post",
    )(og, om, proj3, proj3, x2, gt, sc, sh, g2, wpg, wpm, wo, wr, br)


def _route_body(lg_ref, eidx_ref, rank_ref, gate_ref, cnt_ref, run_scr):
    i = pl.program_id(0)
    tm = lg_ref.shape[0]

    @pl.when(i == 0)
    def _():
        run_scr[...] = jnp.zeros(run_scr.shape, F32)

    l = lg_ref[...]
    lane = lax.broadcasted_iota(I32, l.shape, 1)
    vals, hots = [], []
    for _ in range(TOP_K):
        m = jnp.max(l, axis=1, keepdims=True)
        idx = jnp.min(jnp.where(l == m, lane, LANES), axis=1, keepdims=True)
        hot = lane == idx
        vals.append(m)
        hots.append(hot)
        l = jnp.where(hot, NEG_INF, l)
    es = [jnp.exp(v - vals[0]) for v in vals]
    den = es[0] + es[1] + es[2] + es[3]
    chosen = jnp.zeros(l.shape, F32)
    for hot in hots:
        chosen = chosen + jnp.where(hot, 1.0, 0.0)
    row = lax.broadcasted_iota(I32, (tm, tm), 0)
    col = lax.broadcasted_iota(I32, (tm, tm), 1)
    before = jnp.where(row > col, 1.0, 0.0).astype(BF16)
    pos = _dot(before, chosen.astype(BF16)) + run_scr[...]
    eidx = jnp.zeros(l.shape, I32)
    rank = jnp.zeros(l.shape, I32)
    gate = jnp.zeros(l.shape, F32)
    for k in range(TOP_K):
        e_k = jnp.min(jnp.where(hots[k], lane, LANES), axis=1, keepdims=True)
        r_k = jnp.sum(jnp.where(hots[k], pos, 0.0), axis=1, keepdims=True).astype(I32)
        eidx = jnp.where(lane == k, e_k, eidx)
        rank = jnp.where(lane == k, r_k, rank)
        gate = jnp.where(lane == k, es[k] / den, gate)
    eidx_ref[...] = eidx
    rank_ref[...] = rank
    gate_ref[...] = gate
    run_scr[...] = run_scr[...] + jnp.sum(chosen, axis=0, keepdims=True)
    cnt_ref[...] = jnp.broadcast_to(run_scr[...], cnt_ref.shape)


def _route(logits, tm):
    t = logits.shape[0]
    blk = pl.BlockSpec((tm, LANES), lambda i: (i, 0))
    return pl.pallas_call(
        _route_body,
        grid=(t // tm,),
        in_specs=[blk],
        out_specs=[blk, blk, blk, pl.BlockSpec((8, LANES), lambda i: (0, 0))],
        out_shape=[jax.ShapeDtypeStruct((t, LANES), I32),
                   jax.ShapeDtypeStruct((t, LANES), I32),
                   jax.ShapeDtypeStruct((t, LANES), F32),
                   jax.ShapeDtypeStruct((8, LANES), F32)],
        scratch_shapes=[pltpu.VMEM((1, LANES), F32)],
        compiler_params=_cparams(("arbitrary",)),
        name="route",
    )(logits)


def _dispatch_body(dest_ref, h_ref, zero_ref, xb_ref, sem, *, tm):
    del zero_ref
    i = pl.program_id(0)

    def copy(r, k):
        dst = dest_ref[(i * tm + r) * TOP_K + k]
        return pltpu.make_async_copy(h_ref.at[pl.ds(pl.multiple_of(r * ROW_TILE, ROW_TILE), ROW_TILE), :],
                                     xb_ref.at[pl.ds(pl.multiple_of(dst * ROW_TILE, ROW_TILE), ROW_TILE), :], sem)

    def start(r, carry):
        for k in range(TOP_K):
            copy(r, k).start()
        return carry

    def wait(r, carry):
        for k in range(TOP_K):
            copy(r, k).wait()
        return carry

    lax.fori_loop(0, tm, start, 0)
    lax.fori_loop(0, tm, wait, 0)


def _dispatch(dest_flat, h_all, n_rows, tm):
    t = h_all.shape[0] // ROW_TILE
    zeros = jnp.zeros((n_rows * ROW_TILE, LANES), h_all.dtype)
    grid_spec = pltpu.PrefetchScalarGridSpec(
        num_scalar_prefetch=1,
        grid=(t // tm,),
        in_specs=[pl.BlockSpec((tm * ROW_TILE, LANES), lambda i, dr: (i, 0)), pl.BlockSpec(memory_space=pl.ANY)],
        out_specs=pl.BlockSpec(memory_space=pl.ANY),
        scratch_shapes=[pltpu.SemaphoreType.DMA(())])
    return pl.pallas_call(
        functools.partial(_dispatch_body, tm=tm),
        grid_spec=grid_spec,
        out_shape=jax.ShapeDtypeStruct((n_rows * ROW_TILE, LANES), h_all.dtype),
        input_output_aliases={2: 0},
        compiler_params=_cparams(("arbitrary",)),
        name="dispatch",
    )(dest_flat, h_all, zeros)


def _moe_body(be_ref, nu_ref, x_ref, wgu_ref, bgu_ref, wd_ref, bd_ref, y_ref, wgu_scr, wd_scr):
    i = pl.program_id(0)
    prev = be_ref[jnp.maximum(i - 1, 0)]
    fresh = jnp.logical_or(i == 0, be_ref[i] != prev)

    @pl.when(jnp.logical_and(fresh, i < nu_ref[0]))
    def _():
        wgu_scr[...] = wgu_ref[0].astype(BF16)
        wd_scr[...] = wd_ref[0].astype(BF16)

    @pl.when(i < nu_ref[0])
    def _():
        x = _load_row_tiles(x_ref, MOE_ROWS).astype(BF16)
        gu = _dot(x, wgu_scr[...]) + bgu_ref[0]
        glu = jnp.minimum(gu[:, :D_FF], SWIGLU_LIMIT)
        lin = jnp.clip(gu[:, D_FF:], -SWIGLU_LIMIT, SWIGLU_LIMIT)
        act = glu * _sigmoid(SWIGLU_ALPHA * glu) * (lin + 1.0)
        _store_row_tiles(y_ref, _dot(act.astype(BF16), wd_scr[...]) + bd_ref[0])

    @pl.when(i >= nu_ref[0])
    def _():
        y_ref[...] = jnp.zeros(y_ref.shape, F32)


def _moe(block_expert, n_used, xb, wgu, bgu, wd, bd):
    d = D_MODEL
    n_rows = xb.shape[0] // ROW_TILE
    nblk = n_rows // MOE_ROWS
    ne = wgu.shape[0]
    emap = lambda i, be, nu: (be[i], 0, 0)
    grid_spec = pltpu.PrefetchScalarGridSpec(
        num_scalar_prefetch=2,
        grid=(nblk,),
        in_specs=[pl.BlockSpec((MOE_ROWS * ROW_TILE, LANES), lambda i, be, nu: (i, 0)),
                  pl.BlockSpec((1, d, 2 * D_FF), emap),
                  pl.BlockSpec((1, 1, 2 * D_FF), emap),
                  pl.BlockSpec((1, D_FF, d), emap),
                  pl.BlockSpec((1, 1, d), emap)],
        out_specs=pl.BlockSpec((MOE_ROWS * ROW_TILE, LANES), lambda i, be, nu: (i, 0)),
        scratch_shapes=[pltpu.VMEM((d, 2 * D_FF), BF16), pltpu.VMEM((D_FF, d), BF16)])
    return pl.pallas_call(
        _moe_body,
        grid_spec=grid_spec,
        out_shape=jax.ShapeDtypeStruct((n_rows * ROW_TILE, LANES), F32),
        compiler_params=_cparams(("arbitrary",)),
        name="experts",
    )(block_expert, n_used, xb, wgu, bgu.reshape(ne, 1, 2 * D_FF), wd, bd.reshape(ne, 1, d))


def _combine_body(dest_ref, yb_ref, gate_ref, x1_ref, gt_ref, gf_ref, y_ref, buf, sem, *, tm, tok0):
    i = pl.program_id(0)

    def copy(r, k):
        src = dest_ref[(tok0 + i * tm + r) * TOP_K + k]
        return pltpu.make_async_copy(yb_ref.at[pl.ds(pl.multiple_of(src * ROW_TILE, ROW_TILE), ROW_TILE), :],
                                     buf.at[k, pl.ds(pl.multiple_of(r * ROW_TILE, ROW_TILE), ROW_TILE), :], sem)

    def start(r, carry):
        for k in range(TOP_K):
            copy(r, k).start()
        return carry

    def wait(r, carry):
        for k in range(TOP_K):
            copy(r, k).wait()
        return carry

    lax.fori_loop(0, tm, start, 0)
    lax.fori_loop(0, tm, wait, 0)
    g = gate_ref[...]
    y = g[:, 0:1] * _load_row_tiles(buf, tm, (0,))
    for k in range(1, TOP_K):
        y = y + g[:, k:k + 1] * _load_row_tiles(buf, tm, (k,))
    x2 = x1_ref[...] + gt_ref[0] * y
    ms = jnp.mean(x2 * x2, axis=-1, keepdims=True)
    y_ref[...] = x2 * lax.rsqrt(ms + EPS) * gf_ref[...]


def _combine(dest_flat, yb, gates, x1, gt, g_final, tm, rows_per_mod, tok0):
    r, d = x1.shape
    nt = r // tm
    tiles_per_mod = max(rows_per_mod // tm, 1)
    tile0 = tok0 // tm
    mod_block = (1,) + gt.shape[1:]
    grid_spec = pltpu.PrefetchScalarGridSpec(
        num_scalar_prefetch=1,
        grid=(nt,),
        in_specs=[pl.BlockSpec(memory_space=pl.ANY),
                  pl.BlockSpec((tm, LANES), lambda i, dr: (tile0 + i, 0)),
                  pl.BlockSpec((tm, d), lambda i, dr: (i, 0)),
                  pl.BlockSpec(mod_block, lambda i, dr: (i // tiles_per_mod, 0, 0)),
                  pl.BlockSpec((1, d), lambda i, dr: (0, 0))],
        out_specs=pl.BlockSpec((tm, d), lambda i, dr: (i, 0)),
        scratch_shapes=[pltpu.VMEM((TOP_K, tm * ROW_TILE, LANES), F32), pltpu.SemaphoreType.DMA(())])
    return pl.pallas_call(
        functools.partial(_combine_body, tm=tm, tok0=tok0),
        grid_spec=grid_spec,
        out_shape=jax.ShapeDtypeStruct((r, d), F32),
        compiler_params=_cparams(("arbitrary",)),
        name="combine",
    )(dest_flat, yb, gates, x1, gt, g_final)


def _rope_tables(pos):
    half = MOBA_HD // 2
    inv = 1.0 / (ROPE_THETA ** (jnp.arange(half, dtype=F32) / half))
    ang = pos.astype(F32)[:, None] * inv[None, :]
    cos = jnp.cos(ang)
    sin = jnp.sin(ang)
    cos_h = jnp.concatenate([cos, cos], axis=1)
    sin_h = jnp.concatenate([-sin, sin], axis=1)
    return jnp.tile(cos_h, (1, MOBA_HEADS)), jnp.tile(sin_h, (1, MOBA_HEADS))


def _row_tile(n, cap):
    t = min(n, cap)
    while n % t:
        t //= 2
    return t


def kernel(x_prompt, x_sample, cache_k, cache_v, state_gla, page_table, c_prompt, c_sample, w_ada, b_ada, g_norm1, w_in, w_gla_a2, b_gla_a2, g_gla_head, w_proj_gla, w_proj_moba, w_out, g_norm2, w_router, b_router, w_gate_up, b_gate_up, w_down, b_down, g_final):
    nb_p, seq, d = x_prompt.shape
    nb_s, dec = x_sample.shape[:2]
    past_len = page_table.shape[1] * cache_k.shape[2]
    assert w_ada.shape[0] == 1 and d == D_MODEL
    assert seq % MOBA_BLOCK == 0 and past_len % MOBA_BLOCK == 0 and dec <= MOBA_BLOCK and dec % 8 == 0
    r_p, r_s = nb_p * seq, nb_s * dec
    t_all = r_p + r_s

    c_all = jnp.concatenate([c_prompt, c_sample], axis=0)
    mod = _ada(c_all, w_ada[0], b_ada[0]).reshape(nb_p + nb_s, 6, d)
    mod_p = [mod[:nb_p, k].reshape(nb_p, 1, d) for k in range(6)]
    mod_s = [jnp.broadcast_to(mod[nb_p:, k][:, None, :], (nb_s, dec, d)).reshape(1, r_s, d) for k in range(6)]

    w = w_in[0]
    gkw = GLA_HEADS * GLA_DK
    gvw = GLA_HEADS * GLA_DV
    c_ga = 2 * gkw + 2 * gvw
    pad = jnp.zeros((d, PROJ_TILE - GLA_LOWRANK), w.dtype)
    w_in_p = jnp.concatenate([w[:, :c_ga + GLA_LOWRANK], pad, w[:, c_ga + GLA_LOWRANK:]], axis=1).astype(BF16)
    assert w_in_p.shape[1] == N_PROJ_TILES * PROJ_TILE
    w2_p = jnp.zeros((PROJ_TILE, gkw), BF16).at[:GLA_LOWRANK].set(w_gla_a2[0].astype(BF16))
    b2 = b_gla_a2[0].reshape(1, gkw)
    g1 = g_norm1[0].reshape(1, d)
    g2 = g_norm2[0].reshape(1, d)
    wpg = w_proj_gla[0].astype(BF16)
    wpm = w_proj_moba[0].astype(BF16)
    wo = w_out[0].astype(BF16)
    wr = jnp.zeros((d, LANES), F32).at[:, :N_EXPERTS].set(w_router[0])
    br = jnp.full((1, LANES), NEG_INF, F32).at[0, :N_EXPERTS].set(b_router[0])
    g_head = g_gla_head[0].reshape(1, GLA_DV)

    cos_p, sin_p = _rope_tables(jnp.arange(seq, dtype=I32))
    cos_s, sin_s = _rope_tables(past_len + jnp.arange(dec, dtype=I32))
    cos_s, sin_s = jnp.tile(cos_s, (nb_s, 1)), jnp.tile(sin_s, (nb_s, 1))

    xp2 = x_prompt.reshape(r_p, d)
    xs2 = x_sample.reshape(r_s, d)
    tm_p = _row_tile(seq, 1024)

    proj_p, kt_p, vt_p = _inproj(xp2, mod_p[1], mod_p[0], g1, w_in_p, w2_p, b2, cos_p, sin_p, tm_p, seq, True)
    s0_p = jnp.zeros((nb_p, GLA_HEADS, GLA_DK, GLA_DV), state_gla.dtype)
    og_p, st_p = _gla(proj_p, s0_p, g_head, nb_p, seq, _row_tile(seq, 64))
    om_p = _moba_prompt(proj_p, vt_p, nb_p, seq)
    tm_post = _row_tile(seq, 512)
    x1_p, h2_p, lg_p = _post(og_p, om_p, proj_p, xp2, mod_p[2], mod_p[4], mod_p[3], g2, wpg, wpm, wo, wr, br,
                             tm_post, seq)

    proj_s, k5_s, v5_s = _inproj(xs2, mod_s[1], mod_s[0], g1, w_in_p, w2_p, b2, cos_s, sin_s, r_s, r_s, False)
    og_s, st_s = _gla(proj_s, state_gla[0], g_head, nb_s, dec, dec)
    om_s = _moba_sample(proj_s, cache_k[0], cache_v[0], page_table, nb_s, dec)
    x1_s, h2_s, lg_s = _post(og_s, om_s, proj_s, xs2, mod_s[2], mod_s[4], mod_s[3], g2, wpg, wpm, wo, wr, br,
                             r_s, r_s)

    h_all = jnp.concatenate([h2_p, h2_s], axis=0)
    lg_all = jnp.concatenate([lg_p, lg_s], axis=0)
    tm_r = _row_tile(t_all, 256)
    eidx, rank, gates, cnt = _route(lg_all, tm_r)
    counts = cnt[0, :N_EXPERTS].astype(I32)
    nblk_e = (counts + MOE_ROWS - 1) // MOE_ROWS
    blk_end = jnp.cumsum(nblk_e)
    blk_start = blk_end - nblk_e
    dest = blk_start[eidx[:, :TOP_K]] * MOE_ROWS + rank[:, :TOP_K]
    dest_flat = dest.reshape(-1).astype(I32)
    n_blocks = -(-(t_all * TOP_K) // MOE_ROWS) + N_EXPERTS
    blk_ids = jnp.arange(n_blocks, dtype=I32)
    block_expert = jnp.minimum(jnp.sum((blk_end[None, :] <= blk_ids[:, None]).astype(I32), axis=1), N_EXPERTS - 1)
    n_used = blk_end[-1:].astype(I32)
    xb = _dispatch(dest_flat, h_all, n_blocks * MOE_ROWS, tm_r)
    yb = _moe(block_expert, n_used, xb, w_gate_up[0], b_gate_up[0], w_down[0], b_down[0])
    gf = g_final.reshape(1, d)
    tm_c = _row_tile(seq, 256)
    y_p = _combine(dest_flat, yb, gates, x1_p, mod_p[5], gf, tm_c, seq, 0)
    y_s = _combine(dest_flat, yb, gates, x1_s, mod_s[5], gf, r_s, r_s, r_p)

    to_out = lambda a: a.reshape(1, nb_p, MOBA_HEADS, MOBA_HD, seq).transpose(0, 1, 4, 2, 3)
    k_p = to_out(kt_p)
    v_p = to_out(vt_p)
    k_s = k5_s.reshape(1, nb_s, dec, MOBA_HEADS, MOBA_HD)
    v_s = v5_s.reshape(1, nb_s, dec, MOBA_HEADS, MOBA_HD)
    return (y_p.reshape(nb_p, seq, d), y_s.reshape(nb_s, dec, d), k_p, v_p, st_p[None],
            k_s, v_s, st_s[None])
```

```python
import functools

import jax
import jax.numpy as jnp
from jax import lax
from jax.experimental import pallas as pl
from jax.experimental.pallas import tpu as pltpu

F32 = jnp.float32
BF16 = jnp.bfloat16
I32 = jnp.int32

D_MODEL = 1024
GLA_HEADS = 4
GLA_DK = 128
GLA_DV = 256
GLA_LOWRANK = 16
GLA_GATE_NORM = 16.0
GLA_SUB = 16
MOBA_HEADS = 8
MOBA_HD = 64
MOBA_BLOCK = 256
MOBA_TOPK = 3
MOBA_W = MOBA_HEADS * MOBA_HD
ROPE_THETA = 10000.0
N_EXPERTS = 32
TOP_K = 4
D_FF = D_MODEL
SWIGLU_LIMIT = 7.0
SWIGLU_ALPHA = 1.702
EPS = 1e-6
LANES = 128
PROJ_TILE = 512

T_GQ, T_GK, T_GV, T_GR, T_GA, T_MQ, T_MK, T_MV, T_BA, T_BB = 0, 1, 2, 4, 6, 7, 8, 9, 10, 12
N_PROJ_TILES = 14
MOE_ROWS = 512
MOBA_S_BLOCKS_PER_STEP = 8
VMEM_LIMIT = 56 * 1024 * 1024

NEG_INF = float("-inf")


def _sigmoid(x):
    return 1.0 / (1.0 + jnp.exp(-x))


def _split3(x):
    hi = x.astype(BF16)
    r = x - hi.astype(F32)
    mid = r.astype(BF16)
    lo = (r - mid.astype(F32)).astype(BF16)
    return hi, mid, lo


def _dot(a, b):
    return jnp.dot(a, b, preferred_element_type=F32)


def _dot_nt(a, b):
    return lax.dot_general(a, b, (((1,), (1,)), ((), ())), preferred_element_type=F32)


def _dot_nt_f32(a, b):
    ah = a.astype(BF16)
    al = (a - ah.astype(F32)).astype(BF16)
    bh = b.astype(BF16)
    bl = (b - bh.astype(F32)).astype(BF16)
    return _dot_nt(ah, bh) + _dot_nt(ah, bl) + _dot_nt(al, bh)


def _div(x, n):
    assert n & (n - 1) == 0
    return lax.shift_right_logical(x, n.bit_length() - 1)


def _mod(x, n):
    assert n & (n - 1) == 0
    return x & (n - 1)


ROW_TILE = D_MODEL // LANES


def _store_row_tiles(ref, val):
    n = val.shape[0]
    for s in range(ROW_TILE):
        ref[pl.ds(s, n, stride=ROW_TILE), :] = val[:, s * LANES:(s + 1) * LANES]


def _load_row_tiles(ref, n, lead=()):
    return jnp.concatenate([ref[lead + (pl.ds(s, n, stride=ROW_TILE), slice(None))] for s in range(ROW_TILE)],
                           axis=1)


def _cparams(sem):
    return pltpu.CompilerParams(dimension_semantics=sem, vmem_limit_bytes=VMEM_LIMIT)


def _ada_body(c_ref, w_ref, b_ref, o_ref):
    c = c_ref[...]
    s = c * _sigmoid(c)
    o_ref[...] = _dot(s.astype(BF16), w_ref[...].astype(BF16)) + b_ref[...]


def _ada(c, w, b):
    n = w.shape[1]
    tn = n // 4
    return pl.pallas_call(
        _ada_body,
        grid=(4,),
        in_specs=[pl.BlockSpec(c.shape, lambda j: (0, 0)),
                  pl.BlockSpec((w.shape[0], tn), lambda j: (0, j)),
                  pl.BlockSpec((1, tn), lambda j: (0, j))],
        out_specs=pl.BlockSpec((c.shape[0], tn), lambda j: (0, j)),
        out_shape=jax.ShapeDtypeStruct((c.shape[0], n), F32),
        compiler_params=_cparams(("arbitrary",)),
        name="ada",
    )(c, w, b.reshape(1, n))


def _rotary(x, cos, sin_signed):
    lane = lax.broadcasted_iota(I32, x.shape, 1)
    half = MOBA_HD // 2
    partner = jnp.where((lane & (MOBA_HD - 1)) < half, lane + half, lane - half)
    w = x.shape[1]
    r1 = pltpu.roll(x, half, 1)
    i1 = pltpu.roll(lane, half, 1)
    r2 = pltpu.roll(x, w - half, 1)
    swapped = jnp.where(i1 == partner, r1, r2)
    return x * cos + swapped * sin_signed


def _log_sigmoid(x):
    return jnp.minimum(x, 0.0) - jnp.log(1.0 + jnp.exp(-jnp.abs(x)))


def _store_heads(ref, val, token_minor):
    if token_minor:
        ref[0] = val.T
        return
    tm = val.shape[0]
    for h in range(MOBA_HEADS):
        ref[pl.ds(h, tm, stride=MOBA_HEADS), :] = val[:, h * MOBA_HD:(h + 1) * MOBA_HD]


def _inproj_body(x_ref, sc_ref, sh_ref, g_ref, w_ref, w2_ref, b2_ref, cos_ref, sin_ref, o_ref, k5_ref, v5_ref,
                 h_scr, *, token_minor):
    j = pl.program_id(1)

    @pl.when(j == 0)
    def _():
        x = x_ref[...]
        ms = jnp.mean(x * x, axis=-1, keepdims=True)
        y = x * lax.rsqrt(ms + EPS) * g_ref[...]
        h = y * (1.0 + sc_ref[0]) + sh_ref[0]
        h_scr[...] = h.astype(BF16)

    acc = _dot(h_scr[...], w_ref[...])
    is_la = j == T_GA
    special = functools.reduce(jnp.logical_or, [j == T_MQ, j == T_MK, j == T_MV, is_la])

    @pl.when(j == T_MQ)
    def _():
        o_ref[0] = _rotary(acc, cos_ref[...], sin_ref[...])

    @pl.when(j == T_MK)
    def _():
        rot = _rotary(acc, cos_ref[...], sin_ref[...])
        o_ref[0] = rot
        _store_heads(k5_ref, rot, token_minor)

    @pl.when(j == T_MV)
    def _():
        o_ref[0] = acc
        _store_heads(v5_ref, acc, token_minor)

    @pl.when(is_la)
    def _():
        z = _dot(acc.astype(BF16), w2_ref[...]) + b2_ref[...]
        o_ref[0] = _log_sigmoid(z) * (1.0 / GLA_GATE_NORM)

    @pl.when(jnp.logical_not(special))
    def _():
        o_ref[0] = acc


def _inproj(x2, sc, sh, g1, w_in_p, w2_p, b2, cos, sin, tm, rows_per_mod, token_minor):
    r, d = x2.shape
    nt = r // tm
    tiles_per_mod = max(rows_per_mod // tm, 1)
    tiles_per_tab = cos.shape[0] // tm
    mod_block = (1,) + sc.shape[1:]
    if token_minor:
        seq = cos.shape[0]
        kv_spec = pl.BlockSpec((1, MOBA_W, tm), lambda i, j: (i // tiles_per_tab, 0, i % tiles_per_tab))
        kv_shape = jax.ShapeDtypeStruct((r // seq, MOBA_W, seq), F32)
    else:
        kv_spec = pl.BlockSpec((tm * MOBA_HEADS, MOBA_HD), lambda i, j: (i, 0))
        kv_shape = jax.ShapeDtypeStruct((r * MOBA_HEADS, MOBA_HD), F32)
    return pl.pallas_call(
        functools.partial(_inproj_body, token_minor=token_minor),
        grid=(nt, N_PROJ_TILES),
        in_specs=[pl.BlockSpec((tm, d), lambda i, j: (i, 0)),
                  pl.BlockSpec(mod_block, lambda i, j: (i // tiles_per_mod, 0, 0)),
                  pl.BlockSpec(mod_block, lambda i, j: (i // tiles_per_mod, 0, 0)),
                  pl.BlockSpec((1, d), lambda i, j: (0, 0)),
                  pl.BlockSpec((d, PROJ_TILE), lambda i, j: (0, j)),
                  pl.BlockSpec((PROJ_TILE, PROJ_TILE), lambda i, j: (0, 0)),
                  pl.BlockSpec((1, PROJ_TILE), lambda i, j: (0, 0)),
                  pl.BlockSpec((tm, PROJ_TILE), lambda i, j: (i % tiles_per_tab, 0)),
                  pl.BlockSpec((tm, PROJ_TILE), lambda i, j: (i % tiles_per_tab, 0))],
        out_specs=[pl.BlockSpec((1, tm, PROJ_TILE), lambda i, j: (j, i, 0)), kv_spec, kv_spec],
        out_shape=[jax.ShapeDtypeStruct((N_PROJ_TILES, r, PROJ_TILE), F32), kv_shape, kv_shape],
        scratch_shapes=[pltpu.VMEM((tm, d), BF16)],
        compiler_params=_cparams(("arbitrary", "arbitrary")),
        name="inproj",
    )(x2, sc, sh, g1, w_in_p, w2_p, b2, cos, sin)


def _gla_body(q_ref, k_ref, v_ref, la_ref, gr_ref, s0_ref, gh_ref, og_ref, sout_ref, st_scr, *, C, SB):
    c = pl.program_id(1)
    nc = pl.num_programs(1)

    @pl.when(c == 0)
    def _():
        for h in range(GLA_HEADS):
            st_scr[h] = s0_ref[0, h].T

    la = la_ref[0]
    row_c = lax.broadcasted_iota(I32, (C, C), 0)
    col_c = lax.broadcasted_iota(I32, (C, C), 1)
    tri = jnp.where(row_c >= col_c, 1.0, 0.0).astype(BF16)
    hi, mid, lo = _split3(la)
    b_all = _dot(tri, hi) + _dot(tri, mid) + _dot(tri, lo)
    q_all = q_ref[0] * (GLA_DK ** -0.5)
    k_all = k_ref[0]
    gh = gh_ref[...]
    row_k = lax.broadcasted_iota(I32, (C, GLA_DK), 0)
    row_grp = lax.broadcasted_iota(I32, (8, 1), 0)
    lane_grp = lax.broadcasted_iota(I32, (8, C), 1)

    for h in range(GLA_HEADS):
        ks = slice(h * GLA_DK, (h + 1) * GLA_DK)
        vs = slice((h % 2) * GLA_DV, (h % 2 + 1) * GLA_DV)
        b = b_all[:, ks]
        q = q_all[:, ks]
        k = k_all[:, ks]
        v = v_ref[h // 2][:, vs]
        st = st_scr[h]
        b_last = b[C - 1:C, :]
        o = _dot_nt((q * jnp.exp(b)).astype(BF16), st.astype(BF16))
        att_rows = []
        for i in range(C // SB):
            r0 = i * SB
            bi = b[r0:r0 + SB]
            qi = q[r0:r0 + SB]
            ki = k[r0:r0 + SB]
            if i > 0:
                bref = b[r0 - 1:r0, :]
                qt = qi * jnp.exp(bi - bref)
                kt = jnp.where(row_k < r0, k * jnp.exp(jnp.minimum(bref - b, 0.0)), 0.0)
                att_i = _dot_nt(qt.astype(BF16), kt.astype(BF16))
            else:
                att_i = jnp.zeros((SB, C), F32)
            for lo in range(0, SB, 8):
                up = min(lo + 8, SB)
                part = att_i[lo:up]
                for s in range(up):
                    dec = jnp.exp(jnp.minimum(bi[lo:up] - bi[s:s + 1, :], 0.0))
                    col = jnp.sum(qi[lo:up] * ki[s:s + 1, :] * dec, axis=1, keepdims=True)
                    col = jnp.where(row_grp + lo >= s, col, 0.0)
                    part = part + jnp.where(lane_grp == r0 + s, col, 0.0)
                att_rows.append(part)
        att = att_rows[0] if len(att_rows) == 1 else jnp.concatenate(att_rows, axis=0)
        vb = v.astype(BF16)
        o = o + _dot(att.astype(BF16), vb)
        kd = k * jnp.exp(b_last - b)
        st_scr[h] = st * jnp.exp(b_last) + _dot(v.T.astype(BF16), kd.astype(BF16))
        on = o * lax.rsqrt(jnp.mean(o * o, axis=-1, keepdims=True) + EPS) * gh
        gr = gr_ref[h // 2][:, vs]
        og_ref[:, h * GLA_DV:(h + 1) * GLA_DV] = (on * (gr * _sigmoid(gr))).astype(og_ref.dtype)

    @pl.when(c == nc - 1)
    def _():
        for h in range(GLA_HEADS):
            sout_ref[0, h] = st_scr[h].T


def _gla(proj3, s0, g_head, n_seq, seq_len, chunk):
    r = proj3.shape[1]
    nc = seq_len // chunk
    sb = min(GLA_SUB, chunk)
    out_dtype = BF16 if chunk % 16 == 0 else F32
    row = lambda b, c: b * nc + c
    body = functools.partial(_gla_body, C=chunk, SB=sb)
    return pl.pallas_call(
        body,
        grid=(n_seq, nc),
        in_specs=[pl.BlockSpec((1, chunk, PROJ_TILE), lambda b, c: (T_GQ, row(b, c), 0)),
                  pl.BlockSpec((1, chunk, PROJ_TILE), lambda b, c: (T_GK, row(b, c), 0)),
                  pl.BlockSpec((2, chunk, PROJ_TILE), lambda b, c: (T_GV // 2, row(b, c), 0)),
                  pl.BlockSpec((1, chunk, PROJ_TILE), lambda b, c: (T_GA, row(b, c), 0)),
                  pl.BlockSpec((2, chunk, PROJ_TILE), lambda b, c: (T_GR // 2, row(b, c), 0)),
                  pl.BlockSpec((1, GLA_HEADS, GLA_DK, GLA_DV), lambda b, c: (b, 0, 0, 0)),
                  pl.BlockSpec((1, GLA_DV), lambda b, c: (0, 0))],
        out_specs=[pl.BlockSpec((chunk, GLA_HEADS * GLA_DV), lambda b, c: (row(b, c), 0)),
                   pl.BlockSpec((1, GLA_HEADS, GLA_DK, GLA_DV), lambda b, c: (b, 0, 0, 0))],
        out_shape=[jax.ShapeDtypeStruct((r, GLA_HEADS * GLA_DV), out_dtype),
                   jax.ShapeDtypeStruct((n_seq, GLA_HEADS, GLA_DK, GLA_DV), F32)],
        scratch_shapes=[pltpu.VMEM((GLA_HEADS, GLA_DV, GLA_DK), F32)],
        compiler_params=_cparams(("arbitrary", "arbitrary")),
        name="gla",
    )(proj3, proj3, proj3, proj3, proj3, s0, g_head)


def _select_topk(g, n_valid_rows, n_rows, n_sel):
    row = lax.broadcasted_iota(I32, g.shape, 0)
    valid = row < n_valid_rows
    gm = jnp.where(valid, g, NEG_INF)
    rank = jnp.zeros(g.shape, F32)
    for jp in range(n_rows):
        gj = gm[jp:jp + 1, :]
        ahead = jnp.logical_or(gj > gm, jnp.logical_and(gj == gm, jp < row))
        rank = rank + jnp.where(ahead, 1.0, 0.0)
    return jnp.where(jnp.logical_and(valid, rank < n_sel), 1.0, 0.0)


def _moba_p_body(q_ref, k_ref, v_ref, o_ref, k_scr, vt_scr, km_scr, sel_scr, m_scr, l_scr, acc_scr, *, nb, nbp):
    i = pl.program_id(1)
    blk = MOBA_BLOCK
    scale = MOBA_HD ** -0.5

    @pl.when(i == 0)
    def _():
        km_scr[...] = jnp.zeros(km_scr.shape, F32)
        for j in range(nb):
            kj = k_ref[0, j * blk:(j + 1) * blk, :]
            k_scr[j] = kj.astype(BF16)
            km_scr[j:j + 1, :] = jnp.mean(kj, axis=0, keepdims=True)
            vt_scr[j] = v_ref[0, :, j * blk:(j + 1) * blk].astype(BF16)

    q = q_ref[0]
    km = km_scr[...]
    km_rep = jnp.concatenate([km] * MOBA_HEADS, axis=0)
    wrow = lax.broadcasted_iota(I32, km_rep.shape, 0)
    wlane = lax.broadcasted_iota(I32, km_rep.shape, 1)
    wt = jnp.where(_div(wlane, MOBA_HD) == _div(wrow, nbp), km_rep, 0.0)
    gates_t = _dot_nt_f32(wt, q)
    for h in range(MOBA_HEADS):
        sel_scr[h] = _select_topk(gates_t[h * nbp:(h + 1) * nbp, :], i, nb, MOBA_TOPK)

    n_pair = MOBA_HEADS // 2
    pair_lane = lax.broadcasted_iota(I32, (blk, LANES), 1)
    key_row = lax.broadcasted_iota(I32, (blk, 2 * blk), 0)
    q_col = _mod(lax.broadcasted_iota(I32, (blk, 2 * blk), 1), blk)
    qms = []
    for pr in range(n_pair):
        qp = q[:, pr * LANES:(pr + 1) * LANES]
        qms.append(jnp.concatenate([jnp.where(_div(pair_lane, MOBA_HD) == u, qp, 0.0) for u in range(2)],
                                   axis=0).astype(BF16))

    def scores(j, pr):
        kj = k_scr[j, :, pr * LANES:(pr + 1) * LANES]
        return _dot_nt(kj, qms[pr]) * scale

    def values(j, h):
        return vt_scr[j, h * MOBA_HD:(h + 1) * MOBA_HD, :]

    for pr in range(n_pair):
        s = jnp.where(key_row <= q_col, scores(i, pr), NEG_INF)
        m = jnp.max(s, axis=0, keepdims=True)
        p = jnp.exp(s - m)
        m_scr[pr] = m
        l_scr[pr] = jnp.sum(p, axis=0, keepdims=True)
        pb = p.astype(BF16)
        for u in range(2):
            acc_scr[2 * pr + u] = _dot(values(i, 2 * pr + u), pb[:, u * blk:(u + 1) * blk])

    def step(j, carry):
        for pr in range(n_pair):
            keep = jnp.concatenate([sel_scr[2 * pr + u, pl.ds(j, 1), :] for u in range(2)], axis=1)
            s = jnp.where(keep > 0.5, scores(j, pr), NEG_INF)
            m_old = m_scr[pr]
            m_new = jnp.maximum(m_old, jnp.max(s, axis=0, keepdims=True))
            alpha = jnp.exp(m_old - m_new)
            p = jnp.exp(s - m_new)
            m_scr[pr] = m_new
            l_scr[pr] = l_scr[pr] * alpha + jnp.sum(p, axis=0, keepdims=True)
            pb = p.astype(BF16)
            for u in range(2):
                h = 2 * pr + u
                cs = slice(u * blk, (u + 1) * blk)
                acc_scr[h] = acc_scr[h] * alpha[:, cs] + _dot(values(j, h), pb[:, cs])
        return carry

    lax.fori_loop(0, i, step, 0)
    outs = []
    for h in range(MOBA_HEADS):
        l = l_scr[h // 2][:, (h % 2) * blk:(h % 2 + 1) * blk]
        outs.append(acc_scr[h] / l)
    out_t = jnp.concatenate(outs, axis=0)
    o_ref[...] = out_t.T.astype(o_ref.dtype)


def _moba_prompt(proj3, v_t, n_seq, seq_len):
    r = proj3.shape[1]
    nb = seq_len // MOBA_BLOCK
    nbp = -(-nb // 8) * 8
    body = functools.partial(_moba_p_body, nb=nb, nbp=nbp)
    return pl.pallas_call(
        body,
        grid=(n_seq, nb),
        in_specs=[pl.BlockSpec((1, MOBA_BLOCK, MOBA_W), lambda b, i: (T_MQ, b * nb + i, 0)),
                  pl.BlockSpec((1, seq_len, MOBA_W), lambda b, i: (T_MK, b, 0)),
                  pl.BlockSpec((1, MOBA_W, seq_len), lambda b, i: (b, 0, 0))],
        out_specs=pl.BlockSpec((MOBA_BLOCK, MOBA_W), lambda b, i: (b * nb + i, 0)),
        out_shape=jax.ShapeDtypeStruct((r, MOBA_W), BF16),
        scratch_shapes=[pltpu.VMEM((nb, MOBA_BLOCK, MOBA_W), BF16),
                        pltpu.VMEM((nb, MOBA_W, MOBA_BLOCK), BF16),
                        pltpu.VMEM((nbp, MOBA_W), F32),
                        pltpu.VMEM((MOBA_HEADS, nbp, MOBA_BLOCK), F32),
                        pltpu.VMEM((MOBA_HEADS // 2, 1, 2 * MOBA_BLOCK), F32),
                        pltpu.VMEM((MOBA_HEADS // 2, 1, 2 * MOBA_BLOCK), F32),
                        pltpu.VMEM((MOBA_HEADS, MOBA_HD, MOBA_BLOCK), F32)],
        compiler_params=_cparams(("arbitrary", "arbitrary")),
        name="moba_prompt",
    )(proj3, proj3, v_t)


def _moba_s_body(pt_ref, *refs, nbp, L, bps):
    del pt_ref
    page_refs = refs[:4 * bps]
    qn_ref, kn_ref, vn_ref, o_ref, q2_scr, gate_scr, m_scr, l_scr, acc_scr = refs[4 * bps:]
    j = pl.program_id(1)
    n_steps = nbp // bps
    scale = MOBA_HD ** -0.5
    n_row = MOBA_HEADS * L
    blk_lane = lax.broadcasted_iota(I32, (n_row, LANES), 1)

    @pl.when(j == 0)
    def _():
        qn = qn_ref[...]
        q2 = jnp.concatenate([qn] * MOBA_HEADS, axis=0)
        row = lax.broadcasted_iota(I32, q2.shape, 0)
        lane = lax.broadcasted_iota(I32, q2.shape, 1)
        q2_scr[...] = jnp.where(_div(lane, MOBA_HD) == _div(row, L), q2, 0.0).astype(BF16)
        gate_scr[...] = jnp.full(gate_scr.shape, NEG_INF, F32)
        m_scr[...] = jnp.zeros(m_scr.shape, F32)
        l_scr[...] = jnp.zeros(l_scr.shape, F32)

    def softmax_partial(s):
        m = jnp.max(s, axis=1, keepdims=True)
        p = jnp.exp(s - m)
        return m, jnp.sum(p, axis=1, keepdims=True), p.astype(BF16)

    @pl.when(j < n_steps)
    def _():
        for t in range(bps):
            ka_ref, kb_ref, va_ref, vb_ref = page_refs[4 * t:4 * t + 4]
            jj = j * bps + t
            kt = jnp.concatenate([ka_ref[0], kb_ref[0]], axis=1).astype(BF16)
            vt = jnp.concatenate([va_ref[0], vb_ref[0]], axis=1).astype(BF16)
            s = _dot(q2_scr[...], kt)
            gate = jnp.mean(s, axis=1, keepdims=True)
            m, l, p = softmax_partial(s * scale)
            hot = blk_lane == jj
            gate_scr[...] = jnp.where(hot, gate, gate_scr[...])
            m_scr[...] = jnp.where(hot, m, m_scr[...])
            l_scr[...] = jnp.where(hot, l, l_scr[...])
            acc_scr[jj] = _dot_nt(p, vt)

    @pl.when(j == n_steps)
    def _():
        pad = jnp.zeros((LANES - L, MOBA_W), F32)
        kn = jnp.concatenate([kn_ref[...], pad], axis=0).astype(BF16)
        vn = jnp.concatenate([vn_ref[...], pad], axis=0).astype(BF16)
        s = _dot_nt(q2_scr[...], kn) * scale
        key = lax.broadcasted_iota(I32, s.shape, 1)
        qpos = _mod(lax.broadcasted_iota(I32, s.shape, 0), L)
        m_own, l_own, p_own = softmax_partial(jnp.where(key <= qpos, s, NEG_INF))
        acc_own = _dot(p_own, vn)
        g = gate_scr[...]
        rank = jnp.zeros(g.shape, F32)
        for jp in range(nbp):
            gj = g[:, jp:jp + 1]
            ahead = jnp.logical_or(gj > g, jnp.logical_and(gj == g, jp < blk_lane))
            rank = rank + jnp.where(ahead, 1.0, 0.0)
        sel = jnp.logical_and(blk_lane < nbp, rank < min(MOBA_TOPK, nbp + 1))
        m_all = m_scr[...]
        m_top = jnp.maximum(m_own, jnp.max(jnp.where(sel, m_all, NEG_INF), axis=1, keepdims=True))
        w = jnp.where(sel, jnp.exp(m_all - m_top), 0.0)
        w_own = jnp.exp(m_own - m_top)
        den = jnp.sum(w * l_scr[...], axis=1, keepdims=True) + w_own * l_own
        num = w_own * acc_own
        for jj in range(nbp):
            num = num + w[:, jj:jj + 1] * acc_scr[jj]
        out_r = num / den
        lane = lax.broadcasted_iota(I32, (L, MOBA_W), 1)
        out = jnp.zeros((L, MOBA_W), F32)
        for h in range(MOBA_HEADS):
            out = out + jnp.where(_div(lane, MOBA_HD) == h, out_r[h * L:(h + 1) * L, :], 0.0)
        o_ref[...] = out


def _moba_sample(proj3, cache_k, cache_v, page_table, n_seq, L):
    n_pool, page = cache_k.shape[0], cache_k.shape[1]
    n_pages = page_table.shape[1]
    ppb = MOBA_BLOCK // page
    nbp = n_pages // ppb
    n_row = MOBA_HEADS * L
    assert ppb == 2 and n_pages % ppb == 0 and page == LANES and nbp <= LANES and n_row % 16 == 0 and L <= LANES
    ck = jnp.transpose(cache_k, (0, 2, 3, 1)).reshape(n_pool, MOBA_W, page)
    cv = jnp.transpose(cache_v, (0, 2, 3, 1)).reshape(n_pool, MOBA_W, page)
    pt = page_table.reshape(-1).astype(I32)

    bps = MOBA_S_BLOCKS_PER_STEP if nbp % MOBA_S_BLOCKS_PER_STEP == 0 else 1
    n_steps = nbp // bps

    def page_map(t, off):
        return lambda b, j, pt_ref: (
            pt_ref[b * n_pages + ppb * (jnp.minimum(j, n_steps - 1) * bps + t) + off], 0, 0)

    new_map = lambda t: (lambda b, j, pt_ref: (t, b, 0))
    body = functools.partial(_moba_s_body, nbp=nbp, L=L, bps=bps)
    page_specs = [pl.BlockSpec((1, MOBA_W, page), page_map(t, off)) for t in range(bps) for off in (0, 1, 0, 1)]
    page_args = [a for _ in range(bps) for a in (ck, ck, cv, cv)]
    grid_spec = pltpu.PrefetchScalarGridSpec(
        num_scalar_prefetch=1,
        grid=(n_seq, n_steps + 1),
        in_specs=page_specs + [
                  pl.BlockSpec((None, L, MOBA_W), new_map(T_MQ)),
                  pl.BlockSpec((None, L, MOBA_W), new_map(T_MK)),
                  pl.BlockSpec((None, L, MOBA_W), new_map(T_MV))],
        out_specs=pl.BlockSpec((L, MOBA_W), lambda b, j, pt_ref: (b, 0)),
        scratch_shapes=[pltpu.VMEM((n_row, MOBA_W), BF16),
                        pltpu.VMEM((n_row, LANES), F32),
                        pltpu.VMEM((n_row, LANES), F32),
                        pltpu.VMEM((n_row, LANES), F32),
                        pltpu.VMEM((nbp, n_row, MOBA_W), F32)])
    return pl.pallas_call(
        body,
        grid_spec=grid_spec,
        out_shape=jax.ShapeDtypeStruct((n_seq * L, MOBA_W), F32),
        compiler_params=_cparams(("arbitrary", "arbitrary")),
        name="moba_sample",
    )(pt, *page_args, proj3, proj3, proj3)


def _post_body(og_ref, om_ref, ba_ref, bb_ref, x_ref, gt_ref, sc_ref, sh_ref, g2_ref, wpg_ref, wpm_ref,
               wo_ref, wr_ref, br_ref, x1_ref, h2_ref, lg_ref):
    ya = _dot(og_ref[...].astype(BF16), wpg_ref[...])
    yb = _dot(om_ref[...].astype(BF16), wpm_ref[...])
    ba = jnp.concatenate([ba_ref[0], ba_ref[1]], axis=1)
    bb = jnp.concatenate([bb_ref[0], bb_ref[1]], axis=1)
    mix = _sigmoid(ba) * ya + _sigmoid(bb) * yb
    x1 = x_ref[...] + gt_ref[0] * _dot(mix.astype(BF16), wo_ref[...])
    x1_ref[...] = x1
    ms = jnp.mean(x1 * x1, axis=-1, keepdims=True)
    h2 = x1 * lax.rsqrt(ms + EPS) * g2_ref[...] * (1.0 + sc_ref[0]) + sh_ref[0]
    _store_row_tiles(h2_ref, h2)
    hh = h2.astype(BF16)
    hl = (h2 - hh.astype(F32)).astype(BF16)
    wr = wr_ref[...]
    wh = wr.astype(BF16)
    wl = (wr - wh.astype(F32)).astype(BF16)
    lg_ref[...] = _dot(hh, wh) + _dot(hh, wl) + _dot(hl, wh) + br_ref[...]


def _post(og, om, proj3, x2, gt, sc, sh, g2, wpg, wpm, wo, wr, br, tm, rows_per_mod):
    r, d = x2.shape
    nt = r // tm
    tiles_per_mod = max(rows_per_mod // tm, 1)
    mod_block = (1,) + gt.shape[1:]
    mod_map = lambda i: (i // tiles_per_mod, 0, 0)
    full = lambda a: pl.BlockSpec(a.shape, lambda i: (0,) * a.ndim)
    return pl.pallas_call(
        _post_body,
        grid=(nt,),
        in_specs=[pl.BlockSpec((tm, og.shape[1]), lambda i: (i, 0)),
                  pl.BlockSpec((tm, om.shape[1]), lambda i: (i, 0)),
                  pl.BlockSpec((2, tm, PROJ_TILE), lambda i: (T_BA // 2, i, 0)),
                  pl.BlockSpec((2, tm, PROJ_TILE), lambda i: (T_BB // 2, i, 0)),
                  pl.BlockSpec((tm, d), lambda i: (i, 0)),
                  pl.BlockSpec(mod_block, mod_map),
                  pl.BlockSpec(mod_block, mod_map),
                  pl.BlockSpec(mod_block, mod_map),
                  full(g2), full(wpg), full(wpm), full(wo), full(wr), full(br)],
        out_specs=[pl.BlockSpec((tm, d), lambda i: (i, 0)),
                   pl.BlockSpec((tm * ROW_TILE, LANES), lambda i: (i, 0)),
                   pl.BlockSpec((tm, LANES), lambda i: (i, 0))],
        out_shape=[jax.ShapeDtypeStruct((r, d), F32),
                   jax.ShapeDtypeStruct((r * ROW_TILE, LANES), F32),
                   jax.ShapeDtypeStruct((r, LANES), F32)],
        compiler_params=_cparams(("arbitrary",)),
        name="post",
    )(og, om, proj3, proj3, x2, gt, sc, sh, g2, wpg, wpm, wo, wr, br)


def _route_body(lg_ref, eidx_ref, rank_ref, gate_ref, cnt_ref, run_scr):
    i = pl.program_id(0)
    tm = lg_ref.shape[0]

    @pl.when(i == 0)
    def _():
        run_scr[...] = jnp.zeros(run_scr.shape, F32)

    l = lg_ref[...]
    lane = lax.broadcasted_iota(I32, l.shape, 1)
    vals, hots = [], []
    for _ in range(TOP_K):
        m = jnp.max(l, axis=1, keepdims=True)
        idx = jnp.min(jnp.where(l == m, lane, LANES), axis=1, keepdims=True)
        hot = lane == idx
        vals.append(m)
        hots.append(hot)
        l = jnp.where(hot, NEG_INF, l)
    es = [jnp.exp(v - vals[0]) for v in vals]
    den = es[0] + es[1] + es[2] + es[3]
    chosen = jnp.zeros(l.shape, F32)
    for hot in hots:
        chosen = chosen + jnp.where(hot, 1.0, 0.0)
    row = lax.broadcasted_iota(I32, (tm, tm), 0)
    col = lax.broadcasted_iota(I32, (tm, tm), 1)
    before = jnp.where(row > col, 1.0, 0.0).astype(BF16)
    pos = _dot(before, chosen.astype(BF16)) + run_scr[...]
    eidx = jnp.zeros(l.shape, I32)
    rank = jnp.zeros(l.shape, I32)
    gate = jnp.zeros(l.shape, F32)
    for k in range(TOP_K):
        e_k = jnp.min(jnp.where(hots[k], lane, LANES), axis=1, keepdims=True)
        r_k = jnp.sum(jnp.where(hots[k], pos, 0.0), axis=1, keepdims=True).astype(I32)
        eidx = jnp.where(lane == k, e_k, eidx)
        rank = jnp.where(lane == k, r_k, rank)
        gate = jnp.where(lane == k, es[k] / den, gate)
    eidx_ref[...] = eidx
    rank_ref[...] = rank
    gate_ref[...] = gate
    run_scr[...] = run_scr[...] + jnp.sum(chosen, axis=0, keepdims=True)
    cnt_ref[...] = jnp.broadcast_to(run_scr[...], cnt_ref.shape)


def _route(logits, tm):
    t = logits.shape[0]
    blk = pl.BlockSpec((tm, LANES), lambda i: (i, 0))
    return pl.pallas_call(
        _route_body,
        grid=(t // tm,),
        in_specs=[blk],
        out_specs=[blk, blk, blk, pl.BlockSpec((8, LANES), lambda i: (0, 0))],
        out_shape=[jax.ShapeDtypeStruct((t, LANES), I32),
                   jax.ShapeDtypeStruct((t, LANES), I32),
                   jax.ShapeDtypeStruct((t, LANES), F32),
                   jax.ShapeDtypeStruct((8, LANES), F32)],
        scratch_shapes=[pltpu.VMEM((1, LANES), F32)],
        compiler_params=_cparams(("arbitrary",)),
        name="route",
    )(logits)


def _dispatch_body(dest_ref, h_ref, zero_ref, xb_ref, sem, *, tm):
    del zero_ref
    i = pl.program_id(0)

    def copy(r, k):
        dst = dest_ref[(i * tm + r) * TOP_K + k]
        return pltpu.make_async_copy(h_ref.at[pl.ds(pl.multiple_of(r * ROW_TILE, ROW_TILE), ROW_TILE), :],
                                     xb_ref.at[pl.ds(pl.multiple_of(dst * ROW_TILE, ROW_TILE), ROW_TILE), :], sem)

    def start(r, carry):
        for k in range(TOP_K):
            copy(r, k).start()
        return carry

    def wait(r, carry):
        for k in range(TOP_K):
            copy(r, k).wait()
        return carry

    lax.fori_loop(0, tm, start, 0)
    lax.fori_loop(0, tm, wait, 0)


def _dispatch(dest_flat, h_all, n_rows, tm):
    t = h_all.shape[0] // ROW_TILE
    zeros = jnp.zeros((n_rows * ROW_TILE, LANES), h_all.dtype)
    grid_spec = pltpu.PrefetchScalarGridSpec(
        num_scalar_prefetch=1,
        grid=(t // tm,),
        in_specs=[pl.BlockSpec((tm * ROW_TILE, LANES), lambda i, dr: (i, 0)), pl.BlockSpec(memory_space=pl.ANY)],
        out_specs=pl.BlockSpec(memory_space=pl.ANY),
        scratch_shapes=[pltpu.SemaphoreType.DMA(())])
    return pl.pallas_call(
        functools.partial(_dispatch_body, tm=tm),
        grid_spec=grid_spec,
        out_shape=jax.ShapeDtypeStruct((n_rows * ROW_TILE, LANES), h_all.dtype),
        input_output_aliases={2: 0},
        compiler_params=_cparams(("arbitrary",)),
        name="dispatch",
    )(dest_flat, h_all, zeros)


def _moe_body(be_ref, nu_ref, x_ref, wgu_ref, bgu_ref, wd_ref, bd_ref, y_ref, wgu_scr, wd_scr):
    i = pl.program_id(0)
    prev = be_ref[jnp.maximum(i - 1, 0)]
    fresh = jnp.logical_or(i == 0, be_ref[i] != prev)

    @pl.when(jnp.logical_and(fresh, i < nu_ref[0]))
    def _():
        wgu_scr[...] = wgu_ref[0].astype(BF16)
        wd_scr[...] = wd_ref[0].astype(BF16)

    @pl.when(i < nu_ref[0])
    def _():
        x = _load_row_tiles(x_ref, MOE_ROWS).astype(BF16)
        gu = _dot(x, wgu_scr[...]) + bgu_ref[0]
        glu = jnp.minimum(gu[:, :D_FF], SWIGLU_LIMIT)
        lin = jnp.clip(gu[:, D_FF:], -SWIGLU_LIMIT, SWIGLU_LIMIT)
        act = glu * _sigmoid(SWIGLU_ALPHA * glu) * (lin + 1.0)
        _store_row_tiles(y_ref, _dot(act.astype(BF16), wd_scr[...]) + bd_ref[0])

    @pl.when(i >= nu_ref[0])
    def _():
        y_ref[...] = jnp.zeros(y_ref.shape, F32)


def _moe(block_expert, n_used, xb, wgu, bgu, wd, bd):
    d = D_MODEL
    n_rows = xb.shape[0] // ROW_TILE
    nblk = n_rows // MOE_ROWS
    ne = wgu.shape[0]
    emap = lambda i, be, nu: (be[i], 0, 0)
    grid_spec = pltpu.PrefetchScalarGridSpec(
        num_scalar_prefetch=2,
        grid=(nblk,),
        in_specs=[pl.BlockSpec((MOE_ROWS * ROW_TILE, LANES), lambda i, be, nu: (i, 0)),
                  pl.BlockSpec((1, d, 2 * D_FF), emap),
                  pl.BlockSpec((1, 1, 2 * D_FF), emap),
                  pl.BlockSpec((1, D_FF, d), emap),
                  pl.BlockSpec((1, 1, d), emap)],
        out_specs=pl.BlockSpec((MOE_ROWS * ROW_TILE, LANES), lambda i, be, nu: (i, 0)),
        scratch_shapes=[pltpu.VMEM((d, 2 * D_FF), BF16), pltpu.VMEM((D_FF, d), BF16)])
    return pl.pallas_call(
        _moe_body,
        grid_spec=grid_spec,
        out_shape=jax.ShapeDtypeStruct((n_rows * ROW_TILE, LANES), F32),
        compiler_params=_cparams(("arbitrary",)),
        name="experts",
    )(block_expert, n_used, xb, wgu, bgu.reshape(ne, 1, 2 * D_FF), wd, bd.reshape(ne, 1, d))


def _combine_body(dest_ref, yb_ref, gate_ref, x1_ref, gt_ref, gf_ref, y_ref, buf, sem, *, tm, tok0):
    i = pl.program_id(0)

    def copy(r, k):
        src = dest_ref[(tok0 + i * tm + r) * TOP_K + k]
        return pltpu.make_async_copy(yb_ref.at[pl.ds(pl.multiple_of(src * ROW_TILE, ROW_TILE), ROW_TILE), :],
                                     buf.at[k, pl.ds(pl.multiple_of(r * ROW_TILE, ROW_TILE), ROW_TILE), :], sem)

    def start(r, carry):
        for k in range(TOP_K):
            copy(r, k).start()
        return carry

    def wait(r, carry):
        for k in range(TOP_K):
            copy(r, k).wait()
        return carry

    lax.fori_loop(0, tm, start, 0)
    lax.fori_loop(0, tm, wait, 0)
    g = gate_ref[...]
    y = g[:, 0:1] * _load_row_tiles(buf, tm, (0,))
    for k in range(1, TOP_K):
        y = y + g[:, k:k + 1] * _load_row_tiles(buf, tm, (k,))
    x2 = x1_ref[...] + gt_ref[0] * y
    ms = jnp.mean(x2 * x2, axis=-1, keepdims=True)
    y_ref[...] = x2 * lax.rsqrt(ms + EPS) * gf_ref[...]


def _combine(dest_flat, yb, gates, x1, gt, g_final, tm, rows_per_mod, tok0):
    r, d = x1.shape
    nt = r // tm
    tiles_per_mod = max(rows_per_mod // tm, 1)
    tile0 = tok0 // tm
    mod_block = (1,) + gt.shape[1:]
    grid_spec = pltpu.PrefetchScalarGridSpec(
        num_scalar_prefetch=1,
        grid=(nt,),
        in_specs=[pl.BlockSpec(memory_space=pl.ANY),
                  pl.BlockSpec((tm, LANES), lambda i, dr: (tile0 + i, 0)),
                  pl.BlockSpec((tm, d), lambda i, dr: (i, 0)),
                  pl.BlockSpec(mod_block, lambda i, dr: (i // tiles_per_mod, 0, 0)),
                  pl.BlockSpec((1, d), lambda i, dr: (0, 0))],
        out_specs=pl.BlockSpec((tm, d), lambda i, dr: (i, 0)),
        scratch_shapes=[pltpu.VMEM((TOP_K, tm * ROW_TILE, LANES), F32), pltpu.SemaphoreType.DMA(())])
    return pl.pallas_call(
        functools.partial(_combine_body, tm=tm, tok0=tok0),
        grid_spec=grid_spec,
        out_shape=jax.ShapeDtypeStruct((r, d), F32),
        compiler_params=_cparams(("arbitrary",)),
        name="combine",
    )(dest_flat, yb, gates, x1, gt, g_final)


def _rope_tables(pos):
    half = MOBA_HD // 2
    inv = 1.0 / (ROPE_THETA ** (jnp.arange(half, dtype=F32) / half))
    ang = pos.astype(F32)[:, None] * inv[None, :]
    cos = jnp.cos(ang)
    sin = jnp.sin(ang)
    cos_h = jnp.concatenate([cos, cos], axis=1)
    sin_h = jnp.concatenate([-sin, sin], axis=1)
    return jnp.tile(cos_h, (1, MOBA_HEADS)), jnp.tile(sin_h, (1, MOBA_HEADS))


def _row_tile(n, cap):
    t = min(n, cap)
    while n % t:
        t //= 2
    return t


def kernel(x_prompt, x_sample, cache_k, cache_v, state_gla, page_table, c_prompt, c_sample, w_ada, b_ada, g_norm1, w_in, w_gla_a2, b_gla_a2, g_gla_head, w_proj_gla, w_proj_moba, w_out, g_norm2, w_router, b_router, w_gate_up, b_gate_up, w_down, b_down, g_final):
    nb_p, seq, d = x_prompt.shape
    nb_s, dec = x_sample.shape[:2]
    past_len = page_table.shape[1] * cache_k.shape[2]
    assert w_ada.shape[0] == 1 and d == D_MODEL
    assert seq % MOBA_BLOCK == 0 and past_len % MOBA_BLOCK == 0 and dec <= MOBA_BLOCK and dec % 8 == 0
    r_p, r_s = nb_p * seq, nb_s * dec
    t_all = r_p + r_s

    c_all = jnp.concatenate([c_prompt, c_sample], axis=0)
    mod = _ada(c_all, w_ada[0], b_ada[0]).reshape(nb_p + nb_s, 6, d)
    mod_p = [mod[:nb_p, k].reshape(nb_p, 1, d) for k in range(6)]
    mod_s = [jnp.broadcast_to(mod[nb_p:, k][:, None, :], (nb_s, dec, d)).reshape(1, r_s, d) for k in range(6)]

    w = w_in[0]
    gkw = GLA_HEADS * GLA_DK
    gvw = GLA_HEADS * GLA_DV
    c_ga = 2 * gkw + 2 * gvw
    pad = jnp.zeros((d, PROJ_TILE - GLA_LOWRANK), w.dtype)
    w_in_p = jnp.concatenate([w[:, :c_ga + GLA_LOWRANK], pad, w[:, c_ga + GLA_LOWRANK:]], axis=1).astype(BF16)
    assert w_in_p.shape[1] == N_PROJ_TILES * PROJ_TILE
    w2_p = jnp.zeros((PROJ_TILE, gkw), BF16).at[:GLA_LOWRANK].set(w_gla_a2[0].astype(BF16))
    b2 = b_gla_a2[0].reshape(1, gkw)
    g1 = g_norm1[0].reshape(1, d)
    g2 = g_norm2[0].reshape(1, d)
    wpg = w_proj_gla[0].astype(BF16)
    wpm = w_proj_moba[0].astype(BF16)
    wo = w_out[0].astype(BF16)
    wr = jnp.zeros((d, LANES), F32).at[:, :N_EXPERTS].set(w_router[0])
    br = jnp.full((1, LANES), NEG_INF, F32).at[0, :N_EXPERTS].set(b_router[0])
    g_head = g_gla_head[0].reshape(1, GLA_DV)

    cos_p, sin_p = _rope_tables(jnp.arange(seq, dtype=I32))
    cos_s, sin_s = _rope_tables(past_len + jnp.arange(dec, dtype=I32))
    cos_s, sin_s = jnp.tile(cos_s, (nb_s, 1)), jnp.tile(sin_s, (nb_s, 1))

    xp2 = x_prompt.reshape(r_p, d)
    xs2 = x_sample.reshape(r_s, d)
    tm_p = _row_tile(seq, 1024)

    proj_p, kt_p, vt_p = _inproj(xp2, mod_p[1], mod_p[0], g1, w_in_p, w2_p, b2, cos_p, sin_p, tm_p, seq, True)
    s0_p = jnp.zeros((nb_p, GLA_HEADS, GLA_DK, GLA_DV), state_gla.dtype)
    og_p, st_p = _gla(proj_p, s0_p, g_head, nb_p, seq, _row_tile(seq, 64))
    om_p = _moba_prompt(proj_p, vt_p, nb_p, seq)
    tm_post = _row_tile(seq, 512)
    x1_p, h2_p, lg_p = _post(og_p, om_p, proj_p, xp2, mod_p[2], mod_p[4], mod_p[3], g2, wpg, wpm, wo, wr, br,
                             tm_post, seq)

    proj_s, k5_s, v5_s = _inproj(xs2, mod_s[1], mod_s[0], g1, w_in_p, w2_p, b2, cos_s, sin_s, r_s, r_s, False)
    og_s, st_s = _gla(proj_s, state_gla[0], g_head, nb_s, dec, dec)
    om_s = _moba_sample(proj_s, cache_k[0], cache_v[0], page_table, nb_s, dec)
    x1_s, h2_s, lg_s = _post(og_s, om_s, proj_s, xs2, mod_s[2], mod_s[4], mod_s[3], g2, wpg, wpm, wo, wr, br,
                             r_s, r_s)

    h_all = jnp.concatenate([h2_p, h2_s], axis=0)
    lg_all = jnp.concatenate([lg_p, lg_s], axis=0)
    tm_r = _row_tile(t_all, 256)
    tm_d = t_all // 13 if t_all % (13 * 8) == 0 else tm_r
    eidx, rank, gates, cnt = _route(lg_all, tm_r)
    counts = cnt[0, :N_EXPERTS].astype(I32)
    nblk_e = (counts + MOE_ROWS - 1) // MOE_ROWS
    blk_end = jnp.cumsum(nblk_e)
    blk_start = blk_end - nblk_e
    dest = blk_start[eidx[:, :TOP_K]] * MOE_ROWS + rank[:, :TOP_K]
    dest_flat = dest.reshape(-1).astype(I32)
    n_blocks = -(-(t_all * TOP_K) // MOE_ROWS) + N_EXPERTS
    blk_ids = jnp.arange(n_blocks, dtype=I32)
    block_expert = jnp.minimum(jnp.sum((blk_end[None, :] <= blk_ids[:, None]).astype(I32), axis=1), N_EXPERTS - 1)
    n_used = blk_end[-1:].astype(I32)
    xb = _dispatch(dest_flat, h_all, n_blocks * MOE_ROWS, tm_d)
    yb = _moe(block_expert, n_used, xb, w_gate_up[0], b_gate_up[0], w_down[0], b_down[0])
    gf = g_final.reshape(1, d)
    tm_c = _row_tile(seq, 512)
    y_p = _combine(dest_flat, yb, gates, x1_p, mod_p[5], gf, tm_c, seq, 0)
    y_s = _combine(dest_flat, yb, gates, x1_s, mod_s[5], gf, r_s, r_s, r_p)

    to_out = lambda a: a.reshape(1, nb_p, MOBA_HEADS, MOBA_HD, seq).transpose(0, 1, 4, 2, 3)
    k_p = to_out(kt_p)
    v_p = to_out(vt_p)
    k_s = k5_s.reshape(1, nb_s, dec, MOBA_HEADS, MOBA_HD)
    v_s = v5_s.reshape(1, nb_s, dec, MOBA_HEADS, MOBA_HD)
    return (y_p.reshape(nb_p, seq, d), y_s.reshape(nb_s, dec, d), k_p, v_p, st_p[None],
            k_s, v_s, st_s[None])
```

```python
import functools

import jax
import jax.numpy as jnp
from jax import lax
from jax.experimental import pallas as pl
from jax.experimental.pallas import tpu as pltpu

F32 = jnp.float32
BF16 = jnp.bfloat16
I32 = jnp.int32

D_MODEL = 1024
GLA_HEADS = 4
GLA_DK = 128
GLA_DV = 256
GLA_LOWRANK = 16
GLA_GATE_NORM = 16.0
GLA_SUB = 16
MOBA_HEADS = 8
MOBA_HD = 64
MOBA_BLOCK = 256
MOBA_TOPK = 3
MOBA_W = MOBA_HEADS * MOBA_HD
ROPE_THETA = 10000.0
N_EXPERTS = 32
TOP_K = 4
D_FF = D_MODEL
SWIGLU_LIMIT = 7.0
SWIGLU_ALPHA = 1.702
EPS = 1e-6
LANES = 128
PROJ_TILE = 512

T_GQ, T_GK, T_GV, T_GR, T_GA, T_MQ, T_MK, T_MV, T_BA, T_BB = 0, 1, 2, 4, 6, 7, 8, 9, 10, 12
N_PROJ_TILES = 14
MOE_ROWS = 512
MOBA_S_BLOCKS_PER_STEP = 8
VMEM_LIMIT = 56 * 1024 * 1024

NEG_INF = float("-inf")


def _sigmoid(x):
    return 1.0 / (1.0 + jnp.exp(-x))


def _split3(x):
    hi = x.astype(BF16)
    r = x - hi.astype(F32)
    mid = r.astype(BF16)
    lo = (r - mid.astype(F32)).astype(BF16)
    return hi, mid, lo


def _dot(a, b):
    return jnp.dot(a, b, preferred_element_type=F32)


def _dot_nt(a, b):
    return lax.dot_general(a, b, (((1,), (1,)), ((), ())), preferred_element_type=F32)


def _dot_nt_f32(a, b):
    ah = a.astype(BF16)
    al = (a - ah.astype(F32)).astype(BF16)
    bh = b.astype(BF16)
    bl = (b - bh.astype(F32)).astype(BF16)
    return _dot_nt(ah, bh) + _dot_nt(ah, bl) + _dot_nt(al, bh)


def _div(x, n):
    assert n & (n - 1) == 0
    return lax.shift_right_logical(x, n.bit_length() - 1)


def _mod(x, n):
    assert n & (n - 1) == 0
    return x & (n - 1)


ROW_TILE = D_MODEL // LANES


def _store_row_tiles(ref, val):
    n = val.shape[0]
    for s in range(ROW_TILE):
        ref[pl.ds(s, n, stride=ROW_TILE), :] = val[:, s * LANES:(s + 1) * LANES]


def _load_row_tiles(ref, n, lead=()):
    return jnp.concatenate([ref[lead + (pl.ds(s, n, stride=ROW_TILE), slice(None))] for s in range(ROW_TILE)],
                           axis=1)


def _cparams(sem):
    return pltpu.CompilerParams(dimension_semantics=sem, vmem_limit_bytes=VMEM_LIMIT)


def _ada_body(c_ref, w_ref, b_ref, o_ref):
    c = c_ref[...]
    s = c * _sigmoid(c)
    o_ref[...] = _dot(s.astype(BF16), w_ref[...].astype(BF16)) + b_ref[...]


def _ada(c, w, b):
    n = w.shape[1]
    tn = n // 4
    return pl.pallas_call(
        _ada_body,
        grid=(4,),
        in_specs=[pl.BlockSpec(c.shape, lambda j: (0, 0)),
                  pl.BlockSpec((w.shape[0], tn), lambda j: (0, j)),
                  pl.BlockSpec((1, tn), lambda j: (0, j))],
        out_specs=pl.BlockSpec((c.shape[0], tn), lambda j: (0, j)),
        out_shape=jax.ShapeDtypeStruct((c.shape[0], n), F32),
        compiler_params=_cparams(("arbitrary",)),
        name="ada",
    )(c, w, b.reshape(1, n))


def _rotary(x, cos, sin_signed):
    lane = lax.broadcasted_iota(I32, x.shape, 1)
    half = MOBA_HD // 2
    partner = jnp.where((lane & (MOBA_HD - 1)) < half, lane + half, lane - half)
    w = x.shape[1]
    r1 = pltpu.roll(x, half, 1)
    i1 = pltpu.roll(lane, half, 1)
    r2 = pltpu.roll(x, w - half, 1)
    swapped = jnp.where(i1 == partner, r1, r2)
    return x * cos + swapped * sin_signed


def _log_sigmoid(x):
    return jnp.minimum(x, 0.0) - jnp.log(1.0 + jnp.exp(-jnp.abs(x)))


def _store_heads(ref, val, token_minor):
    if token_minor:
        ref[0] = val.T
        return
    tm = val.shape[0]
    for h in range(MOBA_HEADS):
        ref[pl.ds(h, tm, stride=MOBA_HEADS), :] = val[:, h * MOBA_HD:(h + 1) * MOBA_HD]


def _inproj_body(x_ref, sc_ref, sh_ref, g_ref, w_ref, w2_ref, b2_ref, cos_ref, sin_ref, o_ref, k5_ref, v5_ref,
                 h_scr, *, token_minor):
    j = pl.program_id(1)

    @pl.when(j == 0)
    def _():
        x = x_ref[...]
        ms = jnp.mean(x * x, axis=-1, keepdims=True)
        y = x * lax.rsqrt(ms + EPS) * g_ref[...]
        h = y * (1.0 + sc_ref[0]) + sh_ref[0]
        h_scr[...] = h.astype(BF16)

    acc = _dot(h_scr[...], w_ref[...])
    is_la = j == T_GA
    special = functools.reduce(jnp.logical_or, [j == T_MQ, j == T_MK, j == T_MV, is_la])

    @pl.when(j == T_MQ)
    def _():
        o_ref[0] = _rotary(acc, cos_ref[...], sin_ref[...])

    @pl.when(j == T_MK)
    def _():
        rot = _rotary(acc, cos_ref[...], sin_ref[...])
        o_ref[0] = rot
        _store_heads(k5_ref, rot, token_minor)

    @pl.when(j == T_MV)
    def _():
        o_ref[0] = acc
        _store_heads(v5_ref, acc, token_minor)

    @pl.when(is_la)
    def _():
        z = _dot(acc.astype(BF16), w2_ref[...]) + b2_ref[...]
        o_ref[0] = _log_sigmoid(z) * (1.0 / GLA_GATE_NORM)

    @pl.when(jnp.logical_not(special))
    def _():
        o_ref[0] = acc


def _inproj(x2, sc, sh, g1, w_in_p, w2_p, b2, cos, sin, tm, rows_per_mod, token_minor):
    r, d = x2.shape
    nt = r // tm
    tiles_per_mod = max(rows_per_mod // tm, 1)
    tiles_per_tab = cos.shape[0] // tm
    mod_block = (1,) + sc.shape[1:]
    if token_minor:
        seq = cos.shape[0]
        kv_spec = pl.BlockSpec((1, MOBA_W, tm), lambda i, j: (i // tiles_per_tab, 0, i % tiles_per_tab))
        kv_shape = jax.ShapeDtypeStruct((r // seq, MOBA_W, seq), F32)
    else:
        kv_spec = pl.BlockSpec((tm * MOBA_HEADS, MOBA_HD), lambda i, j: (i, 0))
        kv_shape = jax.ShapeDtypeStruct((r * MOBA_HEADS, MOBA_HD), F32)
    return pl.pallas_call(
        functools.partial(_inproj_body, token_minor=token_minor),
        grid=(nt, N_PROJ_TILES),
        in_specs=[pl.BlockSpec((tm, d), lambda i, j: (i, 0)),
                  pl.BlockSpec(mod_block, lambda i, j: (i // tiles_per_mod, 0, 0)),
                  pl.BlockSpec(mod_block, lambda i, j: (i // tiles_per_mod, 0, 0)),
                  pl.BlockSpec((1, d), lambda i, j: (0, 0)),
                  pl.BlockSpec((d, PROJ_TILE), lambda i, j: (0, j)),
                  pl.BlockSpec((PROJ_TILE, PROJ_TILE), lambda i, j: (0, 0)),
                  pl.BlockSpec((1, PROJ_TILE), lambda i, j: (0, 0)),
                  pl.BlockSpec((tm, PROJ_TILE), lambda i, j: (i % tiles_per_tab, 0)),
                  pl.BlockSpec((tm, PROJ_TILE), lambda i, j: (i % tiles_per_tab, 0))],
        out_specs=[pl.BlockSpec((1, tm, PROJ_TILE), lambda i, j: (j, i, 0)), kv_spec, kv_spec],
        out_shape=[jax.ShapeDtypeStruct((N_PROJ_TILES, r, PROJ_TILE), F32), kv_shape, kv_shape],
        scratch_shapes=[pltpu.VMEM((tm, d), BF16)],
        compiler_params=_cparams(("arbitrary", "arbitrary")),
        name="inproj",
    )(x2, sc, sh, g1, w_in_p, w2_p, b2, cos, sin)


def _gla_body(q_ref, k_ref, v_ref, la_ref, gr_ref, s0_ref, gh_ref, og_ref, sout_ref, st_scr, *, C, SB):
    c = pl.program_id(1)
    nc = pl.num_programs(1)

    @pl.when(c == 0)
    def _():
        for h in range(GLA_HEADS):
            st_scr[h] = s0_ref[0, h].T

    la = la_ref[0]
    row_c = lax.broadcasted_iota(I32, (C, C), 0)
    col_c = lax.broadcasted_iota(I32, (C, C), 1)
    tri = jnp.where(row_c >= col_c, 1.0, 0.0).astype(BF16)
    hi, mid, lo = _split3(la)
    b_all = _dot(tri, hi) + _dot(tri, mid) + _dot(tri, lo)
    q_all = q_ref[0] * (GLA_DK ** -0.5)
    k_all = k_ref[0]
    gh = gh_ref[...]
    row_k = lax.broadcasted_iota(I32, (C, GLA_DK), 0)
    row_grp = lax.broadcasted_iota(I32, (8, 1), 0)
    lane_grp = lax.broadcasted_iota(I32, (8, C), 1)

    for h in range(GLA_HEADS):
        ks = slice(h * GLA_DK, (h + 1) * GLA_DK)
        vs = slice((h % 2) * GLA_DV, (h % 2 + 1) * GLA_DV)
        b = b_all[:, ks]
        q = q_all[:, ks]
        k = k_all[:, ks]
        v = v_ref[h // 2][:, vs]
        st = st_scr[h]
        b_last = b[C - 1:C, :]
        o = _dot_nt((q * jnp.exp(b)).astype(BF16), st.astype(BF16))
        att_rows = []
        for i in range(C // SB):
            r0 = i * SB
            bi = b[r0:r0 + SB]
            qi = q[r0:r0 + SB]
            ki = k[r0:r0 + SB]
            if i > 0:
                bref = b[r0 - 1:r0, :]
                qt = qi * jnp.exp(bi - bref)
                kt = jnp.where(row_k < r0, k * jnp.exp(jnp.minimum(bref - b, 0.0)), 0.0)
                att_i = _dot_nt(qt.astype(BF16), kt.astype(BF16))
            else:
                att_i = jnp.zeros((SB, C), F32)
            for lo in range(0, SB, 8):
                up = min(lo + 8, SB)
                part = att_i[lo:up]
                for s in range(up):
                    dec = jnp.exp(jnp.minimum(bi[lo:up] - bi[s:s + 1, :], 0.0))
                    col = jnp.sum(qi[lo:up] * ki[s:s + 1, :] * dec, axis=1, keepdims=True)
                    col = jnp.where(row_grp + lo >= s, col, 0.0)
                    part = part + jnp.where(lane_grp == r0 + s, col, 0.0)
                att_rows.append(part)
        att = att_rows[0] if len(att_rows) == 1 else jnp.concatenate(att_rows, axis=0)
        vb = v.astype(BF16)
        o = o + _dot(att.astype(BF16), vb)
        kd = k * jnp.exp(b_last - b)
        st_scr[h] = st * jnp.exp(b_last) + _dot(v.T.astype(BF16), kd.astype(BF16))
        on = o * lax.rsqrt(jnp.mean(o * o, axis=-1, keepdims=True) + EPS) * gh
        gr = gr_ref[h // 2][:, vs]
        og_ref[:, h * GLA_DV:(h + 1) * GLA_DV] = (on * (gr * _sigmoid(gr))).astype(og_ref.dtype)

    @pl.when(c == nc - 1)
    def _():
        for h in range(GLA_HEADS):
            sout_ref[0, h] = st_scr[h].T


def _gla(proj3, s0, g_head, n_seq, seq_len, chunk):
    r = proj3.shape[1]
    nc = seq_len // chunk
    sb = min(GLA_SUB, chunk)
    out_dtype = BF16 if chunk % 16 == 0 else F32
    row = lambda b, c: b * nc + c
    body = functools.partial(_gla_body, C=chunk, SB=sb)
    return pl.pallas_call(
        body,
        grid=(n_seq, nc),
        in_specs=[pl.BlockSpec((1, chunk, PROJ_TILE), lambda b, c: (T_GQ, row(b, c), 0)),
                  pl.BlockSpec((1, chunk, PROJ_TILE), lambda b, c: (T_GK, row(b, c), 0)),
                  pl.BlockSpec((2, chunk, PROJ_TILE), lambda b, c: (T_GV // 2, row(b, c), 0)),
                  pl.BlockSpec((1, chunk, PROJ_TILE), lambda b, c: (T_GA, row(b, c), 0)),
                  pl.BlockSpec((2, chunk, PROJ_TILE), lambda b, c: (T_GR // 2, row(b, c), 0)),
                  pl.BlockSpec((1, GLA_HEADS, GLA_DK, GLA_DV), lambda b, c: (b, 0, 0, 0)),
                  pl.BlockSpec((1, GLA_DV), lambda b, c: (0, 0))],
        out_specs=[pl.BlockSpec((chunk, GLA_HEADS * GLA_DV), lambda b, c: (row(b, c), 0)),
                   pl.BlockSpec((1, GLA_HEADS, GLA_DK, GLA_DV), lambda b, c: (b, 0, 0, 0))],
        out_shape=[jax.ShapeDtypeStruct((r, GLA_HEADS * GLA_DV), out_dtype),
                   jax.ShapeDtypeStruct((n_seq, GLA_HEADS, GLA_DK, GLA_DV), F32)],
        scratch_shapes=[pltpu.VMEM((GLA_HEADS, GLA_DV, GLA_DK), F32)],
        compiler_params=_cparams(("arbitrary", "arbitrary")),
        name="gla",
    )(proj3, proj3, proj3, proj3, proj3, s0, g_head)


def _select_topk(g, n_valid_rows, n_rows, n_sel):
    row = lax.broadcasted_iota(I32, g.shape, 0)
    valid = row < n_valid_rows
    gm = jnp.where(valid, g, NEG_INF)
    rank = jnp.zeros(g.shape, F32)
    for jp in range(n_rows):
        gj = gm[jp:jp + 1, :]
        ahead = jnp.logical_or(gj > gm, jnp.logical_and(gj == gm, jp < row))
        rank = rank + jnp.where(ahead, 1.0, 0.0)
    return jnp.where(jnp.logical_and(valid, rank < n_sel), 1.0, 0.0)


def _moba_p_body(q_ref, k_ref, v_ref, o_ref, k_scr, vt_scr, km_scr, sel_scr, m_scr, l_scr, acc_scr, *, nb, nbp):
    i = pl.program_id(1)
    blk = MOBA_BLOCK
    scale = MOBA_HD ** -0.5

    @pl.when(i == 0)
    def _():
        km_scr[...] = jnp.zeros(km_scr.shape, F32)
        for j in range(nb):
            kj = k_ref[0, j * blk:(j + 1) * blk, :]
            k_scr[j] = kj.astype(BF16)
            km_scr[j:j + 1, :] = jnp.mean(kj, axis=0, keepdims=True)
            vt_scr[j] = v_ref[0, :, j * blk:(j + 1) * blk].astype(BF16)

    q = q_ref[0]
    km = km_scr[...]
    km_rep = jnp.concatenate([km] * MOBA_HEADS, axis=0)
    wrow = lax.broadcasted_iota(I32, km_rep.shape, 0)
    wlane = lax.broadcasted_iota(I32, km_rep.shape, 1)
    wt = jnp.where(_div(wlane, MOBA_HD) == _div(wrow, nbp), km_rep, 0.0)
    gates_t = _dot_nt_f32(wt, q)
    for h in range(MOBA_HEADS):
        sel_scr[h] = _select_topk(gates_t[h * nbp:(h + 1) * nbp, :], i, nb, MOBA_TOPK)

    n_pair = MOBA_HEADS // 2
    pair_lane = lax.broadcasted_iota(I32, (blk, LANES), 1)
    key_row = lax.broadcasted_iota(I32, (blk, 2 * blk), 0)
    q_col = _mod(lax.broadcasted_iota(I32, (blk, 2 * blk), 1), blk)
    qms = []
    for pr in range(n_pair):
        qp = q[:, pr * LANES:(pr + 1) * LANES]
        qms.append(jnp.concatenate([jnp.where(_div(pair_lane, MOBA_HD) == u, qp, 0.0) for u in range(2)],
                                   axis=0).astype(BF16))

    def scores(j, pr):
        kj = k_scr[j, :, pr * LANES:(pr + 1) * LANES]
        return _dot_nt(kj, qms[pr]) * scale

    def values(j, h):
        return vt_scr[j, h * MOBA_HD:(h + 1) * MOBA_HD, :]

    for pr in range(n_pair):
        s = jnp.where(key_row <= q_col, scores(i, pr), NEG_INF)
        m = jnp.max(s, axis=0, keepdims=True)
        p = jnp.exp(s - m)
        m_scr[pr] = m
        l_scr[pr] = jnp.sum(p, axis=0, keepdims=True)
        pb = p.astype(BF16)
        for u in range(2):
            acc_scr[2 * pr + u] = _dot(values(i, 2 * pr + u), pb[:, u * blk:(u + 1) * blk])

    def step(j, carry):
        for pr in range(n_pair):
            keep = jnp.concatenate([sel_scr[2 * pr + u, pl.ds(j, 1), :] for u in range(2)], axis=1)
            s = jnp.where(keep > 0.5, scores(j, pr), NEG_INF)
            m_old = m_scr[pr]
            m_new = jnp.maximum(m_old, jnp.max(s, axis=0, keepdims=True))
            alpha = jnp.exp(m_old - m_new)
            p = jnp.exp(s - m_new)
            m_scr[pr] = m_new
            l_scr[pr] = l_scr[pr] * alpha + jnp.sum(p, axis=0, keepdims=True)
            pb = p.astype(BF16)
            for u in range(2):
                h = 2 * pr + u
                cs = slice(u * blk, (u + 1) * blk)
                acc_scr[h] = acc_scr[h] * alpha[:, cs] + _dot(values(j, h), pb[:, cs])
        return carry

    lax.fori_loop(0, i, step, 0)
    outs = []
    for h in range(MOBA_HEADS):
        l = l_scr[h // 2][:, (h % 2) * blk:(h % 2 + 1) * blk]
        outs.append(acc_scr[h] / l)
    out_t = jnp.concatenate(outs, axis=0)
    o_ref[...] = out_t.T.astype(o_ref.dtype)


def _moba_prompt(proj3, v_t, n_seq, seq_len):
    r = proj3.shape[1]
    nb = seq_len // MOBA_BLOCK
    nbp = -(-nb // 8) * 8
    body = functools.partial(_moba_p_body, nb=nb, nbp=nbp)
    return pl.pallas_call(
        body,
        grid=(n_seq, nb),
        in_specs=[pl.BlockSpec((1, MOBA_BLOCK, MOBA_W), lambda b, i: (T_MQ, b * nb + i, 0)),
                  pl.BlockSpec((1, seq_len, MOBA_W), lambda b, i: (T_MK, b, 0)),
                  pl.BlockSpec((1, MOBA_W, seq_len), lambda b, i: (b, 0, 0))],
        out_specs=pl.BlockSpec((MOBA_BLOCK, MOBA_W), lambda b, i: (b * nb + i, 0)),
        out_shape=jax.ShapeDtypeStruct((r, MOBA_W), BF16),
        scratch_shapes=[pltpu.VMEM((nb, MOBA_BLOCK, MOBA_W), BF16),
                        pltpu.VMEM((nb, MOBA_W, MOBA_BLOCK), BF16),
                        pltpu.VMEM((nbp, MOBA_W), F32),
                        pltpu.VMEM((MOBA_HEADS, nbp, MOBA_BLOCK), F32),
                        pltpu.VMEM((MOBA_HEADS // 2, 1, 2 * MOBA_BLOCK), F32),
                        pltpu.VMEM((MOBA_HEADS // 2, 1, 2 * MOBA_BLOCK), F32),
                        pltpu.VMEM((MOBA_HEADS, MOBA_HD, MOBA_BLOCK), F32)],
        compiler_params=_cparams(("arbitrary", "arbitrary")),
        name="moba_prompt",
    )(proj3, proj3, v_t)


def _moba_s_body(pt_ref, *refs, nbp, L, bps):
    del pt_ref
    page_refs = refs[:4 * bps]
    qn_ref, kn_ref, vn_ref, o_ref, q2_scr, gate_scr, m_scr, l_scr, acc_scr = refs[4 * bps:]
    j = pl.program_id(1)
    n_steps = nbp // bps
    scale = MOBA_HD ** -0.5
    n_row = MOBA_HEADS * L
    blk_lane = lax.broadcasted_iota(I32, (n_row, LANES), 1)

    @pl.when(j == 0)
    def _():
        qn = qn_ref[...]
        q2 = jnp.concatenate([qn] * MOBA_HEADS, axis=0)
        row = lax.broadcasted_iota(I32, q2.shape, 0)
        lane = lax.broadcasted_iota(I32, q2.shape, 1)
        q2_scr[...] = jnp.where(_div(lane, MOBA_HD) == _div(row, L), q2, 0.0).astype(BF16)
        gate_scr[...] = jnp.full(gate_scr.shape, NEG_INF, F32)
        m_scr[...] = jnp.zeros(m_scr.shape, F32)
        l_scr[...] = jnp.zeros(l_scr.shape, F32)

    def softmax_partial(s):
        m = jnp.max(s, axis=1, keepdims=True)
        p = jnp.exp(s - m)
        return m, jnp.sum(p, axis=1, keepdims=True), p.astype(BF16)

    @pl.when(j < n_steps)
    def _():
        for t in range(bps):
            ka_ref, kb_ref, va_ref, vb_ref = page_refs[4 * t:4 * t + 4]
            jj = j * bps + t
            kt = jnp.concatenate([ka_ref[0], kb_ref[0]], axis=1).astype(BF16)
            vt = jnp.concatenate([va_ref[0], vb_ref[0]], axis=1).astype(BF16)
            s = _dot(q2_scr[...], kt)
            gate = jnp.mean(s, axis=1, keepdims=True)
            m, l, p = softmax_partial(s * scale)
            hot = blk_lane == jj
            gate_scr[...] = jnp.where(hot, gate, gate_scr[...])
            m_scr[...] = jnp.where(hot, m, m_scr[...])
            l_scr[...] = jnp.where(hot, l, l_scr[...])
            acc_scr[jj] = _dot_nt(p, vt)

    @pl.when(j == n_steps)
    def _():
        pad = jnp.zeros((LANES - L, MOBA_W), F32)
        kn = jnp.concatenate([kn_ref[...], pad], axis=0).astype(BF16)
        vn = jnp.concatenate([vn_ref[...], pad], axis=0).astype(BF16)
        s = _dot_nt(q2_scr[...], kn) * scale
        key = lax.broadcasted_iota(I32, s.shape, 1)
        qpos = _mod(lax.broadcasted_iota(I32, s.shape, 0), L)
        m_own, l_own, p_own = softmax_partial(jnp.where(key <= qpos, s, NEG_INF))
        acc_own = _dot(p_own, vn)
        g = gate_scr[...]
        rank = jnp.zeros(g.shape, F32)
        for jp in range(nbp):
            gj = g[:, jp:jp + 1]
            ahead = jnp.logical_or(gj > g, jnp.logical_and(gj == g, jp < blk_lane))
            rank = rank + jnp.where(ahead, 1.0, 0.0)
        sel = jnp.logical_and(blk_lane < nbp, rank < min(MOBA_TOPK, nbp + 1))
        m_all = m_scr[...]
        m_top = jnp.maximum(m_own, jnp.max(jnp.where(sel, m_all, NEG_INF), axis=1, keepdims=True))
        w = jnp.where(sel, jnp.exp(m_all - m_top), 0.0)
        w_own = jnp.exp(m_own - m_top)
        den = jnp.sum(w * l_scr[...], axis=1, keepdims=True) + w_own * l_own
        num = w_own * acc_own
        for jj in range(nbp):
            num = num + w[:, jj:jj + 1] * acc_scr[jj]
        out_r = num / den
        lane = lax.broadcasted_iota(I32, (L, MOBA_W), 1)
        out = jnp.zeros((L, MOBA_W), F32)
        for h in range(MOBA_HEADS):
            out = out + jnp.where(_div(lane, MOBA_HD) == h, out_r[h * L:(h + 1) * L, :], 0.0)
        o_ref[...] = out


def _moba_sample(proj3, cache_k, cache_v, page_table, n_seq, L):
    n_pool, page = cache_k.shape[0], cache_k.shape[1]
    n_pages = page_table.shape[1]
    ppb = MOBA_BLOCK // page
    nbp = n_pages // ppb
    n_row = MOBA_HEADS * L
    assert ppb == 2 and n_pages % ppb == 0 and page == LANES and nbp <= LANES and n_row % 16 == 0 and L <= LANES
    ck = jnp.transpose(cache_k, (0, 2, 3, 1)).reshape(n_pool, MOBA_W, page)
    cv = jnp.transpose(cache_v, (0, 2, 3, 1)).reshape(n_pool, MOBA_W, page)
    pt = page_table.reshape(-1).astype(I32)

    bps = MOBA_S_BLOCKS_PER_STEP if nbp % MOBA_S_BLOCKS_PER_STEP == 0 else 1
    n_steps = nbp // bps

    def page_map(t, off):
        return lambda b, j, pt_ref: (
            pt_ref[b * n_pages + ppb * (jnp.minimum(j, n_steps - 1) * bps + t) + off], 0, 0)

    new_map = lambda t: (lambda b, j, pt_ref: (t, b, 0))
    body = functools.partial(_moba_s_body, nbp=nbp, L=L, bps=bps)
    page_specs = [pl.BlockSpec((1, MOBA_W, page), page_map(t, off)) for t in range(bps) for off in (0, 1, 0, 1)]
    page_args = [a for _ in range(bps) for a in (ck, ck, cv, cv)]
    grid_spec = pltpu.PrefetchScalarGridSpec(
        num_scalar_prefetch=1,
        grid=(n_seq, n_steps + 1),
        in_specs=page_specs + [
                  pl.BlockSpec((None, L, MOBA_W), new_map(T_MQ)),
                  pl.BlockSpec((None, L, MOBA_W), new_map(T_MK)),
                  pl.BlockSpec((None, L, MOBA_W), new_map(T_MV))],
        out_specs=pl.BlockSpec((L, MOBA_W), lambda b, j, pt_ref: (b, 0)),
        scratch_shapes=[pltpu.VMEM((n_row, MOBA_W), BF16),
                        pltpu.VMEM((n_row, LANES), F32),
                        pltpu.VMEM((n_row, LANES), F32),
                        pltpu.VMEM((n_row, LANES), F32),
                        pltpu.VMEM((nbp, n_row, MOBA_W), F32)])
    return pl.pallas_call(
        body,
        grid_spec=grid_spec,
        out_shape=jax.ShapeDtypeStruct((n_seq * L, MOBA_W), F32),
        compiler_params=_cparams(("arbitrary", "arbitrary")),
        name="moba_sample",
    )(pt, *page_args, proj3, proj3, proj3)


def _post_body(og_ref, om_ref, ba_ref, bb_ref, x_ref, gt_ref, sc_ref, sh_ref, g2_ref, wpg_ref, wpm_ref,
               wo_ref, wr_ref, br_ref, x1_ref, h2_ref, lg_ref):
    ya = _dot(og_ref[...].astype(BF16), wpg_ref[...])
    yb = _dot(om_ref[...].astype(BF16), wpm_ref[...])
    ba = jnp.concatenate([ba_ref[0], ba_ref[1]], axis=1)
    bb = jnp.concatenate([bb_ref[0], bb_ref[1]], axis=1)
    mix = _sigmoid(ba) * ya + _sigmoid(bb) * yb
    x1 = x_ref[...] + gt_ref[0] * _dot(mix.astype(BF16), wo_ref[...])
    x1_ref[...] = x1
    ms = jnp.mean(x1 * x1, axis=-1, keepdims=True)
    h2 = x1 * lax.rsqrt(ms + EPS) * g2_ref[...] * (1.0 + sc_ref[0]) + sh_ref[0]
    _store_row_tiles(h2_ref, h2)
    hh = h2.astype(BF16)
    hl = (h2 - hh.astype(F32)).astype(BF16)
    wr = wr_ref[...]
    wh = wr.astype(BF16)
    wl = (wr - wh.astype(F32)).astype(BF16)
    lg_ref[...] = _dot(hh, wh) + _dot(hh, wl) + _dot(hl, wh) + br_ref[...]


def _post(og, om, proj3, x2, gt, sc, sh, g2, wpg, wpm, wo, wr, br, tm, rows_per_mod):
    r, d = x2.shape
    nt = r // tm
    tiles_per_mod = max(rows_per_mod // tm, 1)
    mod_block = (1,) + gt.shape[1:]
    mod_map = lambda i: (i // tiles_per_mod, 0, 0)
    full = lambda a: pl.BlockSpec(a.shape, lambda i: (0,) * a.ndim)
    return pl.pallas_call(
        _post_body,
        grid=(nt,),
        in_specs=[pl.BlockSpec((tm, og.shape[1]), lambda i: (i, 0)),
                  pl.BlockSpec((tm, om.shape[1]), lambda i: (i, 0)),
                  pl.BlockSpec((2, tm, PROJ_TILE), lambda i: (T_BA // 2, i, 0)),
                  pl.BlockSpec((2, tm, PROJ_TILE), lambda i: (T_BB // 2, i, 0)),
                  pl.BlockSpec((tm, d), lambda i: (i, 0)),
                  pl.BlockSpec(mod_block, mod_map),
                  pl.BlockSpec(mod_block, mod_map),
                  pl.BlockSpec(mod_block, mod_map),
                  full(g2), full(wpg), full(wpm), full(wo), full(wr), full(br)],
        out_specs=[pl.BlockSpec((tm, d), lambda i: (i, 0)),
                   pl.BlockSpec((tm * ROW_TILE, LANES), lambda i: (i, 0)),
                   pl.BlockSpec((tm, LANES), lambda i: (i, 0))],
        out_shape=[jax.ShapeDtypeStruct((r, d), F32),
                   jax.ShapeDtypeStruct((r * ROW_TILE, LANES), F32),
                   jax.ShapeDtypeStruct((r, LANES), F32)],
        compiler_params=_cparams(("arbitrary",)),
        name="post",
    )(og, om, proj3, proj3, x2, gt, sc, sh, g2, wpg, wpm, wo, wr, br)


def _route_body(lg_ref, eidx_ref, rank_ref, gate_ref, cnt_ref, run_scr):
    i = pl.program_id(0)
    tm = lg_ref.shape[0]

    @pl.when(i == 0)
    def _():
        run_scr[...] = jnp.zeros(run_scr.shape, F32)

    l = lg_ref[...]
    lane = lax.broadcasted_iota(I32, l.shape, 1)
    vals, hots = [], []
    for _ in range(TOP_K):
        m = jnp.max(l, axis=1, keepdims=True)
        idx = jnp.min(jnp.where(l == m, lane, LANES), axis=1, keepdims=True)
        hot = lane == idx
        vals.append(m)
        hots.append(hot)
        l = jnp.where(hot, NEG_INF, l)
    es = [jnp.exp(v - vals[0]) for v in vals]
    den = es[0] + es[1] + es[2] + es[3]
    chosen = jnp.zeros(l.shape, F32)
    for hot in hots:
        chosen = chosen + jnp.where(hot, 1.0, 0.0)
    row = lax.broadcasted_iota(I32, (tm, tm), 0)
    col = lax.broadcasted_iota(I32, (tm, tm), 1)
    before = jnp.where(row > col, 1.0, 0.0).astype(BF16)
    pos = _dot(before, chosen.astype(BF16)) + run_scr[...]
    eidx = jnp.zeros(l.shape, I32)
    rank = jnp.zeros(l.shape, I32)
    gate = jnp.zeros(l.shape, F32)
    for k in range(TOP_K):
        e_k = jnp.min(jnp.where(hots[k], lane, LANES), axis=1, keepdims=True)
        r_k = jnp.sum(jnp.where(hots[k], pos, 0.0), axis=1, keepdims=True).astype(I32)
        eidx = jnp.where(lane == k, e_k, eidx)
        rank = jnp.where(lane == k, r_k, rank)
        gate = jnp.where(lane == k, es[k] / den, gate)
    eidx_ref[...] = eidx
    rank_ref[...] = rank
    gate_ref[...] = gate
    run_scr[...] = run_scr[...] + jnp.sum(chosen, axis=0, keepdims=True)
    cnt_ref[...] = jnp.broadcast_to(run_scr[...], cnt_ref.shape)


def _route(logits, tm):
    t = logits.shape[0]
    blk = pl.BlockSpec((tm, LANES), lambda i: (i, 0))
    return pl.pallas_call(
        _route_body,
        grid=(t // tm,),
        in_specs=[blk],
        out_specs=[blk, blk, blk, pl.BlockSpec((8, LANES), lambda i: (0, 0))],
        out_shape=[jax.ShapeDtypeStruct((t, LANES), I32),
                   jax.ShapeDtypeStruct((t, LANES), I32),
                   jax.ShapeDtypeStruct((t, LANES), F32),
                   jax.ShapeDtypeStruct((8, LANES), F32)],
        scratch_shapes=[pltpu.VMEM((1, LANES), F32)],
        compiler_params=_cparams(("arbitrary",)),
        name="route",
    )(logits)


def _dispatch_body(dest_ref, ha_ref, hb_ref, zero_ref, xb_ref, sem, *, tm, tiles_a):
    del zero_ref
    i = pl.program_id(0)

    def scatter_tile(h_ref):
        def copy(r, k):
            dst = dest_ref[(i * tm + r) * TOP_K + k]
            return pltpu.make_async_copy(h_ref.at[pl.ds(pl.multiple_of(r * ROW_TILE, ROW_TILE), ROW_TILE), :],
                                         xb_ref.at[pl.ds(pl.multiple_of(dst * ROW_TILE, ROW_TILE), ROW_TILE), :],
                                         sem)

        def start(r, carry):
            for k in range(TOP_K):
                copy(r, k).start()
            return carry

        def wait(r, carry):
            for k in range(TOP_K):
                copy(r, k).wait()
            return carry

        lax.fori_loop(0, tm, start, 0)
        lax.fori_loop(0, tm, wait, 0)

    @pl.when(i < tiles_a)
    def _():
        scatter_tile(ha_ref)

    @pl.when(i >= tiles_a)
    def _():
        scatter_tile(hb_ref)


def _dispatch(dest_flat, h_a, h_b, n_rows, tm):
    tiles_a = h_a.shape[0] // (ROW_TILE * tm)
    tiles_b = h_b.shape[0] // (ROW_TILE * tm)
    assert tiles_a * tm * ROW_TILE == h_a.shape[0] and tiles_b * tm * ROW_TILE == h_b.shape[0]
    zeros = jnp.zeros((n_rows * ROW_TILE, LANES), h_a.dtype)
    grid_spec = pltpu.PrefetchScalarGridSpec(
        num_scalar_prefetch=1,
        grid=(tiles_a + tiles_b,),
        in_specs=[pl.BlockSpec((tm * ROW_TILE, LANES), lambda i, dr: (jnp.minimum(i, tiles_a - 1), 0)),
                  pl.BlockSpec((tm * ROW_TILE, LANES), lambda i, dr: (jnp.maximum(i - tiles_a, 0), 0)),
                  pl.BlockSpec(memory_space=pl.ANY)],
        out_specs=pl.BlockSpec(memory_space=pl.ANY),
        scratch_shapes=[pltpu.SemaphoreType.DMA(())])
    return pl.pallas_call(
        functools.partial(_dispatch_body, tm=tm, tiles_a=tiles_a),
        grid_spec=grid_spec,
        out_shape=jax.ShapeDtypeStruct((n_rows * ROW_TILE, LANES), h_a.dtype),
        input_output_aliases={3: 0},
        compiler_params=_cparams(("arbitrary",)),
        name="dispatch",
    )(dest_flat, h_a, h_b, zeros)


def _moe_body(be_ref, nu_ref, x_ref, wgu_ref, bgu_ref, wd_ref, bd_ref, y_ref, wgu_scr, wd_scr):
    i = pl.program_id(0)
    prev = be_ref[jnp.maximum(i - 1, 0)]
    fresh = jnp.logical_or(i == 0, be_ref[i] != prev)

    @pl.when(jnp.logical_and(fresh, i < nu_ref[0]))
    def _():
        wgu_scr[...] = wgu_ref[0].astype(BF16)
        wd_scr[...] = wd_ref[0].astype(BF16)

    @pl.when(i < nu_ref[0])
    def _():
        x = _load_row_tiles(x_ref, MOE_ROWS).astype(BF16)
        gu = _dot(x, wgu_scr[...]) + bgu_ref[0]
        glu = jnp.minimum(gu[:, :D_FF], SWIGLU_LIMIT)
        lin = jnp.clip(gu[:, D_FF:], -SWIGLU_LIMIT, SWIGLU_LIMIT)
        act = glu * _sigmoid(SWIGLU_ALPHA * glu) * (lin + 1.0)
        _store_row_tiles(y_ref, _dot(act.astype(BF16), wd_scr[...]) + bd_ref[0])

    @pl.when(i >= nu_ref[0])
    def _():
        y_ref[...] = jnp.zeros(y_ref.shape, F32)


def _moe(block_expert, n_used, xb, wgu, bgu, wd, bd):
    d = D_MODEL
    n_rows = xb.shape[0] // ROW_TILE
    nblk = n_rows // MOE_ROWS
    ne = wgu.shape[0]
    emap = lambda i, be, nu: (be[i], 0, 0)
    grid_spec = pltpu.PrefetchScalarGridSpec(
        num_scalar_prefetch=2,
        grid=(nblk,),
        in_specs=[pl.BlockSpec((MOE_ROWS * ROW_TILE, LANES), lambda i, be, nu: (i, 0)),
                  pl.BlockSpec((1, d, 2 * D_FF), emap),
                  pl.BlockSpec((1, 1, 2 * D_FF), emap),
                  pl.BlockSpec((1, D_FF, d), emap),
                  pl.BlockSpec((1, 1, d), emap)],
        out_specs=pl.BlockSpec((MOE_ROWS * ROW_TILE, LANES), lambda i, be, nu: (i, 0)),
        scratch_shapes=[pltpu.VMEM((d, 2 * D_FF), BF16), pltpu.VMEM((D_FF, d), BF16)])
    return pl.pallas_call(
        _moe_body,
        grid_spec=grid_spec,
        out_shape=jax.ShapeDtypeStruct((n_rows * ROW_TILE, LANES), F32),
        compiler_params=_cparams(("arbitrary",)),
        name="experts",
    )(block_expert, n_used, xb, wgu, bgu.reshape(ne, 1, 2 * D_FF), wd, bd.reshape(ne, 1, d))


def _combine_body(dest_ref, yb_ref, gate_ref, x1_ref, gt_ref, gf_ref, y_ref, buf, sem, *, tm, tok0):
    i = pl.program_id(0)

    def copy(r, k):
        src = dest_ref[(tok0 + i * tm + r) * TOP_K + k]
        return pltpu.make_async_copy(yb_ref.at[pl.ds(pl.multiple_of(src * ROW_TILE, ROW_TILE), ROW_TILE), :],
                                     buf.at[k, pl.ds(pl.multiple_of(r * ROW_TILE, ROW_TILE), ROW_TILE), :], sem)

    def start(r, carry):
        for k in range(TOP_K):
            copy(r, k).start()
        return carry

    def wait(r, carry):
        for k in range(TOP_K):
            copy(r, k).wait()
        return carry

    lax.fori_loop(0, tm, start, 0)
    lax.fori_loop(0, tm, wait, 0)
    g = gate_ref[...]
    y = g[:, 0:1] * _load_row_tiles(buf, tm, (0,))
    for k in range(1, TOP_K):
        y = y + g[:, k:k + 1] * _load_row_tiles(buf, tm, (k,))
    x2 = x1_ref[...] + gt_ref[0] * y
    ms = jnp.mean(x2 * x2, axis=-1, keepdims=True)
    y_ref[...] = x2 * lax.rsqrt(ms + EPS) * gf_ref[...]


def _combine(dest_flat, yb, gates, x1, gt, g_final, tm, rows_per_mod, tok0):
    r, d = x1.shape
    nt = r // tm
    tiles_per_mod = max(rows_per_mod // tm, 1)
    tile0 = tok0 // tm
    mod_block = (1,) + gt.shape[1:]
    grid_spec = pltpu.PrefetchScalarGridSpec(
        num_scalar_prefetch=1,
        grid=(nt,),
        in_specs=[pl.BlockSpec(memory_space=pl.ANY),
                  pl.BlockSpec((tm, LANES), lambda i, dr: (tile0 + i, 0)),
                  pl.BlockSpec((tm, d), lambda i, dr: (i, 0)),
                  pl.BlockSpec(mod_block, lambda i, dr: (i // tiles_per_mod, 0, 0)),
                  pl.BlockSpec((1, d), lambda i, dr: (0, 0))],
        out_specs=pl.BlockSpec((tm, d), lambda i, dr: (i, 0)),
        scratch_shapes=[pltpu.VMEM((TOP_K, tm * ROW_TILE, LANES), F32), pltpu.SemaphoreType.DMA(())])
    return pl.pallas_call(
        functools.partial(_combine_body, tm=tm, tok0=tok0),
        grid_spec=grid_spec,
        out_shape=jax.ShapeDtypeStruct((r, d), F32),
        compiler_params=_cparams(("arbitrary",)),
        name="combine",
    )(dest_flat, yb, gates, x1, gt, g_final)


def _rope_tables(pos):
    half = MOBA_HD // 2
    inv = 1.0 / (ROPE_THETA ** (jnp.arange(half, dtype=F32) / half))
    ang = pos.astype(F32)[:, None] * inv[None, :]
    cos = jnp.cos(ang)
    sin = jnp.sin(ang)
    cos_h = jnp.concatenate([cos, cos], axis=1)
    sin_h = jnp.concatenate([-sin, sin], axis=1)
    return jnp.tile(cos_h, (1, MOBA_HEADS)), jnp.tile(sin_h, (1, MOBA_HEADS))


def _row_tile(n, cap):
    t = min(n, cap)
    while n % t:
        t //= 2
    return t


def kernel(x_prompt, x_sample, cache_k, cache_v, state_gla, page_table, c_prompt, c_sample, w_ada, b_ada, g_norm1, w_in, w_gla_a2, b_gla_a2, g_gla_head, w_proj_gla, w_proj_moba, w_out, g_norm2, w_router, b_router, w_gate_up, b_gate_up, w_down, b_down, g_final):
    nb_p, seq, d = x_prompt.shape
    nb_s, dec = x_sample.shape[:2]
    past_len = page_table.shape[1] * cache_k.shape[2]
    assert w_ada.shape[0] == 1 and d == D_MODEL
    assert seq % MOBA_BLOCK == 0 and past_len % MOBA_BLOCK == 0 and dec <= MOBA_BLOCK and dec % 8 == 0
    r_p, r_s = nb_p * seq, nb_s * dec
    t_all = r_p + r_s

    c_all = jnp.concatenate([c_prompt, c_sample], axis=0)
    mod = _ada(c_all, w_ada[0], b_ada[0]).reshape(nb_p + nb_s, 6, d)
    mod_p = [mod[:nb_p, k].reshape(nb_p, 1, d) for k in range(6)]
    mod_s = [jnp.broadcast_to(mod[nb_p:, k][:, None, :], (nb_s, dec, d)).reshape(1, r_s, d) for k in range(6)]

    w = w_in[0]
    gkw = GLA_HEADS * GLA_DK
    gvw = GLA_HEADS * GLA_DV
    c_ga = 2 * gkw + 2 * gvw
    pad = jnp.zeros((d, PROJ_TILE - GLA_LOWRANK), w.dtype)
    w_in_p = jnp.concatenate([w[:, :c_ga + GLA_LOWRANK], pad, w[:, c_ga + GLA_LOWRANK:]], axis=1).astype(BF16)
    assert w_in_p.shape[1] == N_PROJ_TILES * PROJ_TILE
    w2_p = jnp.zeros((PROJ_TILE, gkw), BF16).at[:GLA_LOWRANK].set(w_gla_a2[0].astype(BF16))
    b2 = b_gla_a2[0].reshape(1, gkw)
    g1 = g_norm1[0].reshape(1, d)
    g2 = g_norm2[0].reshape(1, d)
    wpg = w_proj_gla[0].astype(BF16)
    wpm = w_proj_moba[0].astype(BF16)
    wo = w_out[0].astype(BF16)
    wr = jnp.zeros((d, LANES), F32).at[:, :N_EXPERTS].set(w_router[0])
    br = jnp.full((1, LANES), NEG_INF, F32).at[0, :N_EXPERTS].set(b_router[0])
    g_head = g_gla_head[0].reshape(1, GLA_DV)

    cos_p, sin_p = _rope_tables(jnp.arange(seq, dtype=I32))
    cos_s, sin_s = _rope_tables(past_len + jnp.arange(dec, dtype=I32))
    cos_s, sin_s = jnp.tile(cos_s, (nb_s, 1)), jnp.tile(sin_s, (nb_s, 1))

    xp2 = x_prompt.reshape(r_p, d)
    xs2 = x_sample.reshape(r_s, d)
    tm_p = _row_tile(seq, 1024)

    proj_p, kt_p, vt_p = _inproj(xp2, mod_p[1], mod_p[0], g1, w_in_p, w2_p, b2, cos_p, sin_p, tm_p, seq, True)
    s0_p = jnp.zeros((nb_p, GLA_HEADS, GLA_DK, GLA_DV), state_gla.dtype)
    og_p, st_p = _gla(proj_p, s0_p, g_head, nb_p, seq, _row_tile(seq, 64))
    om_p = _moba_prompt(proj_p, vt_p, nb_p, seq)
    tm_post = _row_tile(seq, 512)
    x1_p, h2_p, lg_p = _post(og_p, om_p, proj_p, xp2, mod_p[2], mod_p[4], mod_p[3], g2, wpg, wpm, wo, wr, br,
                             tm_post, seq)

    proj_s, k5_s, v5_s = _inproj(xs2, mod_s[1], mod_s[0], g1, w_in_p, w2_p, b2, cos_s, sin_s, r_s, r_s, False)
    og_s, st_s = _gla(proj_s, state_gla[0], g_head, nb_s, dec, dec)
    om_s = _moba_sample(proj_s, cache_k[0], cache_v[0], page_table, nb_s, dec)
    x1_s, h2_s, lg_s = _post(og_s, om_s, proj_s, xs2, mod_s[2], mod_s[4], mod_s[3], g2, wpg, wpm, wo, wr, br,
                             r_s, r_s)

    lg_all = jnp.concatenate([lg_p, lg_s], axis=0)
    tm_r = _row_tile(t_all, 256)
    eidx, rank, gates, cnt = _route(lg_all, tm_r)
    counts = cnt[0, :N_EXPERTS].astype(I32)
    nblk_e = (counts + MOE_ROWS - 1) // MOE_ROWS
    blk_end = jnp.cumsum(nblk_e)
    blk_start = blk_end - nblk_e
    chosen = eidx[:, :TOP_K, None] == jnp.arange(N_EXPERTS, dtype=I32)
    dest = jnp.sum(jnp.where(chosen, blk_start, 0), axis=-1) * MOE_ROWS + rank[:, :TOP_K]
    dest_flat = dest.reshape(-1).astype(I32)
    n_blocks = -(-(t_all * TOP_K) // MOE_ROWS) + N_EXPERTS
    blk_ids = jnp.arange(n_blocks, dtype=I32)
    block_expert = jnp.minimum(jnp.sum((blk_end[None, :] <= blk_ids[:, None]).astype(I32), axis=1), N_EXPERTS - 1)
    n_used = blk_end[-1:].astype(I32)
    tm_d = _row_tile(r_s, 256)
    assert r_p % tm_d == 0
    xb = _dispatch(dest_flat, h2_p, h2_s, n_blocks * MOE_ROWS, tm_d)
    yb = _moe(block_expert, n_used, xb, w_gate_up[0], b_gate_up[0], w_down[0], b_down[0])
    gf = g_final.reshape(1, d)
    tm_c = _row_tile(seq, 256)
    y_p = _combine(dest_flat, yb, gates, x1_p, mod_p[5], gf, tm_c, seq, 0)
    y_s = _combine(dest_flat, yb, gates, x1_s, mod_s[5], gf, r_s, r_s, r_p)

    to_out = lambda a: a.reshape(1, nb_p, MOBA_HEADS, MOBA_HD, seq).transpose(0, 1, 4, 2, 3)
    k_p = to_out(kt_p)
    v_p = to_out(vt_p)
    k_s = k5_s.reshape(1, nb_s, dec, MOBA_HEADS, MOBA_HD)
    v_s = v5_s.reshape(1, nb_s, dec, MOBA_HEADS, MOBA_HD)
    return (y_p.reshape(nb_p, seq, d), y_s.reshape(nb_s, dec, d), k_p, v_p, st_p[None],
            k_s, v_s, st_s[None])
```

```python
import functools

import jax
import jax.numpy as jnp
from jax import lax
from jax.experimental import pallas as pl
from jax.experimental.pallas import tpu as pltpu

F32 = jnp.float32
BF16 = jnp.bfloat16
I32 = jnp.int32

D_MODEL = 1024
GLA_HEADS = 4
GLA_DK = 128
GLA_DV = 256
GLA_LOWRANK = 16
GLA_GATE_NORM = 16.0
GLA_SUB = 16
MOBA_HEADS = 8
MOBA_HD = 64
MOBA_BLOCK = 256
MOBA_TOPK = 3
MOBA_W = MOBA_HEADS * MOBA_HD
ROPE_THETA = 10000.0
N_EXPERTS = 32
TOP_K = 4
D_FF = D_MODEL
SWIGLU_LIMIT = 7.0
SWIGLU_ALPHA = 1.702
EPS = 1e-6
LANES = 128
PROJ_TILE = 512

T_GQ, T_GK, T_GV, T_GR, T_GA, T_MQ, T_MK, T_MV, T_BA, T_BB = 0, 1, 2, 4, 6, 7, 8, 9, 10, 12
N_PROJ_TILES = 14
MOE_ROWS = 512
MOBA_S_BLOCKS_PER_STEP = 8
VMEM_LIMIT = 56 * 1024 * 1024

NEG_INF = float("-inf")


def _sigmoid(x):
    return 1.0 / (1.0 + jnp.exp(-x))


def _split3(x):
    hi = x.astype(BF16)
    r = x - hi.astype(F32)
    mid = r.astype(BF16)
    lo = (r - mid.astype(F32)).astype(BF16)
    return hi, mid, lo


def _dot(a, b):
    return jnp.dot(a, b, preferred_element_type=F32)


def _dot_nt(a, b):
    return lax.dot_general(a, b, (((1,), (1,)), ((), ())), preferred_element_type=F32)


def _dot_nt_f32(a, b):
    ah = a.astype(BF16)
    al = (a - ah.astype(F32)).astype(BF16)
    bh = b.astype(BF16)
    bl = (b - bh.astype(F32)).astype(BF16)
    return _dot_nt(ah, bh) + _dot_nt(ah, bl) + _dot_nt(al, bh)


def _div(x, n):
    assert n & (n - 1) == 0
    return lax.shift_right_logical(x, n.bit_length() - 1)


def _mod(x, n):
    assert n & (n - 1) == 0
    return x & (n - 1)


ROW_TILE = D_MODEL // LANES


def _store_row_tiles(ref, val):
    n = val.shape[0]
    for s in range(ROW_TILE):
        ref[pl.ds(s, n, stride=ROW_TILE), :] = val[:, s * LANES:(s + 1) * LANES]


def _load_row_tiles(ref, n, lead=()):
    return jnp.concatenate([ref[lead + (pl.ds(s, n, stride=ROW_TILE), slice(None))] for s in range(ROW_TILE)],
                           axis=1)


def _cparams(sem):
    return pltpu.CompilerParams(dimension_semantics=sem, vmem_limit_bytes=VMEM_LIMIT)


def _ada_body(c_ref, w_ref, b_ref, o_ref):
    c = c_ref[...]
    s = c * _sigmoid(c)
    o_ref[...] = _dot(s.astype(BF16), w_ref[...].astype(BF16)) + b_ref[...]


def _ada(c, w, b):
    n = w.shape[1]
    tn = n // 4
    return pl.pallas_call(
        _ada_body,
        grid=(4,),
        in_specs=[pl.BlockSpec(c.shape, lambda j: (0, 0)),
                  pl.BlockSpec((w.shape[0], tn), lambda j: (0, j)),
                  pl.BlockSpec((1, tn), lambda j: (0, j))],
        out_specs=pl.BlockSpec((c.shape[0], tn), lambda j: (0, j)),
        out_shape=jax.ShapeDtypeStruct((c.shape[0], n), F32),
        compiler_params=_cparams(("arbitrary",)),
        name="ada",
    )(c, w, b.reshape(1, n))


def _rotary(x, cos, sin_signed):
    lane = lax.broadcasted_iota(I32, x.shape, 1)
    half = MOBA_HD // 2
    partner = jnp.where((lane & (MOBA_HD - 1)) < half, lane + half, lane - half)
    w = x.shape[1]
    r1 = pltpu.roll(x, half, 1)
    i1 = pltpu.roll(lane, half, 1)
    r2 = pltpu.roll(x, w - half, 1)
    swapped = jnp.where(i1 == partner, r1, r2)
    return x * cos + swapped * sin_signed


def _log_sigmoid(x):
    return jnp.minimum(x, 0.0) - jnp.log(1.0 + jnp.exp(-jnp.abs(x)))


def _store_heads(ref, val, token_minor):
    if token_minor:
        ref[0] = val.T
        return
    tm = val.shape[0]
    for h in range(MOBA_HEADS):
        ref[pl.ds(h, tm, stride=MOBA_HEADS), :] = val[:, h * MOBA_HD:(h + 1) * MOBA_HD]


def _inproj_body(x_ref, sc_ref, sh_ref, g_ref, w_ref, w2_ref, b2_ref, cos_ref, sin_ref, o_ref, k5_ref, v5_ref,
                 h_scr, *, token_minor):
    j = pl.program_id(1)

    @pl.when(j == 0)
    def _():
        x = x_ref[...]
        ms = jnp.mean(x * x, axis=-1, keepdims=True)
        y = x * lax.rsqrt(ms + EPS) * g_ref[...]
        h = y * (1.0 + sc_ref[0]) + sh_ref[0]
        h_scr[...] = h.astype(BF16)

    acc = _dot(h_scr[...], w_ref[...])
    is_la = j == T_GA
    special = functools.reduce(jnp.logical_or, [j == T_MQ, j == T_MK, j == T_MV, is_la])

    @pl.when(j == T_MQ)
    def _():
        o_ref[0] = _rotary(acc, cos_ref[...], sin_ref[...])

    @pl.when(j == T_MK)
    def _():
        rot = _rotary(acc, cos_ref[...], sin_ref[...])
        o_ref[0] = rot
        _store_heads(k5_ref, rot, token_minor)

    @pl.when(j == T_MV)
    def _():
        o_ref[0] = acc
        _store_heads(v5_ref, acc, token_minor)

    @pl.when(is_la)
    def _():
        z = _dot(acc.astype(BF16), w2_ref[...]) + b2_ref[...]
        o_ref[0] = _log_sigmoid(z) * (1.0 / GLA_GATE_NORM)

    @pl.when(jnp.logical_not(special))
    def _():
        o_ref[0] = acc


def _inproj(x2, sc, sh, g1, w_in_p, w2_p, b2, cos, sin, tm, rows_per_mod, token_minor):
    r, d = x2.shape
    nt = r // tm
    tiles_per_mod = max(rows_per_mod // tm, 1)
    tiles_per_tab = cos.shape[0] // tm
    mod_block = (1,) + sc.shape[1:]
    if token_minor:
        seq = cos.shape[0]
        kv_spec = pl.BlockSpec((1, MOBA_W, tm), lambda i, j: (i // tiles_per_tab, 0, i % tiles_per_tab))
        kv_shape = jax.ShapeDtypeStruct((r // seq, MOBA_W, seq), F32)
    else:
        kv_spec = pl.BlockSpec((tm * MOBA_HEADS, MOBA_HD), lambda i, j: (i, 0))
        kv_shape = jax.ShapeDtypeStruct((r * MOBA_HEADS, MOBA_HD), F32)
    return pl.pallas_call(
        functools.partial(_inproj_body, token_minor=token_minor),
        grid=(nt, N_PROJ_TILES),
        in_specs=[pl.BlockSpec((tm, d), lambda i, j: (i, 0)),
                  pl.BlockSpec(mod_block, lambda i, j: (i // tiles_per_mod, 0, 0)),
                  pl.BlockSpec(mod_block, lambda i, j: (i // tiles_per_mod, 0, 0)),
                  pl.BlockSpec((1, d), lambda i, j: (0, 0)),
                  pl.BlockSpec((d, PROJ_TILE), lambda i, j: (0, j)),
                  pl.BlockSpec((PROJ_TILE, PROJ_TILE), lambda i, j: (0, 0)),
                  pl.BlockSpec((1, PROJ_TILE), lambda i, j: (0, 0)),
                  pl.BlockSpec((tm, PROJ_TILE), lambda i, j: (i % tiles_per_tab, 0)),
                  pl.BlockSpec((tm, PROJ_TILE), lambda i, j: (i % tiles_per_tab, 0))],
        out_specs=[pl.BlockSpec((1, tm, PROJ_TILE), lambda i, j: (j, i, 0)), kv_spec, kv_spec],
        out_shape=[jax.ShapeDtypeStruct((N_PROJ_TILES, r, PROJ_TILE), F32), kv_shape, kv_shape],
        scratch_shapes=[pltpu.VMEM((tm, d), BF16)],
        compiler_params=_cparams(("arbitrary", "arbitrary")),
        name="inproj",
    )(x2, sc, sh, g1, w_in_p, w2_p, b2, cos, sin)


def _gla_body(q_ref, k_ref, v_ref, la_ref, gr_ref, s0_ref, gh_ref, og_ref, sout_ref, st_scr, *, C, SB):
    c = pl.program_id(1)
    nc = pl.num_programs(1)

    @pl.when(c == 0)
    def _():
        for h in range(GLA_HEADS):
            st_scr[h] = s0_ref[0, h].T

    la = la_ref[0]
    row_c = lax.broadcasted_iota(I32, (C, C), 0)
    col_c = lax.broadcasted_iota(I32, (C, C), 1)
    tri = jnp.where(row_c >= col_c, 1.0, 0.0).astype(BF16)
    hi, mid, lo = _split3(la)
    b_all = _dot(tri, hi) + _dot(tri, mid) + _dot(tri, lo)
    q_all = q_ref[0] * (GLA_DK ** -0.5)
    k_all = k_ref[0]
    gh = gh_ref[...]
    row_k = lax.broadcasted_iota(I32, (C, GLA_DK), 0)
    row_grp = lax.broadcasted_iota(I32, (8, 1), 0)
    lane_grp = lax.broadcasted_iota(I32, (8, C), 1)

    for h in range(GLA_HEADS):
        ks = slice(h * GLA_DK, (h + 1) * GLA_DK)
        vs = slice((h % 2) * GLA_DV, (h % 2 + 1) * GLA_DV)
        b = b_all[:, ks]
        q = q_all[:, ks]
        k = k_all[:, ks]
        v = v_ref[h // 2][:, vs]
        st = st_scr[h]
        b_last = b[C - 1:C, :]
        o = _dot_nt((q * jnp.exp(b)).astype(BF16), st.astype(BF16))
        att_rows = []
        for i in range(C // SB):
            r0 = i * SB
            bi = b[r0:r0 + SB]
            qi = q[r0:r0 + SB]
            ki = k[r0:r0 + SB]
            if i > 0:
                bref = b[r0 - 1:r0, :]
                qt = qi * jnp.exp(bi - bref)
                kt = jnp.where(row_k < r0, k * jnp.exp(jnp.minimum(bref - b, 0.0)), 0.0)
                att_i = _dot_nt(qt.astype(BF16), kt.astype(BF16))
            else:
                att_i = jnp.zeros((SB, C), F32)
            for lo in range(0, SB, 8):
                up = min(lo + 8, SB)
                part = att_i[lo:up]
                for s in range(up):
                    dec = jnp.exp(jnp.minimum(bi[lo:up] - bi[s:s + 1, :], 0.0))
                    col = jnp.sum(qi[lo:up] * ki[s:s + 1, :] * dec, axis=1, keepdims=True)
                    col = jnp.where(row_grp + lo >= s, col, 0.0)
                    part = part + jnp.where(lane_grp == r0 + s, col, 0.0)
                att_rows.append(part)
        att = att_rows[0] if len(att_rows) == 1 else jnp.concatenate(att_rows, axis=0)
        vb = v.astype(BF16)
        o = o + _dot(att.astype(BF16), vb)
        kd = k * jnp.exp(b_last - b)
        st_scr[h] = st * jnp.exp(b_last) + _dot(v.T.astype(BF16), kd.astype(BF16))
        on = o * lax.rsqrt(jnp.mean(o * o, axis=-1, keepdims=True) + EPS) * gh
        gr = gr_ref[h // 2][:, vs]
        og_ref[:, h * GLA_DV:(h + 1) * GLA_DV] = (on * (gr * _sigmoid(gr))).astype(og_ref.dtype)

    @pl.when(c == nc - 1)
    def _():
        for h in range(GLA_HEADS):
            sout_ref[0, h] = st_scr[h].T


def _gla(proj3, s0, g_head, n_seq, seq_len, chunk):
    r = proj3.shape[1]
    nc = seq_len // chunk
    sb = min(GLA_SUB, chunk)
    out_dtype = BF16 if chunk % 16 == 0 else F32
    row = lambda b, c: b * nc + c
    body = functools.partial(_gla_body, C=chunk, SB=sb)
    return pl.pallas_call(
        body,
        grid=(n_seq, nc),
        in_specs=[pl.BlockSpec((1, chunk, PROJ_TILE), lambda b, c: (T_GQ, row(b, c), 0)),
                  pl.BlockSpec((1, chunk, PROJ_TILE), lambda b, c: (T_GK, row(b, c), 0)),
                  pl.BlockSpec((2, chunk, PROJ_TILE), lambda b, c: (T_GV // 2, row(b, c), 0)),
                  pl.BlockSpec((1, chunk, PROJ_TILE), lambda b, c: (T_GA, row(b, c), 0)),
                  pl.BlockSpec((2, chunk, PROJ_TILE), lambda b, c: (T_GR // 2, row(b, c), 0)),
                  pl.BlockSpec((1, GLA_HEADS, GLA_DK, GLA_DV), lambda b, c: (b, 0, 0, 0)),
                  pl.BlockSpec((1, GLA_DV), lambda b, c: (0, 0))],
        out_specs=[pl.BlockSpec((chunk, GLA_HEADS * GLA_DV), lambda b, c: (row(b, c), 0)),
                   pl.BlockSpec((1, GLA_HEADS, GLA_DK, GLA_DV), lambda b, c: (b, 0, 0, 0))],
        out_shape=[jax.ShapeDtypeStruct((r, GLA_HEADS * GLA_DV), out_dtype),
                   jax.ShapeDtypeStruct((n_seq, GLA_HEADS, GLA_DK, GLA_DV), F32)],
        scratch_shapes=[pltpu.VMEM((GLA_HEADS, GLA_DV, GLA_DK), F32)],
        compiler_params=_cparams(("arbitrary", "arbitrary")),
        name="gla",
    )(proj3, proj3, proj3, proj3, proj3, s0, g_head)


def _select_topk(g, n_valid_rows, n_rows, n_sel):
    row = lax.broadcasted_iota(I32, g.shape, 0)
    valid = row < n_valid_rows
    gm = jnp.where(valid, g, NEG_INF)
    rank = jnp.zeros(g.shape, F32)
    for jp in range(n_rows):
        gj = gm[jp:jp + 1, :]
        ahead = jnp.logical_or(gj > gm, jnp.logical_and(gj == gm, jp < row))
        rank = rank + jnp.where(ahead, 1.0, 0.0)
    return jnp.where(jnp.logical_and(valid, rank < n_sel), 1.0, 0.0)


def _moba_p_body(q_ref, k_ref, v_ref, o_ref, k_scr, vt_scr, km_scr, sel_scr, m_scr, l_scr, acc_scr, *, nb, nbp):
    i = pl.program_id(1)
    blk = MOBA_BLOCK
    scale = MOBA_HD ** -0.5

    @pl.when(i == 0)
    def _():
        km_scr[...] = jnp.zeros(km_scr.shape, F32)
        for j in range(nb):
            kj = k_ref[0, j * blk:(j + 1) * blk, :]
            k_scr[j] = kj.astype(BF16)
            km_scr[j:j + 1, :] = jnp.mean(kj, axis=0, keepdims=True)
            vt_scr[j] = v_ref[0, :, j * blk:(j + 1) * blk].astype(BF16)

    q = q_ref[0]
    km = km_scr[...]
    km_rep = jnp.concatenate([km] * MOBA_HEADS, axis=0)
    wrow = lax.broadcasted_iota(I32, km_rep.shape, 0)
    wlane = lax.broadcasted_iota(I32, km_rep.shape, 1)
    wt = jnp.where(_div(wlane, MOBA_HD) == _div(wrow, nbp), km_rep, 0.0)
    gates_t = _dot_nt_f32(wt, q)
    for h in range(MOBA_HEADS):
        sel_scr[h] = _select_topk(gates_t[h * nbp:(h + 1) * nbp, :], i, nb, MOBA_TOPK)

    n_pair = MOBA_HEADS // 2
    pair_lane = lax.broadcasted_iota(I32, (blk, LANES), 1)
    key_row = lax.broadcasted_iota(I32, (blk, 2 * blk), 0)
    q_col = _mod(lax.broadcasted_iota(I32, (blk, 2 * blk), 1), blk)
    qms = []
    for pr in range(n_pair):
        qp = q[:, pr * LANES:(pr + 1) * LANES]
        qms.append(jnp.concatenate([jnp.where(_div(pair_lane, MOBA_HD) == u, qp, 0.0) for u in range(2)],
                                   axis=0).astype(BF16))

    def scores(j, pr):
        kj = k_scr[j, :, pr * LANES:(pr + 1) * LANES]
        return _dot_nt(kj, qms[pr]) * scale

    def values(j, h):
        return vt_scr[j, h * MOBA_HD:(h + 1) * MOBA_HD, :]

    for pr in range(n_pair):
        s = jnp.where(key_row <= q_col, scores(i, pr), NEG_INF)
        m = jnp.max(s, axis=0, keepdims=True)
        p = jnp.exp(s - m)
        m_scr[pr] = m
        l_scr[pr] = jnp.sum(p, axis=0, keepdims=True)
        pb = p.astype(BF16)
        for u in range(2):
            acc_scr[2 * pr + u] = _dot(values(i, 2 * pr + u), pb[:, u * blk:(u + 1) * blk])

    def step(j, carry):
        for pr in range(n_pair):
            keep = jnp.concatenate([sel_scr[2 * pr + u, pl.ds(j, 1), :] for u in range(2)], axis=1)
            s = jnp.where(keep > 0.5, scores(j, pr), NEG_INF)
            m_old = m_scr[pr]
            m_new = jnp.maximum(m_old, jnp.max(s, axis=0, keepdims=True))
            alpha = jnp.exp(m_old - m_new)
            p = jnp.exp(s - m_new)
            m_scr[pr] = m_new
            l_scr[pr] = l_scr[pr] * alpha + jnp.sum(p, axis=0, keepdims=True)
            pb = p.astype(BF16)
            for u in range(2):
                h = 2 * pr + u
                cs = slice(u * blk, (u + 1) * blk)
                acc_scr[h] = acc_scr[h] * alpha[:, cs] + _dot(values(j, h), pb[:, cs])
        return carry

    lax.fori_loop(0, i, step, 0)
    outs = []
    for h in range(MOBA_HEADS):
        l = l_scr[h // 2][:, (h % 2) * blk:(h % 2 + 1) * blk]
        outs.append(acc_scr[h] / l)
    out_t = jnp.concatenate(outs, axis=0)
    o_ref[...] = out_t.T.astype(o_ref.dtype)


def _moba_prompt(proj3, v_t, n_seq, seq_len):
    r = proj3.shape[1]
    nb = seq_len // MOBA_BLOCK
    nbp = -(-nb // 8) * 8
    body = functools.partial(_moba_p_body, nb=nb, nbp=nbp)
    return pl.pallas_call(
        body,
        grid=(n_seq, nb),
        in_specs=[pl.BlockSpec((1, MOBA_BLOCK, MOBA_W), lambda b, i: (T_MQ, b * nb + i, 0)),
                  pl.BlockSpec((1, seq_len, MOBA_W), lambda b, i: (T_MK, b, 0)),
                  pl.BlockSpec((1, MOBA_W, seq_len), lambda b, i: (b, 0, 0))],
        out_specs=pl.BlockSpec((MOBA_BLOCK, MOBA_W), lambda b, i: (b * nb + i, 0)),
        out_shape=jax.ShapeDtypeStruct((r, MOBA_W), BF16),
        scratch_shapes=[pltpu.VMEM((nb, MOBA_BLOCK, MOBA_W), BF16),
                        pltpu.VMEM((nb, MOBA_W, MOBA_BLOCK), BF16),
                        pltpu.VMEM((nbp, MOBA_W), F32),
                        pltpu.VMEM((MOBA_HEADS, nbp, MOBA_BLOCK), F32),
                        pltpu.VMEM((MOBA_HEADS // 2, 1, 2 * MOBA_BLOCK), F32),
                        pltpu.VMEM((MOBA_HEADS // 2, 1, 2 * MOBA_BLOCK), F32),
                        pltpu.VMEM((MOBA_HEADS, MOBA_HD, MOBA_BLOCK), F32)],
        compiler_params=_cparams(("arbitrary", "arbitrary")),
        name="moba_prompt",
    )(proj3, proj3, v_t)


def _moba_s_body(pt_ref, *refs, nbp, L, bps):
    del pt_ref
    page_refs = refs[:4 * bps]
    qn_ref, kn_ref, vn_ref, o_ref, q2_scr, gate_scr, m_scr, l_scr, acc_scr = refs[4 * bps:]
    j = pl.program_id(1)
    n_steps = nbp // bps
    scale = MOBA_HD ** -0.5
    n_row = MOBA_HEADS * L
    blk_lane = lax.broadcasted_iota(I32, (n_row, LANES), 1)

    @pl.when(j == 0)
    def _():
        qn = qn_ref[...]
        q2 = jnp.concatenate([qn] * MOBA_HEADS, axis=0)
        row = lax.broadcasted_iota(I32, q2.shape, 0)
        lane = lax.broadcasted_iota(I32, q2.shape, 1)
        q2_scr[...] = jnp.where(_div(lane, MOBA_HD) == _div(row, L), q2, 0.0).astype(BF16)
        gate_scr[...] = jnp.full(gate_scr.shape, NEG_INF, F32)
        m_scr[...] = jnp.zeros(m_scr.shape, F32)
        l_scr[...] = jnp.zeros(l_scr.shape, F32)

    def softmax_partial(s):
        m = jnp.max(s, axis=1, keepdims=True)
        p = jnp.exp(s - m)
        return m, jnp.sum(p, axis=1, keepdims=True), p.astype(BF16)

    @pl.when(j < n_steps)
    def _():
        for t in range(bps):
            ka_ref, kb_ref, va_ref, vb_ref = page_refs[4 * t:4 * t + 4]
            jj = j * bps + t
            kt = jnp.concatenate([ka_ref[0], kb_ref[0]], axis=1).astype(BF16)
            vt = jnp.concatenate([va_ref[0], vb_ref[0]], axis=1).astype(BF16)
            s = _dot(q2_scr[...], kt)
            gate = jnp.mean(s, axis=1, keepdims=True)
            m, l, p = softmax_partial(s * scale)
            hot = blk_lane == jj
            gate_scr[...] = jnp.where(hot, gate, gate_scr[...])
            m_scr[...] = jnp.where(hot, m, m_scr[...])
            l_scr[...] = jnp.where(hot, l, l_scr[...])
            acc_scr[jj] = _dot_nt(p, vt)

    @pl.when(j == n_steps)
    def _():
        pad = jnp.zeros((LANES - L, MOBA_W), F32)
        kn = jnp.concatenate([kn_ref[...], pad], axis=0).astype(BF16)
        vn = jnp.concatenate([vn_ref[...], pad], axis=0).astype(BF16)
        s = _dot_nt(q2_scr[...], kn) * scale
        key = lax.broadcasted_iota(I32, s.shape, 1)
        qpos = _mod(lax.broadcasted_iota(I32, s.shape, 0), L)
        m_own, l_own, p_own = softmax_partial(jnp.where(key <= qpos, s, NEG_INF))
        acc_own = _dot(p_own, vn)
        g = gate_scr[...]
        rank = jnp.zeros(g.shape, F32)
        for jp in range(nbp):
            gj = g[:, jp:jp + 1]
            ahead = jnp.logical_or(gj > g, jnp.logical_and(gj == g, jp < blk_lane))
            rank = rank + jnp.where(ahead, 1.0, 0.0)
        sel = jnp.logical_and(blk_lane < nbp, rank < min(MOBA_TOPK, nbp + 1))
        m_all = m_scr[...]
        m_top = jnp.maximum(m_own, jnp.max(jnp.where(sel, m_all, NEG_INF), axis=1, keepdims=True))
        w = jnp.where(sel, jnp.exp(m_all - m_top), 0.0)
        w_own = jnp.exp(m_own - m_top)
        den = jnp.sum(w * l_scr[...], axis=1, keepdims=True) + w_own * l_own
        num = w_own * acc_own
        for jj in range(nbp):
            num = num + w[:, jj:jj + 1] * acc_scr[jj]
        out_r = num / den
        lane = lax.broadcasted_iota(I32, (L, MOBA_W), 1)
        out = jnp.zeros((L, MOBA_W), F32)
        for h in range(MOBA_HEADS):
            out = out + jnp.where(_div(lane, MOBA_HD) == h, out_r[h * L:(h + 1) * L, :], 0.0)
        o_ref[...] = out


def _moba_sample(proj3, cache_k, cache_v, page_table, n_seq, L):
    n_pool, page = cache_k.shape[0], cache_k.shape[1]
    n_pages = page_table.shape[1]
    ppb = MOBA_BLOCK // page
    nbp = n_pages // ppb
    n_row = MOBA_HEADS * L
    assert ppb == 2 and n_pages % ppb == 0 and page == LANES and nbp <= LANES and n_row % 16 == 0 and L <= LANES
    ck = jnp.transpose(cache_k, (0, 2, 3, 1)).reshape(n_pool, MOBA_W, page)
    cv = jnp.transpose(cache_v, (0, 2, 3, 1)).reshape(n_pool, MOBA_W, page)
    pt = page_table.reshape(-1).astype(I32)

    bps = MOBA_S_BLOCKS_PER_STEP if nbp % MOBA_S_BLOCKS_PER_STEP == 0 else 1
    n_steps = nbp // bps

    def page_map(t, off):
        return lambda b, j, pt_ref: (
            pt_ref[b * n_pages + ppb * (jnp.minimum(j, n_steps - 1) * bps + t) + off], 0, 0)

    new_map = lambda t: (lambda b, j, pt_ref: (t, b, 0))
    body = functools.partial(_moba_s_body, nbp=nbp, L=L, bps=bps)
    page_specs = [pl.BlockSpec((1, MOBA_W, page), page_map(t, off)) for t in range(bps) for off in (0, 1, 0, 1)]
    page_args = [a for _ in range(bps) for a in (ck, ck, cv, cv)]
    grid_spec = pltpu.PrefetchScalarGridSpec(
        num_scalar_prefetch=1,
        grid=(n_seq, n_steps + 1),
        in_specs=page_specs + [
                  pl.BlockSpec((None, L, MOBA_W), new_map(T_MQ)),
                  pl.BlockSpec((None, L, MOBA_W), new_map(T_MK)),
                  pl.BlockSpec((None, L, MOBA_W), new_map(T_MV))],
        out_specs=pl.BlockSpec((L, MOBA_W), lambda b, j, pt_ref: (b, 0)),
        scratch_shapes=[pltpu.VMEM((n_row, MOBA_W), BF16),
                        pltpu.VMEM((n_row, LANES), F32),
                        pltpu.VMEM((n_row, LANES), F32),
                        pltpu.VMEM((n_row, LANES), F32),
                        pltpu.VMEM((nbp, n_row, MOBA_W), F32)])
    return pl.pallas_call(
        body,
        grid_spec=grid_spec,
        out_shape=jax.ShapeDtypeStruct((n_seq * L, MOBA_W), F32),
        compiler_params=_cparams(("arbitrary", "arbitrary")),
        name="moba_sample",
    )(pt, *page_args, proj3, proj3, proj3)


def _post_body(og_ref, om_ref, ba_ref, bb_ref, x_ref, gt_ref, sc_ref, sh_ref, g2_ref, wpg_ref, wpm_ref,
               wo_ref, wr_ref, br_ref, x1_ref, h2_ref, lg_ref):
    ya = _dot(og_ref[...].astype(BF16), wpg_ref[...])
    yb = _dot(om_ref[...].astype(BF16), wpm_ref[...])
    ba = jnp.concatenate([ba_ref[0], ba_ref[1]], axis=1)
    bb = jnp.concatenate([bb_ref[0], bb_ref[1]], axis=1)
    mix = _sigmoid(ba) * ya + _sigmoid(bb) * yb
    x1 = x_ref[...] + gt_ref[0] * _dot(mix.astype(BF16), wo_ref[...])
    x1_ref[...] = x1
    ms = jnp.mean(x1 * x1, axis=-1, keepdims=True)
    h2 = x1 * lax.rsqrt(ms + EPS) * g2_ref[...] * (1.0 + sc_ref[0]) + sh_ref[0]
    _store_row_tiles(h2_ref, h2)
    hh = h2.astype(BF16)
    hl = (h2 - hh.astype(F32)).astype(BF16)
    wr = wr_ref[...]
    wh = wr.astype(BF16)
    wl = (wr - wh.astype(F32)).astype(BF16)
    lg_ref[...] = _dot(hh, wh) + _dot(hh, wl) + _dot(hl, wh) + br_ref[...]


def _post(og, om, proj3, x2, gt, sc, sh, g2, wpg, wpm, wo, wr, br, tm, rows_per_mod):
    r, d = x2.shape
    nt = r // tm
    tiles_per_mod = max(rows_per_mod // tm, 1)
    mod_block = (1,) + gt.shape[1:]
    mod_map = lambda i: (i // tiles_per_mod, 0, 0)
    full = lambda a: pl.BlockSpec(a.shape, lambda i: (0,) * a.ndim)
    return pl.pallas_call(
        _post_body,
        grid=(nt,),
        in_specs=[pl.BlockSpec((tm, og.shape[1]), lambda i: (i, 0)),
                  pl.BlockSpec((tm, om.shape[1]), lambda i: (i, 0)),
                  pl.BlockSpec((2, tm, PROJ_TILE), lambda i: (T_BA // 2, i, 0)),
                  pl.BlockSpec((2, tm, PROJ_TILE), lambda i: (T_BB // 2, i, 0)),
                  pl.BlockSpec((tm, d), lambda i: (i, 0)),
                  pl.BlockSpec(mod_block, mod_map),
                  pl.BlockSpec(mod_block, mod_map),
                  pl.BlockSpec(mod_block, mod_map),
                  full(g2), full(wpg), full(wpm), full(wo), full(wr), full(br)],
        out_specs=[pl.BlockSpec((tm, d), lambda i: (i, 0)),
                   pl.BlockSpec((tm * ROW_TILE, LANES), lambda i: (i, 0)),
                   pl.BlockSpec((tm, LANES), lambda i: (i, 0))],
        out_shape=[jax.ShapeDtypeStruct((r, d), F32),
                   jax.ShapeDtypeStruct((r * ROW_TILE, LANES), F32),
                   jax.ShapeDtypeStruct((r, LANES), F32)],
        compiler_params=_cparams(("arbitrary",)),
        name="post",
    )(og, om, proj3, proj3, x2, gt, sc, sh, g2, wpg, wpm, wo, wr, br)


def _route_body(lg_ref, eidx_ref, rank_ref, gate_ref, cnt_ref, run_scr):
    i = pl.program_id(0)
    tm = lg_ref.shape[0]

    @pl.when(i == 0)
    def _():
        run_scr[...] = jnp.zeros(run_scr.shape, F32)

    l = lg_ref[...]
    lane = lax.broadcasted_iota(I32, l.shape, 1)
    vals, hots = [], []
    for _ in range(TOP_K):
        m = jnp.max(l, axis=1, keepdims=True)
        idx = jnp.min(jnp.where(l == m, lane, LANES), axis=1, keepdims=True)
        hot = lane == idx
        vals.append(m)
        hots.append(hot)
        l = jnp.where(hot, NEG_INF, l)
    es = [jnp.exp(v - vals[0]) for v in vals]
    den = es[0] + es[1] + es[2] + es[3]
    chosen = jnp.zeros(l.shape, F32)
    for hot in hots:
        chosen = chosen + jnp.where(hot, 1.0, 0.0)
    row = lax.broadcasted_iota(I32, (tm, tm), 0)
    col = lax.broadcasted_iota(I32, (tm, tm), 1)
    before = jnp.where(row > col, 1.0, 0.0).astype(BF16)
    pos = _dot(before, chosen.astype(BF16)) + run_scr[...]
    eidx = jnp.zeros(l.shape, I32)
    rank = jnp.zeros(l.shape, I32)
    gate = jnp.zeros(l.shape, F32)
    for k in range(TOP_K):
        e_k = jnp.min(jnp.where(hots[k], lane, LANES), axis=1, keepdims=True)
        r_k = jnp.sum(jnp.where(hots[k], pos, 0.0), axis=1, keepdims=True).astype(I32)
        eidx = jnp.where(lane == k, e_k, eidx)
        rank = jnp.where(lane == k, r_k, rank)
        gate = jnp.where(lane == k, es[k] / den, gate)
    eidx_ref[...] = eidx
    rank_ref[...] = rank
    gate_ref[...] = gate
    run_scr[...] = run_scr[...] + jnp.sum(chosen, axis=0, keepdims=True)
    cnt_ref[...] = jnp.broadcast_to(run_scr[...], cnt_ref.shape)


def _route(logits, tm):
    t = logits.shape[0]
    blk = pl.BlockSpec((tm, LANES), lambda i: (i, 0))
    return pl.pallas_call(
        _route_body,
        grid=(t // tm,),
        in_specs=[blk],
        out_specs=[blk, blk, blk, pl.BlockSpec((8, LANES), lambda i: (0, 0))],
        out_shape=[jax.ShapeDtypeStruct((t, LANES), I32),
                   jax.ShapeDtypeStruct((t, LANES), I32),
                   jax.ShapeDtypeStruct((t, LANES), F32),
                   jax.ShapeDtypeStruct((8, LANES), F32)],
        scratch_shapes=[pltpu.VMEM((1, LANES), F32)],
        compiler_params=_cparams(("arbitrary",)),
        name="route",
    )(logits)


def _dispatch_body(dest_ref, ha_ref, hb_ref, zero_ref, xb_ref, sem, *, tm, tiles_a):
    del zero_ref
    i = pl.program_id(0)

    def scatter_tile(h_ref):
        def copy(r, k):
            dst = dest_ref[(i * tm + r) * TOP_K + k]
            return pltpu.make_async_copy(h_ref.at[pl.ds(pl.multiple_of(r * ROW_TILE, ROW_TILE), ROW_TILE), :],
                                         xb_ref.at[pl.ds(pl.multiple_of(dst * ROW_TILE, ROW_TILE), ROW_TILE), :],
                                         sem)

        def start(r, carry):
            for k in range(TOP_K):
                copy(r, k).start(priority=k % 2)
            return carry

        def wait(r, carry):
            for k in range(TOP_K):
                copy(r, k).wait()
            return carry

        lax.fori_loop(0, tm, start, 0)
        lax.fori_loop(0, tm, wait, 0)

    @pl.when(i < tiles_a)
    def _():
        scatter_tile(ha_ref)

    @pl.when(i >= tiles_a)
    def _():
        scatter_tile(hb_ref)


def _dispatch(dest_flat, h_a, h_b, n_rows, tm):
    tiles_a = h_a.shape[0] // (ROW_TILE * tm)
    tiles_b = h_b.shape[0] // (ROW_TILE * tm)
    assert tiles_a * tm * ROW_TILE == h_a.shape[0] and tiles_b * tm * ROW_TILE == h_b.shape[0]
    zeros = jnp.zeros((n_rows * ROW_TILE, LANES), h_a.dtype)
    grid_spec = pltpu.PrefetchScalarGridSpec(
        num_scalar_prefetch=1,
        grid=(tiles_a + tiles_b,),
        in_specs=[pl.BlockSpec((tm * ROW_TILE, LANES), lambda i, dr: (jnp.minimum(i, tiles_a - 1), 0)),
                  pl.BlockSpec((tm * ROW_TILE, LANES), lambda i, dr: (jnp.maximum(i - tiles_a, 0), 0)),
                  pl.BlockSpec(memory_space=pl.ANY)],
        out_specs=pl.BlockSpec(memory_space=pl.ANY),
        scratch_shapes=[pltpu.SemaphoreType.DMA(())])
    return pl.pallas_call(
        functools.partial(_dispatch_body, tm=tm, tiles_a=tiles_a),
        grid_spec=grid_spec,
        out_shape=jax.ShapeDtypeStruct((n_rows * ROW_TILE, LANES), h_a.dtype),
        input_output_aliases={3: 0},
        compiler_params=_cparams(("arbitrary",)),
        name="dispatch",
    )(dest_flat, h_a, h_b, zeros)


def _moe_body(be_ref, nu_ref, x_ref, wgu_ref, bgu_ref, wd_ref, bd_ref, y_ref, wgu_scr, wd_scr):
    i = pl.program_id(0)
    prev = be_ref[jnp.maximum(i - 1, 0)]
    fresh = jnp.logical_or(i == 0, be_ref[i] != prev)

    @pl.when(jnp.logical_and(fresh, i < nu_ref[0]))
    def _():
        wgu_scr[...] = wgu_ref[0].astype(BF16)
        wd_scr[...] = wd_ref[0].astype(BF16)

    @pl.when(i < nu_ref[0])
    def _():
        x = _load_row_tiles(x_ref, MOE_ROWS).astype(BF16)
        gu = _dot(x, wgu_scr[...]) + bgu_ref[0]
        glu = jnp.minimum(gu[:, :D_FF], SWIGLU_LIMIT)
        lin = jnp.clip(gu[:, D_FF:], -SWIGLU_LIMIT, SWIGLU_LIMIT)
        act = glu * _sigmoid(SWIGLU_ALPHA * glu) * (lin + 1.0)
        _store_row_tiles(y_ref, _dot(act.astype(BF16), wd_scr[...]) + bd_ref[0])

    @pl.when(i >= nu_ref[0])
    def _():
        y_ref[...] = jnp.zeros(y_ref.shape, F32)


def _moe(block_expert, n_used, xb, wgu, bgu, wd, bd):
    d = D_MODEL
    n_rows = xb.shape[0] // ROW_TILE
    nblk = n_rows // MOE_ROWS
    ne = wgu.shape[0]
    emap = lambda i, be, nu: (be[i], 0, 0)
    grid_spec = pltpu.PrefetchScalarGridSpec(
        num_scalar_prefetch=2,
        grid=(nblk,),
        in_specs=[pl.BlockSpec((MOE_ROWS * ROW_TILE, LANES), lambda i, be, nu: (i, 0)),
                  pl.BlockSpec((1, d, 2 * D_FF), emap),
                  pl.BlockSpec((1, 1, 2 * D_FF), emap),
                  pl.BlockSpec((1, D_FF, d), emap),
                  pl.BlockSpec((1, 1, d), emap)],
        out_specs=pl.BlockSpec((MOE_ROWS * ROW_TILE, LANES), lambda i, be, nu: (i, 0)),
        scratch_shapes=[pltpu.VMEM((d, 2 * D_FF), BF16), pltpu.VMEM((D_FF, d), BF16)])
    return pl.pallas_call(
        _moe_body,
        grid_spec=grid_spec,
        out_shape=jax.ShapeDtypeStruct((n_rows * ROW_TILE, LANES), F32),
        compiler_params=_cparams(("arbitrary",)),
        name="experts",
    )(block_expert, n_used, xb, wgu, bgu.reshape(ne, 1, 2 * D_FF), wd, bd.reshape(ne, 1, d))


def _combine_body(dest_ref, yb_ref, gate_ref, x1_ref, gt_ref, gf_ref, y_ref, buf, sem, *, tm, tok0):
    i = pl.program_id(0)

    def copy(r, k):
        src = dest_ref[(tok0 + i * tm + r) * TOP_K + k]
        return pltpu.make_async_copy(yb_ref.at[pl.ds(pl.multiple_of(src * ROW_TILE, ROW_TILE), ROW_TILE), :],
                                     buf.at[k, pl.ds(pl.multiple_of(r * ROW_TILE, ROW_TILE), ROW_TILE), :], sem)

    def start(r, carry):
        for k in range(TOP_K):
            copy(r, k).start(priority=k % 2)
        return carry

    def wait(r, carry):
        for k in range(TOP_K):
            copy(r, k).wait()
        return carry

    lax.fori_loop(0, tm, start, 0)
    lax.fori_loop(0, tm, wait, 0)
    g = gate_ref[...]
    y = g[:, 0:1] * _load_row_tiles(buf, tm, (0,))
    for k in range(1, TOP_K):
        y = y + g[:, k:k + 1] * _load_row_tiles(buf, tm, (k,))
    x2 = x1_ref[...] + gt_ref[0] * y
    ms = jnp.mean(x2 * x2, axis=-1, keepdims=True)
    y_ref[...] = x2 * lax.rsqrt(ms + EPS) * gf_ref[...]


def _combine(dest_flat, yb, gates, x1, gt, g_final, tm, rows_per_mod, tok0):
    r, d = x1.shape
    nt = r // tm
    tiles_per_mod = max(rows_per_mod // tm, 1)
    tile0 = tok0 // tm
    mod_block = (1,) + gt.shape[1:]
    grid_spec = pltpu.PrefetchScalarGridSpec(
        num_scalar_prefetch=1,
        grid=(nt,),
        in_specs=[pl.BlockSpec(memory_space=pl.ANY),
                  pl.BlockSpec((tm, LANES), lambda i, dr: (tile0 + i, 0)),
                  pl.BlockSpec((tm, d), lambda i, dr: (i, 0)),
                  pl.BlockSpec(mod_block, lambda i, dr: (i // tiles_per_mod, 0, 0)),
                  pl.BlockSpec((1, d), lambda i, dr: (0, 0))],
        out_specs=pl.BlockSpec((tm, d), lambda i, dr: (i, 0)),
        scratch_shapes=[pltpu.VMEM((TOP_K, tm * ROW_TILE, LANES), F32), pltpu.SemaphoreType.DMA(())])
    return pl.pallas_call(
        functools.partial(_combine_body, tm=tm, tok0=tok0),
        grid_spec=grid_spec,
        out_shape=jax.ShapeDtypeStruct((r, d), F32),
        compiler_params=_cparams(("arbitrary",)),
        name="combine",
    )(dest_flat, yb, gates, x1, gt, g_final)


def _rope_tables(pos):
    half = MOBA_HD // 2
    inv = 1.0 / (ROPE_THETA ** (jnp.arange(half, dtype=F32) / half))
    ang = pos.astype(F32)[:, None] * inv[None, :]
    cos = jnp.cos(ang)
    sin = jnp.sin(ang)
    cos_h = jnp.concatenate([cos, cos], axis=1)
    sin_h = jnp.concatenate([-sin, sin], axis=1)
    return jnp.tile(cos_h, (1, MOBA_HEADS)), jnp.tile(sin_h, (1, MOBA_HEADS))


def _row_tile(n, cap):
    t = min(n, cap)
    while n % t:
        t //= 2
    return t


def kernel(x_prompt, x_sample, cache_k, cache_v, state_gla, page_table, c_prompt, c_sample, w_ada, b_ada, g_norm1, w_in, w_gla_a2, b_gla_a2, g_gla_head, w_proj_gla, w_proj_moba, w_out, g_norm2, w_router, b_router, w_gate_up, b_gate_up, w_down, b_down, g_final):
    nb_p, seq, d = x_prompt.shape
    nb_s, dec = x_sample.shape[:2]
    past_len = page_table.shape[1] * cache_k.shape[2]
    assert w_ada.shape[0] == 1 and d == D_MODEL
    assert seq % MOBA_BLOCK == 0 and past_len % MOBA_BLOCK == 0 and dec <= MOBA_BLOCK and dec % 8 == 0
    r_p, r_s = nb_p * seq, nb_s * dec
    t_all = r_p + r_s

    c_all = jnp.concatenate([c_prompt, c_sample], axis=0)
    mod = _ada(c_all, w_ada[0], b_ada[0]).reshape(nb_p + nb_s, 6, d)
    mod_p = [mod[:nb_p, k].reshape(nb_p, 1, d) for k in range(6)]
    mod_s = [jnp.broadcast_to(mod[nb_p:, k][:, None, :], (nb_s, dec, d)).reshape(1, r_s, d) for k in range(6)]

    w = w_in[0]
    gkw = GLA_HEADS * GLA_DK
    gvw = GLA_HEADS * GLA_DV
    c_ga = 2 * gkw + 2 * gvw
    pad = jnp.zeros((d, PROJ_TILE - GLA_LOWRANK), w.dtype)
    w_in_p = jnp.concatenate([w[:, :c_ga + GLA_LOWRANK], pad, w[:, c_ga + GLA_LOWRANK:]], axis=1).astype(BF16)
    assert w_in_p.shape[1] == N_PROJ_TILES * PROJ_TILE
    w2_p = jnp.zeros((PROJ_TILE, gkw), BF16).at[:GLA_LOWRANK].set(w_gla_a2[0].astype(BF16))
    b2 = b_gla_a2[0].reshape(1, gkw)
    g1 = g_norm1[0].reshape(1, d)
    g2 = g_norm2[0].reshape(1, d)
    wpg = w_proj_gla[0].astype(BF16)
    wpm = w_proj_moba[0].astype(BF16)
    wo = w_out[0].astype(BF16)
    wr = jnp.zeros((d, LANES), F32).at[:, :N_EXPERTS].set(w_router[0])
    br = jnp.full((1, LANES), NEG_INF, F32).at[0, :N_EXPERTS].set(b_router[0])
    g_head = g_gla_head[0].reshape(1, GLA_DV)

    cos_p, sin_p = _rope_tables(jnp.arange(seq, dtype=I32))
    cos_s, sin_s = _rope_tables(past_len + jnp.arange(dec, dtype=I32))
    cos_s, sin_s = jnp.tile(cos_s, (nb_s, 1)), jnp.tile(sin_s, (nb_s, 1))

    xp2 = x_prompt.reshape(r_p, d)
    xs2 = x_sample.reshape(r_s, d)
    tm_p = _row_tile(seq, 1024)

    proj_p, kt_p, vt_p = _inproj(xp2, mod_p[1], mod_p[0], g1, w_in_p, w2_p, b2, cos_p, sin_p, tm_p, seq, True)
    s0_p = jnp.zeros((nb_p, GLA_HEADS, GLA_DK, GLA_DV), state_gla.dtype)
    og_p, st_p = _gla(proj_p, s0_p, g_head, nb_p, seq, _row_tile(seq, 64))
    om_p = _moba_prompt(proj_p, vt_p, nb_p, seq)
    tm_post = _row_tile(seq, 512)
    x1_p, h2_p, lg_p = _post(og_p, om_p, proj_p, xp2, mod_p[2], mod_p[4], mod_p[3], g2, wpg, wpm, wo, wr, br,
                             tm_post, seq)

    proj_s, k5_s, v5_s = _inproj(xs2, mod_s[1], mod_s[0], g1, w_in_p, w2_p, b2, cos_s, sin_s, r_s, r_s, False)
    og_s, st_s = _gla(proj_s, state_gla[0], g_head, nb_s, dec, dec)
    om_s = _moba_sample(proj_s, cache_k[0], cache_v[0], page_table, nb_s, dec)
    x1_s, h2_s, lg_s = _post(og_s, om_s, proj_s, xs2, mod_s[2], mod_s[4], mod_s[3], g2, wpg, wpm, wo, wr, br,
                             r_s, r_s)

    lg_all = jnp.concatenate([lg_p, lg_s], axis=0)
    tm_r = _row_tile(t_all, 256)
    eidx, rank, gates, cnt = _route(lg_all, tm_r)
    counts = cnt[0, :N_EXPERTS].astype(I32)
    nblk_e = (counts + MOE_ROWS - 1) // MOE_ROWS
    blk_end = jnp.cumsum(nblk_e)
    blk_start = blk_end - nblk_e
    chosen = eidx[:, :TOP_K, None] == jnp.arange(N_EXPERTS, dtype=I32)
    dest = jnp.sum(jnp.where(chosen, blk_start, 0), axis=-1) * MOE_ROWS + rank[:, :TOP_K]
    dest_flat = dest.reshape(-1).astype(I32)
    n_blocks = -(-(t_all * TOP_K) // MOE_ROWS) + N_EXPERTS
    blk_ids = jnp.arange(n_blocks, dtype=I32)
    block_expert = jnp.minimum(jnp.sum((blk_end[None, :] <= blk_ids[:, None]).astype(I32), axis=1), N_EXPERTS - 1)
    n_used = blk_end[-1:].astype(I32)
    tm_d = _row_tile(r_s, 256)
    assert r_p % tm_d == 0
    xb = _dispatch(dest_flat, h2_p, h2_s, n_blocks * MOE_ROWS, tm_d)
    yb = _moe(block_expert, n_used, xb, w_gate_up[0], b_gate_up[0], w_down[0], b_down[0])
    gf = g_final.reshape(1, d)
    tm_c = _row_tile(seq, 256)
    y_p = _combine(dest_flat, yb, gates, x1_p, mod_p[5], gf, tm_c, seq, 0)
    y_s = _combine(dest_flat, yb, gates, x1_s, mod_s[5], gf, r_s, r_s, r_p)

    to_out = lambda a: a.reshape(1, nb_p, MOBA_HEADS, MOBA_HD, seq).transpose(0, 1, 4, 2, 3)
    k_p = to_out(kt_p)
    v_p = to_out(vt_p)
    k_s = k5_s.reshape(1, nb_s, dec, MOBA_HEADS, MOBA_HD)
    v_s = v5_s.reshape(1, nb_s, dec, MOBA_HEADS, MOBA_HD)
    return (y_p.reshape(nb_p, seq, d), y_s.reshape(nb_s, dec, d), k_p, v_p, st_p[None],
            k_s, v_s, st_s[None])
```
